```python
import math
import jax
import jax.numpy as jnp
from jax import lax
import numpy as np

D_MODEL = 2048
BATCH = 4
SEQ = 2048
DEPTH = 2

CTX_LEN = 256
GRID_W = 64

HEAD_DIM = D_MODEL // 16
HY_GROUPS = 4
NA_HEADS = 6
GLA_HEADS = 6
HY_CH = HY_GROUPS * HEAD_DIM
NA_DIM = NA_HEADS * HEAD_DIM
GLA_DK = HEAD_DIM // 2
GLA_DV = HEAD_DIM
GLA_K = GLA_HEADS * GLA_DK
GLA_V = GLA_HEADS * GLA_DV
MIX_DIM = HY_CH + NA_DIM + GLA_V

SHORT_CONV = 3
HY_EMB_BANDS = 16
HY_EMB_DIM = 1 + 2 * HY_EMB_BANDS
HY_FILTER_WIDTH = 64
HY_DECAY_TARGET = 1e-2
HY_FAST_DECAY = 0.3
HY_SLOW_DECAY = 1.5
HY_MIN_DECAY = math.log(HY_DECAY_TARGET) / HY_SLOW_DECAY
HY_MAX_DECAY = math.log(HY_DECAY_TARGET) / HY_FAST_DECAY

NA_WIN_ROWS = 8
NA_WIN_COLS = 16
NEG_INF = -1e30

GLA_RANK = 16
GLA_GATE_NORM = 16.0
GLA_CHUNK = 64
ROPE_BASE = 10000.0

N_GROUPS = 4
EXPERTS_PER_GROUP = 8
TOP_K_IN_GROUP = 2
D_EXPERT = D_MODEL // 4

ALPHA = (2 * DEPTH) ** 0.25
BETA = (8 * DEPTH) ** -0.25
LN_EPS = 1e-6

IN_SPLITS = (3 * HY_CH, NA_DIM, NA_DIM, NA_DIM, GLA_K, GLA_K, GLA_V, GLA_V, 2 * GLA_RANK)
IN_COLS = sum(IN_SPLITS)

kernel_name = 'hybrid_hyena_natten_gla_hmoe_prefix_dit'


def layer_norm(x, g=None, b=None):
    xf = x.astype(jnp.float32)
    mu = jnp.mean(xf, axis=-1, keepdims=True)
    xc = xf - mu
    y = xc * lax.rsqrt(jnp.mean(xc * xc, axis=-1, keepdims=True) + LN_EPS)
    if g is not None:
        y = y * g.astype(jnp.float32) + b.astype(jnp.float32)
    return y.astype(x.dtype)


def rms_norm(x, w):
    xf = x.astype(jnp.float32)
    y = xf * lax.rsqrt(jnp.mean(xf * xf, axis=-1, keepdims=True) + LN_EPS) * w.astype(jnp.float32)
    return y.astype(x.dtype)


def split_columns(p):
    idx = [int(i) for i in np.cumsum(IN_SPLITS)[:-1]]
    return jnp.split(p, idx, axis=-1)


def to_heads(x, dh):
    b, l, _ = x.shape
    return x.reshape(b, l, -1, dh).transpose(0, 2, 1, 3)


def short_conv(u, w, b):
    l = u.shape[1]
    pad = SHORT_CONV // 2
    up = jnp.pad(u, ((0, 0), (pad, pad), (0, 0)))
    out = up[:, 0:l] * w[0]
    for j in range(1, SHORT_CONV):
        out = out + up[:, j:j + l] * w[j]
    return out + b


def hyena_filter(l, w1, b1, w2, b2, w3, sin_freq):
    f32 = jnp.float32
    n = jnp.arange(l, dtype=f32)
    t = n / max(l - 1, 1)
    bands = jnp.linspace(1e-4, HY_EMB_BANDS - 1, HY_EMB_BANDS, dtype=f32)
    ang = (2.0 * math.pi / l) * n[:, None] * bands[None, :]
    z = jnp.concatenate([t[:, None], jnp.cos(ang), -jnp.sin(ang)], axis=-1)
    fr = sin_freq.astype(f32)
    hid = jnp.sin(fr * (z @ w1.astype(f32) + b1.astype(f32)))
    hid = jnp.sin(fr * (hid @ w2.astype(f32) + b2.astype(f32)))
    h = (hid @ w3.astype(f32)).reshape(l, 2, HY_CH)
    deltas = jnp.abs(jnp.linspace(HY_MIN_DECAY, HY_MAX_DECAY, HY_CH, dtype=f32))
    h = h * jnp.exp(-t[:, None, None] * deltas)
    h_fwd = h[:, 0].T
    h_bwd = h[:, 1].T
    return jnp.concatenate([h_fwd, jnp.zeros((HY_CH, 1), f32), h_bwd[:, :0:-1]], axis=1)


def bidir_long_conv(u, k_circ, bias):
    l = u.shape[1]
    uf = u.astype(jnp.float32)
    u_f = jnp.fft.rfft(uf, n=2 * l, axis=1)
    k_f = jnp.fft.rfft(k_circ, axis=-1).T
    y = jnp.fft.irfft(u_f * k_f, n=2 * l, axis=1)[:, :l]
    return (y + uf * bias.astype(jnp.float32)).astype(u.dtype)


def hyena_branch(u, short_w, short_b, f_w1, f_b1, f_w2, f_b2, f_w3, sin_freq, bias):
    l = u.shape[1]
    u = short_conv(u, short_w, short_b)
    x0, x1, v = jnp.split(u, 3, axis=-1)
    k = hyena_filter(l, f_w1, f_b1, f_w2, f_b2, f_w3, sin_freq)
    return x0 * bidir_long_conv(x1 * v, k, bias)


def na_latent(q, k, v, k_ctx, v_ctx, rpb):
    b, l, _ = q.shape
    lc = k_ctx.shape[1]
    rows_n = l // GRID_W
    wr = min(NA_WIN_ROWS, rows_n)
    wc = NA_WIN_COLS
    scale = HEAD_DIM ** -0.5
    q = q.reshape(b, rows_n, GRID_W, NA_HEADS, HEAD_DIM) * scale
    k = k.reshape(b, rows_n, GRID_W, NA_HEADS, HEAD_DIM)
    v = v.reshape(b, rows_n, GRID_W, NA_HEADS, HEAD_DIM)
    k_ctx = k_ctx.reshape(b, lc, NA_HEADS, HEAD_DIM)
    v_ctx = v_ctx.reshape(b, lc, NA_HEADS, HEAD_DIM)
    r = jnp.arange(rows_n)
    r0 = jnp.clip(r - wr // 2, 0, rows_n - wr)
    rows = r0[:, None] + jnp.arange(wr)[None, :]
    col = jnp.arange(GRID_W)
    c0 = jnp.clip(col - wc // 2, 0, GRID_W - wc)
    col_ok = (col[None, :] >= c0[:, None]) & (col[None, :] < c0[:, None] + wc)
    k_blk = k[:, rows]
    v_blk = v[:, rows]
    s_loc = jnp.einsum('brqhd,brikhd->bhrqik', q, k_blk).astype(jnp.float32)
    rel_r = rows - r[:, None] + (NA_WIN_ROWS - 1)
    rel_c = jnp.clip(col[None, :] - col[:, None] + (wc - 1), 0, 2 * wc - 2)
    bias = rpb.astype(jnp.float32)[:, rel_r[:, None, :, None], rel_c[None, :, None, :]]
    s_loc = jnp.where(col_ok[None, None, None, :, None, :], s_loc + bias, NEG_INF)
    s_loc = s_loc.reshape(b, NA_HEADS, rows_n, GRID_W, wr * GRID_W)
    s_ctx = jnp.einsum('brqhd,bkhd->bhrqk', q, k_ctx).astype(jnp.float32)
    p = jax.nn.softmax(jnp.concatenate([s_loc, s_ctx], axis=-1), axis=-1).astype(v.dtype)
    p_loc = p[..., :wr * GRID_W].reshape(b, NA_HEADS, rows_n, GRID_W, wr, GRID_W)
    p_ctx = p[..., wr * GRID_W:]
    o = (jnp.einsum('bhrqik,brikhd->brqhd', p_loc, v_blk)
         + jnp.einsum('bhrqk,bkhd->brqhd', p_ctx, v_ctx))
    return o.reshape(b, l, NA_DIM)


def na_context(q, k, v):
    b, l, _ = q.shape
    q = q.reshape(b, l, NA_HEADS, HEAD_DIM) * (HEAD_DIM ** -0.5)
    k = k.reshape(b, l, NA_HEADS, HEAD_DIM)
    v = v.reshape(b, l, NA_HEADS, HEAD_DIM)
    s = jnp.einsum('bqhd,bkhd->bhqk', q, k).astype(jnp.float32)
    p = jax.nn.softmax(s, axis=-1).astype(v.dtype)
    return jnp.einsum('bhqk,bkhd->bqhd', p, v).reshape(b, l, NA_DIM)


def rope_2d(x, rows, cols):
    half = x.shape[-1] // 2
    quarter = half // 2
    inv = ROPE_BASE ** (-jnp.arange(quarter, dtype=jnp.float32) / quarter)

    def rot(xa, pos):
        ang = pos[:, None] * inv[None, :]
        cos = jnp.cos(ang).astype(x.dtype)
        sin = jnp.sin(ang).astype(x.dtype)
        x1, x2 = xa[..., :quarter], xa[..., quarter:]
        return jnp.concatenate([x1 * cos - x2 * sin, x1 * sin + x2 * cos], axis=-1)

    return jnp.concatenate([rot(x[..., :half], rows), rot(x[..., half:], cols)], axis=-1)


def gla_log_decay(a_lr, w2, b):
    outs = []
    for d in range(2):
        z = a_lr[..., d * GLA_RANK:(d + 1) * GLA_RANK] @ w2[d] + b[d]
        outs.append(to_heads(jax.nn.log_sigmoid(z.astype(jnp.float32)) / GLA_GATE_NORM, GLA_DK))
    return outs[0], outs[1]


def gla_scan(q, k, v, log_a, h0):
    b, h, l, _ = k.shape
    dv = v.shape[-1]
    n = l // GLA_CHUNK

    def chunks(a):
        return jnp.moveaxis(a.astype(jnp.float32).reshape(b, h, n, GLA_CHUNK, a.shape[-1]), 2, 0)

    with_out = q is not None
    causal = jnp.tril(jnp.ones((GLA_CHUNK, GLA_CHUNK), dtype=bool))
    xs = (chunks(k), chunks(v), chunks(log_a)) + ((chunks(q),) if with_out else ())

    def step(state, xs_i):
        k_i, v_i, a_i = xs_i[0], xs_i[1], xs_i[2]
        cum = jnp.cumsum(a_i, axis=-2)
        cum_last = cum[..., -1:, :]
        new_state = (state * jnp.exp(cum_last)[..., 0, :, None]
                     + jnp.einsum('bhck,bhcv->bhkv', k_i * jnp.exp(cum_last - cum), v_i))
        if not with_out:
            return new_state, None
        q_i = xs_i[3]
        diff = cum[..., :, None, :] - cum[..., None, :, :]
        decay = jnp.exp(jnp.where(causal[:, :, None], diff, -jnp.inf))
        att = jnp.einsum('bhtk,bhsk,bhtsk->bhts', q_i, k_i, decay)
        o = (jnp.einsum('bhck,bhkv->bhcv', q_i * jnp.exp(cum), state)
             + jnp.einsum('bhts,bhsv->bhtv', att, v_i))
        return new_state, o

    state, o = lax.scan(step, h0, xs)
    if with_out:
        o = jnp.moveaxis(o, 0, 2).reshape(b, h, l, dv).astype(v.dtype)
    return o, state


def gla_output(o, g, norm_w):
    b, _, l, _ = o.shape
    o = rms_norm(o.transpose(0, 2, 1, 3), norm_w).reshape(b, l, GLA_V)
    return o * jax.nn.silu(g)


def token_mixers(h, hc, w_in, hy_short_w, hy_short_b, hy_f_w1, hy_f_b1, hy_f_w2, hy_f_b2, hy_f_w3,
                 hy_sin_freq, hy_bias, hy_norm_w, na_rpb, na_norm_w, gla_a_w2, gla_a_b, gla_norm_w,
                 w_out, ctx_out):
    hy_l, nq_l, nk_l, nv_l, gq_l, gk_l, gv_l, gg_l, ga_l = split_columns(h @ w_in)
    hy_c, nq_c, nk_c, nv_c, gq_c, gk_c, gv_c, gg_c, ga_c = split_columns(hc @ w_in)
    hy_args = (hy_short_w, hy_short_b, hy_f_w1, hy_f_b1, hy_f_w2, hy_f_b2, hy_f_w3, hy_sin_freq, hy_bias)

    y_a = hyena_branch(hy_l, *hy_args)
    y_b = na_latent(nq_l, nk_l, nv_l, nk_c, nv_c, na_rpb)
    b, l, _ = h.shape
    t = jnp.arange(l)
    rows = (t // GRID_W).astype(jnp.float32)
    cols = (t % GRID_W).astype(jnp.float32)
    qs = GLA_DK ** -0.5
    q = rope_2d(to_heads(gq_l * qs, GLA_DK), rows, cols)
    k = rope_2d(to_heads(gk_l, GLA_DK), rows, cols)
    v = to_heads(gv_l, GLA_DV)
    la_f, la_b = gla_log_decay(ga_l, gla_a_w2, gla_a_b)
    k_c = to_heads(gk_c, GLA_DK)
    v_c = to_heads(gv_c, GLA_DV)
    la_fc, la_bc = gla_log_decay(ga_c, gla_a_w2, gla_a_b)
    q_c = to_heads(gq_c * qs, GLA_DK) if ctx_out else None
    q_cb = jnp.flip(q_c, 2) if ctx_out else None
    h0 = jnp.zeros((b, GLA_HEADS, GLA_DK, GLA_DV), jnp.float32)
    o_cf, s_f = gla_scan(q_c, k_c, v_c, la_fc, h0)
    o_cb, s_b = gla_scan(q_cb, jnp.flip(k_c, 2), jnp.flip(v_c, 2), jnp.flip(la_bc, 2), h0)
    o_f, _ = gla_scan(q, k, v, la_f, s_f)
    o_b, _ = gla_scan(jnp.flip(q, 2), jnp.flip(k, 2), jnp.flip(v, 2), jnp.flip(la_b, 2), s_b)
    y_c = gla_output(o_f + jnp.flip(o_b, 2), gg_l, gla_norm_w)

    y = jnp.concatenate([rms_norm(y_a, hy_norm_w), rms_norm(y_b, na_norm_w), y_c], axis=-1) @ w_out
    if not ctx_out:
        return y, None
    yc_a = hyena_branch(hy_c, *hy_args)
    yc_b = na_context(nq_c, nk_c, nv_c)
    yc_c = gla_output(o_cf + jnp.flip(o_cb, 2), gg_c, gla_norm_w)
    yc = jnp.concatenate([rms_norm(yc_a, hy_norm_w), rms_norm(yc_b, na_norm_w), yc_c], axis=-1) @ w_out
    return y, yc


def hier_moe(h, w_rg, b_rg, w_re, b_re, w_up, w_down):
    n = h.shape[0]
    p_group = jax.nn.softmax((h @ w_rg).astype(jnp.float32) + b_rg.astype(jnp.float32), axis=-1)
    top_pg, gidx = lax.top_k(p_group, 1)
    gsel = jax.nn.one_hot(gidx[:, 0], N_GROUPS, dtype=jnp.float32)
    le = ((h @ w_re).astype(jnp.float32) + b_re.astype(jnp.float32)).reshape(n, N_GROUPS, EXPERTS_PER_GROUP)
    le_sel = jnp.einsum('nge,ng->ne', le, gsel)
    top_v, eidx = lax.top_k(le_sel, TOP_K_IN_GROUP)
    w_sel = jax.nn.softmax(top_v, axis=-1) * top_pg
    e_gate = jnp.einsum('nke,nk->ne', jax.nn.one_hot(eidx, EXPERTS_PER_GROUP, dtype=jnp.float32), w_sel)
    gate = (gsel[:, :, None] * e_gate[:, None, :]).astype(h.dtype)
    out = jnp.zeros_like(h)
    for gi in range(N_GROUPS):
        gu = jnp.einsum('nd,edf->nef', h, w_up[gi])
        act = jax.nn.silu(gu[..., :D_EXPERT]) * gu[..., D_EXPERT:]
        out = out + jnp.einsum('nef,efd->nd', act * gate[:, gi, :, None], w_down[gi])
    return out


def trunk_layer(x, xc, c, c_ctx, w_ada, b_ada, w_in, hy_short_w, hy_short_b, hy_f_w1, hy_f_b1, hy_f_w2,
                hy_f_b2, hy_f_w3, hy_sin_freq, hy_bias, hy_norm_w, na_rpb, na_norm_w, gla_a_w2, gla_a_b,
                gla_norm_w, w_out, ln1_g, ln1_b, w_rg, b_rg, w_re, b_re, w_up, w_down, ln2_g, ln2_b, last):
    d = x.shape[-1]
    mod = jax.nn.silu(c) @ w_ada + b_ada
    mod_c = jax.nn.silu(c_ctx) @ w_ada + b_ada
    sh1, sc1, g1, sh2, sc2, g2 = [m[:, None, :] for m in jnp.split(mod, 6, axis=-1)]
    sh1c, sc1c, g1c, sh2c, sc2c, g2c = jnp.split(mod_c, 6)

    h = layer_norm(x) * (1 + sc1) + sh1
    hc = layer_norm(xc) * (1 + sc1c) + sh1c
    y, yc = token_mixers(h, hc, w_in, hy_short_w, hy_short_b, hy_f_w1, hy_f_b1, hy_f_w2, hy_f_b2, hy_f_w3,
                         hy_sin_freq, hy_bias, hy_norm_w, na_rpb, na_norm_w, gla_a_w2, gla_a_b, gla_norm_w,
                         w_out, not last)
    x = layer_norm(ALPHA * x + g1 * y, ln1_g, ln1_b)
    h2 = layer_norm(x) * (1 + sc2) + sh2
    b, l, _ = x.shape
    if last:
        f = hier_moe(h2.reshape(b * l, d), w_rg, b_rg, w_re, b_re, w_up, w_down).reshape(b, l, d)
        return layer_norm(ALPHA * x + g2 * f, ln2_g, ln2_b), xc
    xc = layer_norm(ALPHA * xc + g1c * yc, ln1_g, ln1_b)
    h2c = layer_norm(xc) * (1 + sc2c) + sh2c
    lc = xc.shape[1]
    f_all = hier_moe(jnp.concatenate([h2.reshape(b * l, d), h2c.reshape(b * lc, d)], axis=0),
                     w_rg, b_rg, w_re, b_re, w_up, w_down)
    f = f_all[:b * l].reshape(b, l, d)
    fc = f_all[b * l:].reshape(b, lc, d)
    x = layer_norm(ALPHA * x + g2 * f, ln2_g, ln2_b)
    xc = layer_norm(ALPHA * xc + g2c * fc, ln2_g, ln2_b)
    return x, xc


def setup_inputs(seed: int = 0) -> dict:
    key = jax.random.key(seed)
    ks = iter(jax.random.split(key, 48))
    d = D_MODEL
    ne = N_GROUPS * EXPERTS_PER_GROUP

    def nrm(shape, scale):
        return jax.random.normal(next(ks), shape, jnp.float32) * scale

    return {
        'x': nrm((BATCH, SEQ, d), 1.0),
        'c': nrm((BATCH, d), 1.0),
        'ctx': nrm((BATCH, CTX_LEN, d), 1.0),
        'c_ctx': nrm((d,), 1.0),
        'w_ada': nrm((DEPTH, d, 6 * d), d ** -0.5),
        'b_ada': nrm((DEPTH, 6 * d), 0.01),
        'w_in': nrm((DEPTH, d, IN_COLS), d ** -0.5),
        'hy_short_w': nrm((DEPTH, SHORT_CONV, 3 * HY_CH), SHORT_CONV ** -0.5),
        'hy_short_b': nrm((DEPTH, 3 * HY_CH), 0.01),
        'hy_f_w1': nrm((DEPTH, HY_EMB_DIM, HY_FILTER_WIDTH), HY_EMB_DIM ** -0.5),
        'hy_f_b1': nrm((DEPTH, HY_FILTER_WIDTH), 0.02),
        'hy_f_w2': nrm((DEPTH, HY_FILTER_WIDTH, HY_FILTER_WIDTH), HY_FILTER_WIDTH ** -0.5),
        'hy_f_b2': nrm((DEPTH, HY_FILTER_WIDTH), 0.02),
        'hy_f_w3': nrm((DEPTH, HY_FILTER_WIDTH, 2 * HY_CH), HY_FILTER_WIDTH ** -0.5),
        'hy_sin_freq': 1.0 + nrm((DEPTH, HY_FILTER_WIDTH), 0.01),
        'hy_bias': nrm((DEPTH, HY_CH), 0.1),
        'hy_norm_w': 1.0 + nrm((DEPTH, HY_CH), 0.01),
        'na_rpb': nrm((DEPTH, NA_HEADS, 2 * NA_WIN_ROWS - 1, 2 * NA_WIN_COLS - 1), 0.02),
        'na_norm_w': 1.0 + nrm((DEPTH, NA_DIM), 0.01),
        'gla_a_w2': nrm((DEPTH, 2, GLA_RANK, GLA_K), GLA_RANK ** -0.5),
        'gla_a_b': nrm((DEPTH, 2, GLA_K), 0.01),
        'gla_norm_w': 1.0 + nrm((DEPTH, GLA_DV), 0.01),
        'w_out': nrm((DEPTH, MIX_DIM, d), MIX_DIM ** -0.5 * BETA),
        'ln1_g': 1.0 + nrm((DEPTH, d), 0.01),
        'ln1_b': nrm((DEPTH, d), 0.01),
        'w_rg': nrm((DEPTH, d, N_GROUPS), d ** -0.5),
        'b_rg': nrm((DEPTH, N_GROUPS), 0.01),
        'w_re': nrm((DEPTH, d, ne), d ** -0.5),
        'b_re': nrm((DEPTH, ne), 0.01),
        'w_up': nrm((DEPTH, N_GROUPS, EXPERTS_PER_GROUP, d, 2 * D_EXPERT), d ** -0.5),
        'w_down': nrm((DEPTH, N_GROUPS, EXPERTS_PER_GROUP, D_EXPERT, d), D_EXPERT ** -0.5 * BETA),
        'ln2_g': 1.0 + nrm((DEPTH, d), 0.01),
        'ln2_b': nrm((DEPTH, d), 0.01),
    }


def reference(x, c, ctx, c_ctx, w_ada, b_ada, w_in, hy_short_w, hy_short_b, hy_f_w1, hy_f_b1, hy_f_w2,
              hy_f_b2, hy_f_w3, hy_sin_freq, hy_bias, hy_norm_w, na_rpb, na_norm_w, gla_a_w2, gla_a_b,
              gla_norm_w, w_out, ln1_g, ln1_b, w_rg, b_rg, w_re, b_re, w_up, w_down, ln2_g, ln2_b):
    xc = ctx
    for i in range(DEPTH):
        x, xc = trunk_layer(x, xc, c, c_ctx, w_ada[i], b_ada[i], w_in[i], hy_short_w[i], hy_short_b[i],
                            hy_f_w1[i], hy_f_b1[i], hy_f_w2[i], hy_f_b2[i], hy_f_w3[i], hy_sin_freq[i],
                            hy_bias[i], hy_norm_w[i], na_rpb[i], na_norm_w[i], gla_a_w2[i], gla_a_b[i],
                            gla_norm_w[i], w_out[i], ln1_g[i], ln1_b[i], w_rg[i], b_rg[i], w_re[i], b_re[i],
                            w_up[i], w_down[i], ln2_g[i], ln2_b[i], i == DEPTH - 1)
    return x
```

```python
import functools
import math

import jax
import jax.numpy as jnp
import numpy as np
from jax import lax
from jax.experimental import pallas as pl
from jax.experimental.pallas import tpu as pltpu

F32 = jnp.float32
BF16 = jnp.bfloat16
HIGHEST = lax.Precision.HIGHEST

D_MODEL = 2048
BATCH = 4
SEQ = 2048
DEPTH = 2
CTX_LEN = 256
GRID_W = 64
HEAD_DIM = 128
HY_CH = 512
NA_HEADS = 6
NA_DIM = 768
GLA_HEADS = 6
GLA_DK = 64
GLA_DV = 128
GLA_K = 384
GLA_V = 768
GLA_RANK = 16
GLA_GATE_NORM = 16.0
GLA_CHUNK = 64
ROPE_BASE = 10000.0
SHORT_CONV = 3
HY_EMB_BANDS = 16
HY_FILTER_WIDTH = 64
HY_MIN_DECAY = math.log(1e-2) / 1.5
HY_MAX_DECAY = math.log(1e-2) / 0.3
NA_WIN_ROWS = 8
NA_WIN_COLS = 16
NEG_INF = -1e30
N_GROUPS = 4
EXPERTS_PER_GROUP = 8
N_EXPERTS = N_GROUPS * EXPERTS_PER_GROUP
D_EXPERT = 512
ALPHA = (2 * DEPTH) ** 0.25
LN_EPS = 1e-6

T_LAT = BATCH * SEQ
T_CTX = BATCH * CTX_LEN
T_ALL = T_LAT + T_CTX
GRID_ROWS = SEQ // GRID_W

COL_HY = 0
COL_NQ = 3 * HY_CH
COL_NK = COL_NQ + NA_DIM
COL_NV = COL_NK + NA_DIM
COL_GQ = COL_NV + NA_DIM
COL_GK = COL_GQ + GLA_K
COL_GV = COL_GK + GLA_K
COL_GG = COL_GV + GLA_V
COL_GA = COL_GG + GLA_V
P_COLS = COL_GA

VMEM_LIMIT_BYTES = 56 * 1024 * 1024
ROW_TILE = 256
INPROJ_ROWS = 1024
INPROJ_COLS = 512
INPROJ_CHUNK = 128
HY_COLS = 256
DFT_CHUNK = 512
NA_QROWS = 4
NA_KROWS = NA_QROWS + NA_WIN_ROWS - 1
MOE_TILE = 256


def _cparams(sem, vmem=VMEM_LIMIT_BYTES):
    return pltpu.CompilerParams(dimension_semantics=sem, vmem_limit_bytes=vmem)


def _layer_norm_rows(x):
    mu = jnp.mean(x, axis=-1, keepdims=True)
    xc = x - mu
    return xc * lax.rsqrt(jnp.mean(xc * xc, axis=-1, keepdims=True) + LN_EPS)


def _rms_rows(x):
    return x * lax.rsqrt(jnp.mean(x * x, axis=-1, keepdims=True) + LN_EPS)


def _silu(x):
    return x / (1.0 + jnp.exp(-x))


def _mod_row(tile_rows):
    per_batch = SEQ // tile_rows
    return lambda i: jnp.minimum(i // per_batch, BATCH)


def _ada_body(c_ref, w_ref, b_ref, o_ref):
    s = _silu(c_ref[...])
    o_ref[...] = jnp.dot(s, w_ref[...], preferred_element_type=F32, precision=HIGHEST) + b_ref[...]


def _ada(c8, w_ada, b_ada):
    tn = 1024
    n = 6 * D_MODEL
    return pl.pallas_call(
        _ada_body,
        grid=(n // tn,),
        in_specs=[pl.BlockSpec((8, D_MODEL), lambda j: (0, 0)),
                  pl.BlockSpec((D_MODEL, tn), lambda j: (0, j)),
                  pl.BlockSpec((1, tn), lambda j: (0, j))],
        out_specs=pl.BlockSpec((8, tn), lambda j: (0, j)),
        out_shape=jax.ShapeDtypeStruct((8, n), F32),
        compiler_params=_cparams(("parallel",)),
        name="ada_mod",
    )(c8, w_ada, b_ada.reshape(1, n))


def _inproj_body(x_ref, mod_ref, w_ref, wga_ref, p_ref, ga_ref, h_scr):
    @pl.when(pl.program_id(1) == 0)
    def _():
        sh = mod_ref[0, 0:1, :]
        sc = mod_ref[0, 1:2, :]

        def chunk(r, carry):
            rows = pl.ds(pl.multiple_of(r * INPROJ_CHUNK, INPROJ_CHUNK), INPROJ_CHUNK)
            h = _layer_norm_rows(x_ref[rows, :]) * (1.0 + sc) + sh
            h_scr[rows, :] = h.astype(BF16)
            return carry

        lax.fori_loop(0, INPROJ_ROWS // INPROJ_CHUNK, chunk, 0)
        ga_ref[...] = jnp.dot(h_scr[...], wga_ref[...], preferred_element_type=F32)

    p_ref[...] = jnp.dot(h_scr[...], w_ref[...], preferred_element_type=F32)


def _inproj(xa, mod, w_in_bf, w_ga_bf):
    tm, tn = INPROJ_ROWS, INPROJ_COLS
    mrow = _mod_row(tm)
    return pl.pallas_call(
        _inproj_body,
        grid=(T_ALL // tm, P_COLS // tn),
        in_specs=[pl.BlockSpec((tm, D_MODEL), lambda i, j: (i, 0)),
                  pl.BlockSpec((1, 6, D_MODEL), lambda i, j: (mrow(i), 0, 0)),
                  pl.BlockSpec((D_MODEL, tn), lambda i, j: (0, j)),
                  pl.BlockSpec((D_MODEL, 2 * GLA_RANK), lambda i, j: (0, 0))],
        out_specs=[pl.BlockSpec((tm, tn), lambda i, j: (i, j)),
                   pl.BlockSpec((tm, 2 * GLA_RANK), lambda i, j: (i, 0))],
        out_shape=[jax.ShapeDtypeStruct((T_ALL, P_COLS), F32),
                   jax.ShapeDtypeStruct((T_ALL, 2 * GLA_RANK), F32)],
        scratch_shapes=[pltpu.VMEM((tm, D_MODEL), BF16)],
        compiler_params=_cparams(("parallel", "arbitrary")),
        name="inproj",
    )(xa, mod, w_in_bf, w_ga_bf)


def _dft_tables(lx):
    n = 2 * lx
    k = jnp.arange(lx, dtype=jnp.int32)
    idx = (k[:, None] * k[None, :]) % n
    ang = idx.astype(F32) * (2.0 * math.pi / n)
    return jnp.cos(ang).astype(BF16), jnp.sin(ang).astype(BF16)


def _hyena_features(lx):
    n = jnp.arange(lx, dtype=F32)
    t = n / max(lx - 1, 1)
    bands = jnp.linspace(1e-4, HY_EMB_BANDS - 1, HY_EMB_BANDS, dtype=F32)
    ang = (2.0 * math.pi / lx) * n[:, None] * bands[None, :]
    z = jnp.concatenate([t[:, None], jnp.cos(ang), -jnp.sin(ang)], axis=-1)
    z = jnp.pad(z, ((0, 0), (0, 128 - z.shape[1])))
    deltas = jnp.abs(jnp.linspace(HY_MIN_DECAY, HY_MAX_DECAY, HY_CH, dtype=F32))
    win = jnp.exp(-t[:, None] * deltas[None, :])
    return z, win


def _split_dot(tab, x):
    xh = x.astype(BF16)
    xl = (x - xh.astype(F32)).astype(BF16)
    return (jnp.dot(tab, xh, preferred_element_type=F32) + jnp.dot(tab, xl, preferred_element_type=F32))


def _hyfilt_body(z_ref, w1_ref, b1_ref, w2_ref, b2_ref, w3f_ref, w3b_ref, fr_ref, win_ref, c_ref, s_ref,
                 kre_ref, kim_ref):
    fr = fr_ref[...]
    hid = jnp.sin(fr * (jnp.dot(z_ref[...], w1_ref[...], preferred_element_type=F32, precision=HIGHEST) + b1_ref[...]))
    hid = jnp.sin(fr * (jnp.dot(hid, w2_ref[...], preferred_element_type=F32, precision=HIGHEST) + b2_ref[...]))
    win = win_ref[...]
    hf = jnp.dot(hid, w3f_ref[...], preferred_element_type=F32, precision=HIGHEST) * win
    hb = jnp.dot(hid, w3b_ref[...], preferred_element_type=F32, precision=HIGHEST) * win
    row = lax.broadcasted_iota(jnp.int32, hf.shape, 0)
    hb = jnp.where(row == 0, 0.0, hb)
    hs = hf + hb
    hd = hf - hb
    sgn = jnp.where((row & 1) == 0, 1.0, -1.0)
    k_nyq = jnp.sum(hs * sgn, axis=0, keepdims=True)
    kre_ref[...] = _split_dot(c_ref[...], hs)
    kim = -_split_dot(s_ref[...], hd)
    kim_ref[...] = jnp.where(row == 0, k_nyq, kim)


def _hyena_filter(lx, z, win, ctab, stab, w1p, b1, w2, b2, w3, fr):
    tc = HY_COLS
    nj = HY_CH // tc
    full = lambda shape: pl.BlockSpec(shape, lambda j: (0,) * len(shape))
    return pl.pallas_call(
        _hyfilt_body,
        grid=(nj,),
        in_specs=[full((lx, 128)), full((128, HY_FILTER_WIDTH)), full((1, HY_FILTER_WIDTH)),
                  full((HY_FILTER_WIDTH, HY_FILTER_WIDTH)), full((1, HY_FILTER_WIDTH)),
                  pl.BlockSpec((HY_FILTER_WIDTH, tc), lambda j: (0, j)),
                  pl.BlockSpec((HY_FILTER_WIDTH, tc), lambda j: (0, nj + j)),
                  full((1, HY_FILTER_WIDTH)),
                  pl.BlockSpec((lx, tc), lambda j: (0, j)),
                  full((lx, lx)), full((lx, lx))],
        out_specs=[pl.BlockSpec((lx, tc), lambda j: (0, j)), pl.BlockSpec((lx, tc), lambda j: (0, j))],
        out_shape=[jax.ShapeDtypeStruct((lx, HY_CH), F32), jax.ShapeDtypeStruct((lx, HY_CH), F32)],
        compiler_params=_cparams(("arbitrary",)),
        name=f"hyena_filter_{lx}",
    )(z, w1p, b1, w2, b2, w3, w3, fr, win, ctab, stab)


def _hyena_body(x0_ref, x1_ref, v_ref, w0_ref, w1_ref, w2_ref, b0_ref, b1_ref, b2_ref, kre_ref, kim_ref,
                bias_ref, c_ref, s_ref, o_ref, x0_scr, wv_scr, wb_scr, yre_scr, yim_scr, *, lx):
    tc = HY_COLS
    ch = min(DFT_CHUNK, lx)
    row = lax.broadcasted_iota(jnp.int32, (lx, tc), 0)
    first = row == 0
    last = row == lx - 1

    def short_conv(x_ref, w_ref, b_ref):
        x = x_ref[...]
        xm = jnp.where(first, 0.0, pltpu.roll(x, 1, 0))
        xp = jnp.where(last, 0.0, pltpu.roll(x, lx - 1, 0))
        return xm * w_ref[0:1, :] + x * w_ref[1:2, :] + xp * w_ref[2:3, :] + b_ref[...]

    x0_scr[...] = short_conv(x0_ref, w0_ref, b0_ref)
    wv = short_conv(x1_ref, w1_ref, b1_ref) * short_conv(v_ref, w2_ref, b2_ref)
    wv_scr[...] = wv
    wb_scr[...] = wv.astype(BF16)
    sgn = jnp.where((row & 1) == 0, 1.0, -1.0)
    u_nyq = jnp.sum(wv * sgn, axis=0, keepdims=True)
    y_nyq = u_nyq * kim_ref[0:1, :]

    for kc in range(lx // ch):
        rows = slice(kc * ch, (kc + 1) * ch)
        wb = wb_scr[...]
        ure = jnp.dot(c_ref[rows, :], wb, preferred_element_type=F32)
        uim = -jnp.dot(s_ref[rows, :], wb, preferred_element_type=F32)
        kre = kre_ref[rows, :]
        kim = kim_ref[rows, :]
        yre = ure * kre - uim * kim
        yim = ure * kim + uim * kre
        if kc == 0:
            r0 = lax.broadcasted_iota(jnp.int32, (ch, tc), 0) == 0
            yre = jnp.where(r0, 0.5 * ure * kre, yre)
            yim = jnp.where(r0, 0.0, yim)
        yre_scr[rows, :] = yre.astype(BF16)
        yim_scr[rows, :] = yim.astype(BF16)

    inv_n = 1.0 / (2 * lx)
    for nc in range(lx // ch):
        rows = slice(nc * ch, (nc + 1) * ch)
        y = (jnp.dot(c_ref[rows, :], yre_scr[...], preferred_element_type=F32)
             - jnp.dot(s_ref[rows, :], yim_scr[...], preferred_element_type=F32))
        rr = lax.broadcasted_iota(jnp.int32, (ch, tc), 0)
        sg = jnp.where((rr & 1) == 0, inv_n, -inv_n)
        y = y * (2.0 * inv_n) + sg * y_nyq
        o_ref[rows, :] = x0_scr[rows, :] * (y + wv_scr[rows, :] * bias_ref[...])


def _hyena(lx, row_block0, p, short_w, short_b, kre, kim, bias, ctab, stab):
    tc = HY_COLS
    nj = HY_CH // tc
    seg = lambda s: pl.BlockSpec((lx, tc), lambda b, j: (row_block0 + b, s * nj + j))
    wseg = lambda s: pl.BlockSpec((SHORT_CONV, tc), lambda b, j: (0, s * nj + j))
    bseg = lambda s: pl.BlockSpec((1, tc), lambda b, j: (0, s * nj + j))
    col = lambda rows: pl.BlockSpec((rows, tc), lambda b, j: (0, j))
    tab = pl.BlockSpec((lx, lx), lambda b, j: (0, 0))
    return pl.pallas_call(
        functools.partial(_hyena_body, lx=lx),
        grid=(BATCH, nj),
        in_specs=[seg(0), seg(1), seg(2), wseg(0), wseg(1), wseg(2), bseg(0), bseg(1), bseg(2),
                  col(lx), col(lx), col(1), tab, tab],
        out_specs=pl.BlockSpec((lx, tc), lambda b, j: (b, j)),
        out_shape=jax.ShapeDtypeStruct((BATCH * lx, HY_CH), F32),
        scratch_shapes=[pltpu.VMEM((lx, tc), F32), pltpu.VMEM((lx, tc), F32), pltpu.VMEM((lx, tc), BF16),
                        pltpu.VMEM((lx, tc), BF16), pltpu.VMEM((lx, tc), BF16)],
        compiler_params=_cparams(("parallel", "parallel")),
        name=f"hyena_{lx}",
    )(p, p, p, short_w, short_w, short_w, short_b, short_b, short_b, kre, kim, bias, ctab, stab)


def _nabias_body(rpb_ref, o_ref, bm_scr):
    rb = pl.program_id(0)
    w = GRID_W

    @pl.when(rb == 0)
    def _():
        qc = lax.broadcasted_iota(jnp.int32, (w, w), 0)
        kc = lax.broadcasted_iota(jnp.int32, (w, w), 1)
        d = jnp.clip(kc - qc + (NA_WIN_COLS - 1), 0, 2 * NA_WIN_COLS - 2)
        c0 = jnp.clip(qc - NA_WIN_COLS // 2, 0, w - NA_WIN_COLS)
        ok = (kc >= c0) & (kc < c0 + NA_WIN_COLS)
        for h in range(NA_HEADS):
            for dr in range(2 * NA_WIN_ROWS - 1):
                acc = jnp.zeros((w, w), F32)
                for j in range(2 * NA_WIN_COLS - 1):
                    acc = jnp.where(d == j, rpb_ref[h, dr, j], acc)
                bm_scr[h, dr] = jnp.where(ok, acc, NEG_INF)

    start = jnp.clip(NA_QROWS * rb - NA_WIN_ROWS // 2, 0, GRID_ROWS - NA_KROWS)
    for h in range(NA_HEADS):
        for i in range(NA_QROWS):
            r = NA_QROWS * rb + i
            r0 = jnp.clip(r - NA_WIN_ROWS // 2, 0, GRID_ROWS - NA_WIN_ROWS)
            for j in range(NA_KROWS):
                kr = start + j
                valid = (kr >= r0) & (kr < r0 + NA_WIN_ROWS)
                didx = jnp.clip(kr - r + (NA_WIN_ROWS - 1), 0, 2 * NA_WIN_ROWS - 2)
                blk = jnp.where(valid, bm_scr[h, didx], NEG_INF)
                o_ref[0, h, i * w:(i + 1) * w, j * w:(j + 1) * w] = blk


def _na_bias(rpb):
    nrb = GRID_ROWS // NA_QROWS
    return pl.pallas_call(
        _nabias_body,
        grid=(nrb,),
        in_specs=[pl.BlockSpec(memory_space=pltpu.SMEM)],
        out_specs=pl.BlockSpec((1, NA_HEADS, NA_QROWS * GRID_W, NA_KROWS * GRID_W), lambda rb: (rb, 0, 0, 0)),
        out_shape=jax.ShapeDtypeStruct((nrb, NA_HEADS, NA_QROWS * GRID_W, NA_KROWS * GRID_W), F32),
        scratch_shapes=[pltpu.VMEM((NA_HEADS, 2 * NA_WIN_ROWS - 1, GRID_W, GRID_W), F32)],
        compiler_params=_cparams(("arbitrary",)),
        name="na_bias",
    )(rpb)


_NT = (((1,), (1,)), ((), ()))


def _na_body(q_ref, k_ref, v_ref, kc_ref, vc_ref, bias_ref, o_ref):
    rb = pl.program_id(1)
    start = jnp.clip(NA_QROWS * rb - NA_WIN_ROWS // 2, 0, GRID_ROWS - NA_KROWS)
    row0 = pl.multiple_of(start * GRID_W, GRID_W)
    nk = NA_KROWS * GRID_W
    scale = HEAD_DIM ** -0.5
    for h in range(NA_HEADS):
        cs = slice(h * HEAD_DIM, (h + 1) * HEAD_DIM)
        q = (q_ref[:, cs] * scale).astype(BF16)
        kw = k_ref[pl.ds(row0, nk), cs].astype(BF16)
        vw = v_ref[pl.ds(row0, nk), cs].astype(BF16)
        s = lax.dot_general(q, kw, _NT, preferred_element_type=F32)
        bm = bias_ref[0, h]
        s = jnp.where(bm > 0.5 * NEG_INF, s + bm, NEG_INF)
        sc = lax.dot_general(q, kc_ref[:, cs].astype(BF16), _NT, preferred_element_type=F32)
        m = jnp.maximum(jnp.max(s, axis=-1, keepdims=True), jnp.max(sc, axis=-1, keepdims=True))
        p = jnp.exp(s - m)
        pc = jnp.exp(sc - m)
        denom = jnp.sum(p, axis=-1, keepdims=True) + jnp.sum(pc, axis=-1, keepdims=True)
        o = (jnp.dot(p.astype(BF16), vw, preferred_element_type=F32)
             + jnp.dot(pc.astype(BF16), vc_ref[:, cs].astype(BF16), preferred_element_type=F32))
        o_ref[:, cs] = o / denom


def _na_latent(p, bias):
    nrb = GRID_ROWS // NA_QROWS
    tq = NA_QROWS * GRID_W
    cq, ck, cv = COL_NQ // NA_DIM, COL_NK // NA_DIM, COL_NV // NA_DIM
    cblk = T_LAT // CTX_LEN
    return pl.pallas_call(
        _na_body,
        grid=(BATCH, nrb),
        in_specs=[pl.BlockSpec((tq, NA_DIM), lambda b, r: (b * nrb + r, cq)),
                  pl.BlockSpec((SEQ, NA_DIM), lambda b, r: (b, ck)),
                  pl.BlockSpec((SEQ, NA_DIM), lambda b, r: (b, cv)),
                  pl.BlockSpec((CTX_LEN, NA_DIM), lambda b, r: (cblk + b, ck)),
                  pl.BlockSpec((CTX_LEN, NA_DIM), lambda b, r: (cblk + b, cv)),
                  pl.BlockSpec((1, NA_HEADS, tq, NA_KROWS * GRID_W), lambda b, r: (r, 0, 0, 0))],
        out_specs=pl.BlockSpec((tq, NA_DIM), lambda b, r: (b * nrb + r, 0)),
        out_shape=jax.ShapeDtypeStruct((T_LAT, NA_DIM), F32),
        compiler_params=_cparams(("parallel", "arbitrary")),
        name="na_latent",
    )(p, p, p, p, p, bias)


def _nactx_body(q_ref, k_ref, v_ref, o_ref):
    scale = HEAD_DIM ** -0.5
    for h in range(NA_HEADS):
        cs = slice(h * HEAD_DIM, (h + 1) * HEAD_DIM)
        q = (q_ref[:, cs] * scale).astype(BF16)
        s = lax.dot_general(q, k_ref[:, cs].astype(BF16), _NT, preferred_element_type=F32)
        m = jnp.max(s, axis=-1, keepdims=True)
        p = jnp.exp(s - m)
        denom = jnp.sum(p, axis=-1, keepdims=True)
        o = jnp.dot(p.astype(BF16), v_ref[:, cs].astype(BF16), preferred_element_type=F32)
        o_ref[:, cs] = o / denom


def _na_context(p):
    cq, ck, cv = COL_NQ // NA_DIM, COL_NK // NA_DIM, COL_NV // NA_DIM
    cblk = T_LAT // CTX_LEN
    spec = lambda c: pl.BlockSpec((CTX_LEN, NA_DIM), lambda b: (cblk + b, c))
    return pl.pallas_call(
        _nactx_body,
        grid=(BATCH,),
        in_specs=[spec(cq), spec(ck), spec(cv)],
        out_specs=pl.BlockSpec((CTX_LEN, NA_DIM), lambda b: (b, 0)),
        out_shape=jax.ShapeDtypeStruct((T_CTX, NA_DIM), F32),
        compiler_params=_cparams(("parallel",)),
        name="na_context",
    )(p, p, p)


_BNT = (((2,), (2,)), ((0,), (0,)))
_BNN = (((2,), (1,)), ((0,), (0,)))


def _rope_tables(lx):
    t = jnp.arange(lx)
    rows = (t // GRID_W).astype(F32)
    cols = (t % GRID_W).astype(F32)
    quarter = GLA_DK // 4
    inv = ROPE_BASE ** (-jnp.arange(quarter, dtype=F32) / quarter)
    ang_r = rows[:, None] * inv[None, :]
    ang_c = cols[:, None] * inv[None, :]
    cos = jnp.concatenate([jnp.cos(ang_r)] * 2 + [jnp.cos(ang_c)] * 2, axis=-1)
    sin = jnp.concatenate([-jnp.sin(ang_r), jnp.sin(ang_r), -jnp.sin(ang_c), jnp.sin(ang_c)], axis=-1)
    return jnp.tile(cos, (1, 2)), jnp.tile(sin, (1, 2))


def _log_sigmoid(z):
    return jnp.minimum(z, 0.0) - jnp.log(1.0 + jnp.exp(-jnp.abs(z)))


def _gla_body(*refs, lx, rope, with_out, with_init):
    it = iter(refs)
    q_ref = next(it) if with_out else None
    k_ref, v_ref = next(it), next(it)
    g_ref = next(it) if with_out else None
    ga_ref, w2_ref, b2_ref = next(it), next(it), next(it)
    cos_ref = sin_ref = None
    if rope:
        cos_ref, sin_ref = next(it), next(it)
    nw_ref = next(it) if with_out else None
    sf0_ref = sb0_ref = None
    if with_init:
        sf0_ref, sb0_ref = next(it), next(it)
    y_ref = next(it) if with_out else None
    sf_ref, sb_ref = next(it), next(it)
    ut_scr, stp_scr, dl_scr = next(it), next(it), next(it)

    c = GLA_CHUNK
    n = lx // c
    lane = lax.broadcasted_iota(jnp.int32, (lx, 2 * GLA_DK), 1)

    def rot(x):
        if not rope:
            return x
        even = ((lane // (GLA_DK // 4)) % 2) == 0
        partner = jnp.where(even, pltpu.roll(x, 2 * GLA_DK - GLA_DK // 4, 1), pltpu.roll(x, GLA_DK // 4, 1))
        return x * cos_ref[...] + partner * sin_ref[...]

    k = rot(k_ref[...])
    q = rot(q_ref[...] * (GLA_DK ** -0.5)) if with_out else None
    v3 = v_ref[...].reshape(n, c, 2 * GLA_DV)

    ga = ga_ref[...]
    log_f = _log_sigmoid(jnp.dot(ga, w2_ref[0], preferred_element_type=F32, precision=HIGHEST)
                         + b2_ref[0]) / GLA_GATE_NORM
    log_b = _log_sigmoid(jnp.dot(ga, w2_ref[1], preferred_element_type=F32, precision=HIGHEST)
                         + b2_ref[1]) / GLA_GATE_NORM
    pos = lax.broadcasted_iota(jnp.int32, (lx, 2 * GLA_DK), 0) % c
    cum_f, cum_b = log_f, log_b
    step = 1
    while step < c:
        cum_f = cum_f + jnp.where(pos >= step, pltpu.roll(cum_f, step, 0), 0.0)
        cum_b = cum_b + jnp.where(pos < c - step, pltpu.roll(cum_b, lx - step, 0), 0.0)
        step *= 2

    ti = lax.broadcasted_iota(jnp.int32, (n, c, c), 1)
    si = lax.broadcasted_iota(jnp.int32, (n, c, c), 2)
    o_acc = [None, None]
    for forward in (True, False):
        cum = cum_f if forward else cum_b
        cum3 = cum.reshape(n, c, 2 * GLA_DK)
        tot = cum3[:, c - 1:c, :] if forward else cum3[:, 0:1, :]
        kinv = (k * jnp.exp(-cum)).reshape(n, c, 2 * GLA_DK)
        kd = (kinv * jnp.exp(tot)).astype(BF16)
        kinv = kinv.astype(BF16)
        dl_scr[...] = jnp.exp(tot)
        qd3 = (q * jnp.exp(cum)).reshape(n, c, 2 * GLA_DK) if with_out else None
        keep = (ti >= si) if forward else (si >= ti)
        for hh in range(2):
            vh = v3[:, :, hh * GLA_DV:(hh + 1) * GLA_DV].astype(BF16)
            vt = jnp.swapaxes(vh, 1, 2)
            ut_scr[...] = lax.dot_general(vt, kd, _BNN, preferred_element_type=F32)
            st0 = (sf0_ref if forward else sb0_ref)[0, hh] if with_init else jnp.zeros((GLA_DV, 2 * GLA_DK), F32)

            def scan(i, st):
                ci = i if forward else n - 1 - i
                stp_scr[ci] = st
                return st * dl_scr[ci] + ut_scr[ci]

            st_fin = lax.fori_loop(0, n, scan, st0)
            (sf_ref if forward else sb_ref)[0, hh] = st_fin
            if with_out:
                in_head = (lax.broadcasted_iota(jnp.int32, (n, c, 2 * GLA_DK), 2) // GLA_DK) == hh
                qm = jnp.where(in_head, qd3, 0.0).astype(BF16)
                att = lax.dot_general(qm, kinv, _BNT, preferred_element_type=F32)
                att = jnp.where(keep, att, 0.0).astype(BF16)
                o = (lax.dot_general(att, vh, _BNN, preferred_element_type=F32)
                     + lax.dot_general(qm, stp_scr[...].astype(BF16), _BNT, preferred_element_type=F32))
                o_acc[hh] = o if o_acc[hh] is None else o_acc[hh] + o

    if with_out:
        for hh in range(2):
            vs = slice(hh * GLA_DV, (hh + 1) * GLA_DV)
            o = _rms_rows(o_acc[hh].reshape(lx, GLA_DV)) * nw_ref[...]
            y_ref[:, vs] = o * _silu(g_ref[:, vs])


def _gla(lx, row_block0, p, p_ga, w2cat, b2, norm_w, rope_tabs, init, with_out):
    npair = GLA_HEADS // 2
    rope = rope_tabs is not None
    with_init = init is not None
    rb = lambda b: row_block0 + b
    qk = lambda col: pl.BlockSpec((lx, 2 * GLA_DK), lambda b, h: (rb(b), col // (2 * GLA_DK) + h))
    vg = lambda col: pl.BlockSpec((lx, 2 * GLA_DV), lambda b, h: (rb(b), col // (2 * GLA_DV) + h))
    st_spec = pl.BlockSpec((1, 2, GLA_DV, 2 * GLA_DK), lambda b, h: (b, h, 0, 0))
    in_specs, args = [], []
    if with_out:
        in_specs.append(qk(COL_GQ)); args.append(p)
    in_specs += [qk(COL_GK), vg(COL_GV)]; args += [p, p]
    if with_out:
        in_specs.append(vg(COL_GG)); args.append(p)
    in_specs += [pl.BlockSpec((lx, 2 * GLA_RANK), lambda b, h: (rb(b), 0)),
                 pl.BlockSpec((2, 2 * GLA_RANK, 2 * GLA_DK), lambda b, h: (0, 0, h)),
                 pl.BlockSpec((2, 1, 2 * GLA_DK), lambda b, h: (0, 0, h))]
    args += [p_ga, w2cat, b2]
    if rope:
        in_specs += [pl.BlockSpec((lx, 2 * GLA_DK), lambda b, h: (0, 0))] * 2
        args += list(rope_tabs)
    if with_out:
        in_specs.append(pl.BlockSpec((1, GLA_DV), lambda b, h: (0, 0))); args.append(norm_w)
    if with_init:
        in_specs += [st_spec, st_spec]; args += list(init)
    st_shape = jax.ShapeDtypeStruct((BATCH, GLA_HEADS, GLA_DV, 2 * GLA_DK), F32)
    out_specs, out_shape = [], []
    if with_out:
        out_specs.append(pl.BlockSpec((lx, 2 * GLA_DV), lambda b, h: (b, h)))
        out_shape.append(jax.ShapeDtypeStruct((BATCH * lx, GLA_V), F32))
    out_specs += [st_spec, st_spec]
    out_shape += [st_shape, st_shape]
    n = lx // GLA_CHUNK
    outs = pl.pallas_call(
        functools.partial(_gla_body, lx=lx, rope=rope, with_out=with_out, with_init=with_init),
        grid=(BATCH, npair),
        in_specs=in_specs, out_specs=out_specs, out_shape=out_shape,
        scratch_shapes=[pltpu.VMEM((n, GLA_DV, 2 * GLA_DK), F32), pltpu.VMEM((n, GLA_DV, 2 * GLA_DK), F32),
                        pltpu.VMEM((n, 1, 2 * GLA_DK), F32)],
        compiler_params=_cparams(("parallel", "parallel")),
        name=f"gla_{lx}",
    )(*args)
    if with_out:
        return outs[0], outs[1], outs[2]
    return None, outs[0], outs[1]


def _outproj_body(*refs, with_ctx):
    it = iter(refs)
    lat = [next(it) for _ in range(3)]
    ctx = [next(it) for _ in range(3)] if with_ctx else None
    x_ref, mod_ref, hyw_ref, naw_ref, w_ref, g_ref, b_ref, o_ref = (next(it) for _ in range(8))

    def pick(j):
        if not with_ctx:
            return lat[j][...]
        return jnp.where(pl.program_id(0) >= T_LAT // ROW_TILE, ctx[j][...], lat[j][...])

    ya = (_rms_rows(pick(0)) * hyw_ref[...]).astype(BF16)
    yb = (_rms_rows(pick(1)) * naw_ref[...]).astype(BF16)
    yc = pick(2).astype(BF16)
    y = (jnp.dot(ya, w_ref[0:HY_CH, :], preferred_element_type=F32)
         + jnp.dot(yb, w_ref[HY_CH:HY_CH + NA_DIM, :], preferred_element_type=F32)
         + jnp.dot(yc, w_ref[HY_CH + NA_DIM:, :], preferred_element_type=F32))
    r = ALPHA * x_ref[...] + mod_ref[0, 2:3, :] * y
    o_ref[...] = _layer_norm_rows(r) * g_ref[...] + b_ref[...]


def _outproj(rows, lat, ctx, xa, mod, hy_norm_w, na_norm_w, w_out_bf, ln_g, ln_b):
    tm = ROW_TILE
    with_ctx = ctx is not None
    nlat = T_LAT // tm
    mrow = _mod_row(tm)
    widths = (HY_CH, NA_DIM, GLA_V)
    in_specs = [pl.BlockSpec((tm, w), lambda i: (jnp.minimum(i, nlat - 1), 0)) for w in widths]
    args = list(lat)
    if with_ctx:
        in_specs += [pl.BlockSpec((tm, w), lambda i: (jnp.maximum(i - nlat, 0), 0)) for w in widths]
        args += list(ctx)
    vec = lambda w: pl.BlockSpec((1, w), lambda i: (0, 0))
    in_specs += [pl.BlockSpec((tm, D_MODEL), lambda i: (i, 0)),
                 pl.BlockSpec((1, 6, D_MODEL), lambda i: (mrow(i), 0, 0)),
                 vec(HY_CH), vec(NA_DIM),
                 pl.BlockSpec((D_MODEL, D_MODEL), lambda i: (0, 0)),
                 vec(D_MODEL), vec(D_MODEL)]
    args += [xa, mod, hy_norm_w, na_norm_w, w_out_bf, ln_g, ln_b]
    return pl.pallas_call(
        functools.partial(_outproj_body, with_ctx=with_ctx),
        grid=(rows // tm,),
        in_specs=in_specs,
        out_specs=pl.BlockSpec((tm, D_MODEL), lambda i: (i, 0)),
        out_shape=jax.ShapeDtypeStruct((rows, D_MODEL), F32),
        compiler_params=_cparams(("parallel",)),
        name=f"outproj_{rows}",
    )(*args)


ROUTER_ROWS = 8 + N_EXPERTS


def _router_body(x_ref, mod_ref, w_ref, b_ref, h_ref, eid_ref, gw_ref):
    tm = ROW_TILE
    h2 = _layer_norm_rows(x_ref[...]) * (1.0 + mod_ref[0, 4:5, :]) + mod_ref[0, 3:4, :]
    h_ref[...] = h2
    logits = lax.dot_general(w_ref[...], h2, _NT, preferred_element_type=F32, precision=HIGHEST) + b_ref[...]
    lg = logits[0:N_GROUPS, :]
    eg = jnp.exp(lg - jnp.max(lg, axis=0, keepdims=True))
    pg = eg / jnp.sum(eg, axis=0, keepdims=True)
    top_pg = jnp.max(pg, axis=0, keepdims=True)
    gio = lax.broadcasted_iota(jnp.int32, (N_GROUPS, tm), 0)
    gi = jnp.min(jnp.where(pg == top_pg, gio, N_GROUPS), axis=0, keepdims=True)
    le = logits[8:8 + N_EXPERTS, :].reshape(N_GROUPS, EXPERTS_PER_GROUP, tm)
    gsel = lax.broadcasted_iota(jnp.int32, (N_GROUPS, EXPERTS_PER_GROUP, tm), 0) == gi[None]
    les = jnp.sum(jnp.where(gsel, le, 0.0), axis=0)
    eio = lax.broadcasted_iota(jnp.int32, (EXPERTS_PER_GROUP, tm), 0)
    v1 = jnp.max(les, axis=0, keepdims=True)
    i1 = jnp.min(jnp.where(les == v1, eio, EXPERTS_PER_GROUP), axis=0, keepdims=True)
    les2 = jnp.where(eio == i1, -jnp.inf, les)
    v2 = jnp.max(les2, axis=0, keepdims=True)
    i2 = jnp.min(jnp.where(les2 == v2, eio, EXPERTS_PER_GROUP), axis=0, keepdims=True)
    t = jnp.exp(v2 - v1)
    w1 = top_pg / (1.0 + t)
    eid_ref[...] = jnp.concatenate([gi * EXPERTS_PER_GROUP + i1, gi * EXPERTS_PER_GROUP + i2], axis=0)
    gw_ref[...] = jnp.concatenate([w1, w1 * t], axis=0)


def _router(rows, x1, mod, w_rt, b_rt):
    tm = ROW_TILE
    mrow = _mod_row(tm)
    return pl.pallas_call(
        _router_body,
        grid=(rows // tm,),
        in_specs=[pl.BlockSpec((tm, D_MODEL), lambda i: (i, 0)),
                  pl.BlockSpec((1, 6, D_MODEL), lambda i: (mrow(i), 0, 0)),
                  pl.BlockSpec((ROUTER_ROWS, D_MODEL), lambda i: (0, 0)),
                  pl.BlockSpec((ROUTER_ROWS, 1), lambda i: (0, 0))],
        out_specs=[pl.BlockSpec((tm, D_MODEL), lambda i: (i, 0)),
                   pl.BlockSpec((2, tm), lambda i: (0, i)),
                   pl.BlockSpec((2, tm), lambda i: (0, i))],
        out_shape=[jax.ShapeDtypeStruct((rows, D_MODEL), F32),
                   jax.ShapeDtypeStruct((2, rows), jnp.int32),
                   jax.ShapeDtypeStruct((2, rows), F32)],
        compiler_params=_cparams(("parallel",)),
        name=f"router_{rows}",
    )(x1, mod, w_rt, b_rt)


def _moe_tiles(rows):
    return -(-(2 * rows + N_EXPERTS * (MOE_TILE - 1)) // MOE_TILE)


def _dispatch_plan(rows, eid, gw):
    nt = _moe_tiles(rows)
    e = eid.reshape(-1)
    onehot = (e[:, None] == jnp.arange(N_EXPERTS, dtype=jnp.int32)[None, :]).astype(jnp.int32)
    csum = jnp.cumsum(onehot, axis=0)
    rank = jnp.sum(csum * onehot, axis=1) - 1
    counts = csum[-1]
    padded = ((counts + MOE_TILE - 1) // MOE_TILE) * MOE_TILE
    pend = jnp.cumsum(padded)
    pstart = pend - padded
    dest = pstart[e] + rank
    tok = jnp.arange(2 * rows, dtype=jnp.int32) % rows
    row_token = jnp.zeros((nt * MOE_TILE,), jnp.int32).at[dest].set(tok)
    row_gate = jnp.zeros((nt * MOE_TILE,), F32).at[dest].set(gw.reshape(-1))
    tile_start = jnp.arange(nt, dtype=jnp.int32) * MOE_TILE
    tile_valid = (tile_start < pend[-1]).astype(jnp.int32)
    tile_expert = jnp.searchsorted(pend, tile_start, side="right").astype(jnp.int32)
    last_expert = jnp.max(jnp.where(counts > 0, jnp.arange(N_EXPERTS, dtype=jnp.int32), 0))
    tile_expert = jnp.minimum(tile_expert, last_expert)
    return tile_expert, tile_valid, row_token, row_gate.reshape(nt * MOE_TILE, 1), dest


def _row_copy(src, src_row, dst, dst_row, sem):
    return pltpu.make_async_copy(src.at[pl.ds(src_row, 1), :], dst.at[pl.ds(dst_row, 1), :], sem)


def _moe_body(te_ref, tv_ref, tok_ref, h_hbm, gate_ref, wup_ref, wdn_ref, o_ref, x_scr, sem):
    i = pl.program_id(0)
    tm = MOE_TILE

    @pl.when(tv_ref[i] == 1)
    def _():
        def start(r, carry):
            _row_copy(h_hbm, tok_ref[i * tm + r], x_scr, r, sem).start()
            return carry

        lax.fori_loop(0, tm, start, 0)

        def wait(r, carry):
            _row_copy(h_hbm, 0, x_scr, r, sem).wait()
            return carry

        lax.fori_loop(0, tm, wait, 0)
        xb = x_scr[...].astype(BF16)
        gu = jnp.dot(xb, wup_ref[...].astype(BF16), preferred_element_type=F32)
        act = _silu(gu[:, :D_EXPERT]) * gu[:, D_EXPERT:] * gate_ref[...]
        o_ref[...] = jnp.dot(act.astype(BF16), wdn_ref[...].astype(BF16), preferred_element_type=F32)

    @pl.when(tv_ref[i] == 0)
    def _():
        o_ref[...] = jnp.zeros_like(o_ref)


def _moe_ffn(rows, plan, h2, w_up, w_down):
    tile_expert, tile_valid, row_token, row_gate, _ = plan
    nt = _moe_tiles(rows)
    tm = MOE_TILE
    eg = lambda i, te, tv, tok: (te[i] // EXPERTS_PER_GROUP, te[i] % EXPERTS_PER_GROUP, 0, 0)
    grid_spec = pltpu.PrefetchScalarGridSpec(
        num_scalar_prefetch=3,
        grid=(nt,),
        in_specs=[pl.BlockSpec(memory_space=pl.ANY),
                  pl.BlockSpec((tm, 1), lambda i, te, tv, tok: (i, 0)),
                  pl.BlockSpec((None, None, D_MODEL, 2 * D_EXPERT), eg),
                  pl.BlockSpec((None, None, D_EXPERT, D_MODEL), eg)],
        out_specs=pl.BlockSpec((tm, D_MODEL), lambda i, te, tv, tok: (i, 0)),
        scratch_shapes=[pltpu.VMEM((tm, D_MODEL), F32), pltpu.SemaphoreType.DMA],
    )
    return pl.pallas_call(
        _moe_body,
        grid_spec=grid_spec,
        out_shape=jax.ShapeDtypeStruct((nt * tm, D_MODEL), F32),
        compiler_params=_cparams(("arbitrary",)),
        name=f"moe_ffn_{rows}",
    )(tile_expert, tile_valid, row_token, h2, row_gate, w_up, w_down)


def _combine_body(pos_ref, y_hbm, x_ref, mod_ref, g_ref, b_ref, o_ref, y_scr, sem, *, rows):
    i = pl.program_id(0)
    tm = ROW_TILE

    def start(r, carry):
        for s in range(2):
            _row_copy(y_hbm, pos_ref[s * rows + i * tm + r], y_scr.at[s], r, sem).start()
        return carry

    lax.fori_loop(0, tm, start, 0)

    def wait(r, carry):
        for s in range(2):
            _row_copy(y_hbm, 0, y_scr.at[s], r, sem).wait()
        return carry

    lax.fori_loop(0, tm, wait, 0)
    f = y_scr[0] + y_scr[1]
    r = ALPHA * x_ref[...] + mod_ref[0, 5:6, :] * f
    o_ref[...] = _layer_norm_rows(r) * g_ref[...] + b_ref[...]


def _combine(rows, pos, ysort, x1, mod, ln_g, ln_b):
    tm = ROW_TILE
    mrow = _mod_row(tm)
    vec = pl.BlockSpec((1, D_MODEL), lambda i, pos: (0, 0))
    grid_spec = pltpu.PrefetchScalarGridSpec(
        num_scalar_prefetch=1,
        grid=(rows // tm,),
        in_specs=[pl.BlockSpec(memory_space=pl.ANY),
                  pl.BlockSpec((tm, D_MODEL), lambda i, pos: (i, 0)),
                  pl.BlockSpec((1, 6, D_MODEL), lambda i, pos: (mrow(i), 0, 0)),
                  vec, vec],
        out_specs=pl.BlockSpec((tm, D_MODEL), lambda i, pos: (i, 0)),
        scratch_shapes=[pltpu.VMEM((2, tm, D_MODEL), F32), pltpu.SemaphoreType.DMA],
    )
    return pl.pallas_call(
        functools.partial(_combine_body, rows=rows),
        grid_spec=grid_spec,
        out_shape=jax.ShapeDtypeStruct((rows, D_MODEL), F32),
        compiler_params=_cparams(("arbitrary",)),
        name=f"combine_{rows}",
    )(pos, ysort, x1, mod, ln_g, ln_b)


def _layer(xa, c8, tabs, w, last):
    row = lambda v: v.reshape(1, -1)
    mod = _ada(c8, w["w_ada"], w["b_ada"]).reshape(8, 6, D_MODEL)
    p, p_ga = _inproj(xa, mod, w["w_in"][:, :P_COLS].astype(BF16), w["w_in"][:, P_COLS:].astype(BF16))

    w1p = jnp.pad(w["hy_f_w1"], ((0, 128 - w["hy_f_w1"].shape[0]), (0, 0)))
    filt_args = (w1p, row(w["hy_f_b1"]), w["hy_f_w2"], row(w["hy_f_b2"]), w["hy_f_w3"], row(w["hy_sin_freq"]))
    short_b, hy_bias = row(w["hy_short_b"]), row(w["hy_bias"])
    kre, kim = _hyena_filter(SEQ, tabs["z_lat"], tabs["win_lat"], tabs["c_lat"], tabs["s_lat"], *filt_args)
    ya = _hyena(SEQ, 0, p, w["hy_short_w"], short_b, kre, kim, hy_bias, tabs["c_lat"], tabs["s_lat"])

    yb = _na_latent(p, _na_bias(w["na_rpb"]))

    zpad = jnp.zeros((GLA_RANK, GLA_K), F32)
    w2cat = jnp.stack([jnp.concatenate([w["gla_a_w2"][0], zpad], axis=0),
                       jnp.concatenate([zpad, w["gla_a_w2"][1]], axis=0)])
    b2 = w["gla_a_b"].reshape(2, 1, GLA_K)
    nw = row(w["gla_norm_w"])
    ctx_blk = T_LAT // CTX_LEN
    yc_ctx, sf, sb = _gla(CTX_LEN, ctx_blk, p, p_ga, w2cat, b2, nw, None, None, with_out=not last)
    yc, _, _ = _gla(SEQ, 0, p, p_ga, w2cat, b2, nw, tabs["rope"], (sf, sb), with_out=True)

    lat = (ya, yb, yc)
    ctx = None
    if not last:
        kre_c, kim_c = _hyena_filter(CTX_LEN, tabs["z_ctx"], tabs["win_ctx"], tabs["c_ctx"], tabs["s_ctx"], *filt_args)
        ya_c = _hyena(CTX_LEN, ctx_blk, p, w["hy_short_w"], short_b, kre_c, kim_c, hy_bias,
                      tabs["c_ctx"], tabs["s_ctx"])
        ctx = (ya_c, _na_context(p), yc_ctx)

    rows = T_LAT if last else T_ALL
    x1 = _outproj(rows, lat, ctx, xa, mod, row(w["hy_norm_w"]), row(w["na_norm_w"]),
                  w["w_out"].astype(BF16), row(w["ln1_g"]), row(w["ln1_b"]))

    pad4 = jnp.zeros((8 - N_GROUPS, D_MODEL), F32)
    w_rt = jnp.concatenate([w["w_rg"].T, pad4, w["w_re"].T], axis=0)
    b_rt = jnp.concatenate([w["b_rg"], jnp.zeros((8 - N_GROUPS,), F32), w["b_re"]]).reshape(ROUTER_ROWS, 1)
    h2, eid, gw = _router(rows, x1, mod, w_rt, b_rt)
    plan = _dispatch_plan(rows, eid, gw)
    ysort = _moe_ffn(rows, plan, h2, w["w_up"], w["w_down"])
    return _combine(rows, plan[4], ysort, x1, mod, row(w["ln2_g"]), row(w["ln2_b"]))


_LAYER_WEIGHTS = ("w_ada", "b_ada", "w_in", "hy_short_w", "hy_short_b", "hy_f_w1", "hy_f_b1", "hy_f_w2", "hy_f_b2",
                  "hy_f_w3", "hy_sin_freq", "hy_bias", "hy_norm_w", "na_rpb", "na_norm_w", "gla_a_w2", "gla_a_b",
                  "gla_norm_w", "w_out", "ln1_g", "ln1_b", "w_rg", "b_rg", "w_re", "b_re", "w_up", "w_down",
                  "ln2_g", "ln2_b")


def kernel(x, c, ctx, c_ctx, w_ada, b_ada, w_in, hy_short_w, hy_short_b, hy_f_w1, hy_f_b1, hy_f_w2, hy_f_b2, hy_f_w3, hy_sin_freq, hy_bias, hy_norm_w, na_rpb, na_norm_w, gla_a_w2, gla_a_b, gla_norm_w, w_out, ln1_g, ln1_b, w_rg, b_rg, w_re, b_re, w_up, w_down, ln2_g, ln2_b):
    stacked = dict(zip(_LAYER_WEIGHTS, (w_ada, b_ada, w_in, hy_short_w, hy_short_b, hy_f_w1, hy_f_b1, hy_f_w2, hy_f_b2,
                                        hy_f_w3, hy_sin_freq, hy_bias, hy_norm_w, na_rpb, na_norm_w, gla_a_w2,
                                        gla_a_b, gla_norm_w, w_out, ln1_g, ln1_b, w_rg, b_rg, w_re, b_re, w_up,
                                        w_down, ln2_g, ln2_b)))
    tabs = {}
    tabs["c_lat"], tabs["s_lat"] = _dft_tables(SEQ)
    tabs["c_ctx"], tabs["s_ctx"] = _dft_tables(CTX_LEN)
    tabs["z_lat"], tabs["win_lat"] = _hyena_features(SEQ)
    tabs["z_ctx"], tabs["win_ctx"] = _hyena_features(CTX_LEN)
    tabs["rope"] = _rope_tables(SEQ)

    xa = jnp.concatenate([x.reshape(T_LAT, D_MODEL), ctx.reshape(T_CTX, D_MODEL)], axis=0)
    c8 = jnp.concatenate([c, c_ctx[None, :], jnp.zeros((8 - BATCH - 1, D_MODEL), F32)], axis=0)
    for i in range(DEPTH):
        xa = _layer(xa, c8, tabs, {k: v[i] for k, v in stacked.items()}, last=(i == DEPTH - 1))
    return xa.reshape(BATCH, SEQ, D_MODEL)
```

```python
import functools
import math

import jax
import jax.numpy as jnp
import numpy as np
from jax import lax
from jax.experimental import pallas as pl
from jax.experimental.pallas import tpu as pltpu

F32 = jnp.float32
BF16 = jnp.bfloat16
HIGHEST = lax.Precision.HIGHEST

D_MODEL = 2048
BATCH = 4
SEQ = 2048
DEPTH = 2
CTX_LEN = 256
GRID_W = 64
HEAD_DIM = 128
HY_CH = 512
NA_HEADS = 6
NA_DIM = 768
GLA_HEADS = 6
GLA_DK = 64
GLA_DV = 128
GLA_K = 384
GLA_V = 768
GLA_RANK = 16
GLA_GATE_NORM = 16.0
GLA_CHUNK = 64
ROPE_BASE = 10000.0
SHORT_CONV = 3
HY_EMB_BANDS = 16
HY_FILTER_WIDTH = 64
HY_MIN_DECAY = math.log(1e-2) / 1.5
HY_MAX_DECAY = math.log(1e-2) / 0.3
NA_WIN_ROWS = 8
NA_WIN_COLS = 16
NEG_INF = -1e30
N_GROUPS = 4
EXPERTS_PER_GROUP = 8
N_EXPERTS = N_GROUPS * EXPERTS_PER_GROUP
D_EXPERT = 512
ALPHA = (2 * DEPTH) ** 0.25
LN_EPS = 1e-6

T_LAT = BATCH * SEQ
T_CTX = BATCH * CTX_LEN
T_ALL = T_LAT + T_CTX
GRID_ROWS = SEQ // GRID_W

COL_HY = 0
COL_NQ = 3 * HY_CH
COL_NK = COL_NQ + NA_DIM
COL_NV = COL_NK + NA_DIM
COL_GQ = COL_NV + NA_DIM
COL_GK = COL_GQ + GLA_K
COL_GV = COL_GK + GLA_K
COL_GG = COL_GV + GLA_V
COL_GA = COL_GG + GLA_V
P_COLS = COL_GA

VMEM_LIMIT_BYTES = 56 * 1024 * 1024
ROW_TILE = 256
INPROJ_ROWS = 1024
INPROJ_COLS = 512
INPROJ_CHUNK = 128
HY_COLS = 256
DFT_CHUNK = 512
NA_QROWS = 4
NA_KROWS = NA_QROWS + NA_WIN_ROWS - 1
MOE_TILE = 256


def _cparams(sem, vmem=VMEM_LIMIT_BYTES):
    return pltpu.CompilerParams(dimension_semantics=sem, vmem_limit_bytes=vmem)


def _layer_norm_rows(x):
    mu = jnp.mean(x, axis=-1, keepdims=True)
    xc = x - mu
    return xc * lax.rsqrt(jnp.mean(xc * xc, axis=-1, keepdims=True) + LN_EPS)


def _rms_rows(x):
    return x * lax.rsqrt(jnp.mean(x * x, axis=-1, keepdims=True) + LN_EPS)


def _silu(x):
    return x / (1.0 + jnp.exp(-x))


def _mod_row(tile_rows):
    per_batch = SEQ // tile_rows
    return lambda i: jnp.minimum(i // per_batch, BATCH)


def _ada_body(c_ref, w_ref, b_ref, o_ref):
    s = _silu(c_ref[...])
    o_ref[...] = jnp.dot(s, w_ref[...], preferred_element_type=F32, precision=HIGHEST) + b_ref[...]


def _ada(c8, layer, w_ada, b_ada):
    tn = 1024
    n = 6 * D_MODEL
    return pl.pallas_call(
        _ada_body,
        grid=(n // tn,),
        in_specs=[pl.BlockSpec((8, D_MODEL), lambda j: (0, 0)),
                  pl.BlockSpec((None, D_MODEL, tn), lambda j: (layer, 0, j)),
                  pl.BlockSpec((1, tn), lambda j: (0, j))],
        out_specs=pl.BlockSpec((8, tn), lambda j: (0, j)),
        out_shape=jax.ShapeDtypeStruct((8, n), F32),
        compiler_params=_cparams(("parallel",)),
        name="ada_mod",
    )(c8, w_ada, b_ada.reshape(1, n))


def _inproj_body(x_ref, mod_ref, w_ref, wga_ref, p_ref, ga_ref, h_scr):
    @pl.when(pl.program_id(1) == 0)
    def _():
        sh = mod_ref[0, 0:1, :]
        sc = mod_ref[0, 1:2, :]

        def chunk(r, carry):
            rows = pl.ds(pl.multiple_of(r * INPROJ_CHUNK, INPROJ_CHUNK), INPROJ_CHUNK)
            h = _layer_norm_rows(x_ref[rows, :]) * (1.0 + sc) + sh
            h_scr[rows, :] = h.astype(BF16)
            return carry

        lax.fori_loop(0, INPROJ_ROWS // INPROJ_CHUNK, chunk, 0)
        ga_ref[...] = jnp.dot(h_scr[...], wga_ref[...], preferred_element_type=F32)

    p_ref[...] = jnp.dot(h_scr[...], w_ref[...], preferred_element_type=F32).astype(p_ref.dtype)


def _inproj(xa, mod, w_in_bf, w_ga_bf):
    tm, tn = INPROJ_ROWS, INPROJ_COLS
    mrow = _mod_row(tm)
    return pl.pallas_call(
        _inproj_body,
        grid=(T_ALL // tm, P_COLS // tn),
        in_specs=[pl.BlockSpec((tm, D_MODEL), lambda i, j: (i, 0)),
                  pl.BlockSpec((1, 6, D_MODEL), lambda i, j: (mrow(i), 0, 0)),
                  pl.BlockSpec((D_MODEL, tn), lambda i, j: (0, j)),
                  pl.BlockSpec((D_MODEL, 2 * GLA_RANK), lambda i, j: (0, 0))],
        out_specs=[pl.BlockSpec((tm, tn), lambda i, j: (i, j)),
                   pl.BlockSpec((tm, 2 * GLA_RANK), lambda i, j: (i, 0))],
        out_shape=[jax.ShapeDtypeStruct((T_ALL, P_COLS), BF16),
                   jax.ShapeDtypeStruct((T_ALL, 2 * GLA_RANK), F32)],
        scratch_shapes=[pltpu.VMEM((tm, D_MODEL), BF16)],
        compiler_params=_cparams(("parallel", "arbitrary")),
        name="inproj",
    )(xa, mod, w_in_bf, w_ga_bf)


def _dft_tables(lx):
    n = 2 * lx
    k = jnp.arange(lx, dtype=jnp.int32)
    idx = (k[:, None] * k[None, :]) % n
    ang = idx.astype(F32) * (2.0 * math.pi / n)
    return jnp.cos(ang).astype(BF16), jnp.sin(ang).astype(BF16)


def _hyena_features(lx):
    n = jnp.arange(lx, dtype=F32)
    t = n / max(lx - 1, 1)
    bands = jnp.linspace(1e-4, HY_EMB_BANDS - 1, HY_EMB_BANDS, dtype=F32)
    ang = (2.0 * math.pi / lx) * n[:, None] * bands[None, :]
    z = jnp.concatenate([t[:, None], jnp.cos(ang), -jnp.sin(ang)], axis=-1)
    z = jnp.pad(z, ((0, 0), (0, 128 - z.shape[1])))
    deltas = jnp.abs(jnp.linspace(HY_MIN_DECAY, HY_MAX_DECAY, HY_CH, dtype=F32))
    win = jnp.exp(-t[:, None] * deltas[None, :])
    return z, win


def _split_dot(tab, x):
    xh = x.astype(BF16)
    xl = (x - xh.astype(F32)).astype(BF16)
    return (jnp.dot(tab, xh, preferred_element_type=F32) + jnp.dot(tab, xl, preferred_element_type=F32))


def _hyfilt_body(z_ref, w1_ref, b1_ref, w2_ref, b2_ref, w3f_ref, w3b_ref, fr_ref, win_ref, c_ref, s_ref,
                 kre_ref, kim_ref):
    fr = fr_ref[...]
    hid = jnp.sin(fr * (jnp.dot(z_ref[...], w1_ref[...], preferred_element_type=F32, precision=HIGHEST) + b1_ref[...]))
    hid = jnp.sin(fr * (jnp.dot(hid, w2_ref[...], preferred_element_type=F32, precision=HIGHEST) + b2_ref[...]))
    win = win_ref[...]
    hf = jnp.dot(hid, w3f_ref[...], preferred_element_type=F32, precision=HIGHEST) * win
    hb = jnp.dot(hid, w3b_ref[...], preferred_element_type=F32, precision=HIGHEST) * win
    row = lax.broadcasted_iota(jnp.int32, hf.shape, 0)
    hb = jnp.where(row == 0, 0.0, hb)
    hs = hf + hb
    hd = hf - hb
    sgn = jnp.where((row & 1) == 0, 1.0, -1.0)
    k_nyq = jnp.sum(hs * sgn, axis=0, keepdims=True)
    kre_ref[...] = _split_dot(c_ref[...], hs)
    kim = -_split_dot(s_ref[...], hd)
    kim_ref[...] = jnp.where(row == 0, k_nyq, kim)


def _hyena_filter(lx, z, win, ctab, stab, w1p, b1, w2, b2, w3, fr):
    tc = HY_COLS
    nj = HY_CH // tc
    full = lambda shape: pl.BlockSpec(shape, lambda j: (0,) * len(shape))
    return pl.pallas_call(
        _hyfilt_body,
        grid=(nj,),
        in_specs=[full((lx, 128)), full((128, HY_FILTER_WIDTH)), full((1, HY_FILTER_WIDTH)),
                  full((HY_FILTER_WIDTH, HY_FILTER_WIDTH)), full((1, HY_FILTER_WIDTH)),
                  pl.BlockSpec((HY_FILTER_WIDTH, tc), lambda j: (0, j)),
                  pl.BlockSpec((HY_FILTER_WIDTH, tc), lambda j: (0, nj + j)),
                  full((1, HY_FILTER_WIDTH)),
                  pl.BlockSpec((lx, tc), lambda j: (0, j)),
                  full((lx, lx)), full((lx, lx))],
        out_specs=[pl.BlockSpec((lx, tc), lambda j: (0, j)), pl.BlockSpec((lx, tc), lambda j: (0, j))],
        out_shape=[jax.ShapeDtypeStruct((lx, HY_CH), F32), jax.ShapeDtypeStruct((lx, HY_CH), F32)],
        compiler_params=_cparams(("arbitrary",)),
        name=f"hyena_filter_{lx}",
    )(z, w1p, b1, w2, b2, w3, w3, fr, win, ctab, stab)


def _hyena_body(x0_ref, x1_ref, v_ref, w0_ref, w1_ref, w2_ref, b0_ref, b1_ref, b2_ref, kre_ref, kim_ref,
                bias_ref, c_ref, s_ref, o_ref, x0_scr, wv_scr, wb_scr, yre_scr, yim_scr, *, lx):
    tc = HY_COLS
    ch = min(DFT_CHUNK, lx)
    row = lax.broadcasted_iota(jnp.int32, (lx, tc), 0)
    first = row == 0
    last = row == lx - 1

    def short_conv(x_ref, w_ref, b_ref):
        x = x_ref[...].astype(F32)
        xm = jnp.where(first, 0.0, pltpu.roll(x, 1, 0))
        xp = jnp.where(last, 0.0, pltpu.roll(x, lx - 1, 0))
        return xm * w_ref[0:1, :] + x * w_ref[1:2, :] + xp * w_ref[2:3, :] + b_ref[...]

    x0_scr[...] = short_conv(x0_ref, w0_ref, b0_ref)
    wv = short_conv(x1_ref, w1_ref, b1_ref) * short_conv(v_ref, w2_ref, b2_ref)
    wv_scr[...] = wv
    wb_scr[...] = wv.astype(BF16)
    sgn = jnp.where((row & 1) == 0, 1.0, -1.0)
    u_nyq = jnp.sum(wv * sgn, axis=0, keepdims=True)
    y_nyq = u_nyq * kim_ref[0:1, :]

    for kc in range(lx // ch):
        rows = slice(kc * ch, (kc + 1) * ch)
        wb = wb_scr[...]
        ure = jnp.dot(c_ref[rows, :], wb, preferred_element_type=F32)
        uim = -jnp.dot(s_ref[rows, :], wb, preferred_element_type=F32)
        kre = kre_ref[rows, :]
        kim = kim_ref[rows, :]
        yre = ure * kre - uim * kim
        yim = ure * kim + uim * kre
        if kc == 0:
            r0 = lax.broadcasted_iota(jnp.int32, (ch, tc), 0) == 0
            yre = jnp.where(r0, 0.5 * ure * kre, yre)
            yim = jnp.where(r0, 0.0, yim)
        yre_scr[rows, :] = yre.astype(BF16)
        yim_scr[rows, :] = yim.astype(BF16)

    inv_n = 1.0 / (2 * lx)
    for nc in range(lx // ch):
        rows = slice(nc * ch, (nc + 1) * ch)
        y = (jnp.dot(c_ref[rows, :], yre_scr[...], preferred_element_type=F32)
             - jnp.dot(s_ref[rows, :], yim_scr[...], preferred_element_type=F32))
        rr = lax.broadcasted_iota(jnp.int32, (ch, tc), 0)
        sg = jnp.where((rr & 1) == 0, inv_n, -inv_n)
        y = y * (2.0 * inv_n) + sg * y_nyq
        o_ref[rows, :] = x0_scr[rows, :] * (y + wv_scr[rows, :] * bias_ref[...])


def _hyena(lx, row_block0, p, short_w, short_b, kre, kim, bias, ctab, stab):
    tc = HY_COLS
    nj = HY_CH // tc
    seg = lambda s: pl.BlockSpec((lx, tc), lambda b, j: (row_block0 + b, s * nj + j))
    wseg = lambda s: pl.BlockSpec((SHORT_CONV, tc), lambda b, j: (0, s * nj + j))
    bseg = lambda s: pl.BlockSpec((1, tc), lambda b, j: (0, s * nj + j))
    col = lambda rows: pl.BlockSpec((rows, tc), lambda b, j: (0, j))
    tab = pl.BlockSpec((lx, lx), lambda b, j: (0, 0))
    return pl.pallas_call(
        functools.partial(_hyena_body, lx=lx),
        grid=(BATCH, nj),
        in_specs=[seg(0), seg(1), seg(2), wseg(0), wseg(1), wseg(2), bseg(0), bseg(1), bseg(2),
                  col(lx), col(lx), col(1), tab, tab],
        out_specs=pl.BlockSpec((lx, tc), lambda b, j: (b, j)),
        out_shape=jax.ShapeDtypeStruct((BATCH * lx, HY_CH), F32),
        scratch_shapes=[pltpu.VMEM((lx, tc), F32), pltpu.VMEM((lx, tc), F32), pltpu.VMEM((lx, tc), BF16),
                        pltpu.VMEM((lx, tc), BF16), pltpu.VMEM((lx, tc), BF16)],
        compiler_params=_cparams(("parallel", "parallel")),
        name=f"hyena_{lx}",
    )(p, p, p, short_w, short_w, short_w, short_b, short_b, short_b, kre, kim, bias, ctab, stab)


def _nabias_body(rpb_ref, o_ref, bm_scr):
    rb = pl.program_id(0)
    w = GRID_W

    @pl.when(rb == 0)
    def _():
        qc = lax.broadcasted_iota(jnp.int32, (w, w), 0)
        kc = lax.broadcasted_iota(jnp.int32, (w, w), 1)
        d = jnp.clip(kc - qc + (NA_WIN_COLS - 1), 0, 2 * NA_WIN_COLS - 2)
        c0 = jnp.clip(qc - NA_WIN_COLS // 2, 0, w - NA_WIN_COLS)
        ok = (kc >= c0) & (kc < c0 + NA_WIN_COLS)
        for h in range(NA_HEADS):
            for dr in range(2 * NA_WIN_ROWS - 1):
                acc = jnp.zeros((w, w), F32)
                for j in range(2 * NA_WIN_COLS - 1):
                    acc = jnp.where(d == j, rpb_ref[h, dr, j], acc)
                bm_scr[h, dr] = jnp.where(ok, acc, NEG_INF)

    start = jnp.clip(NA_QROWS * rb - NA_WIN_ROWS // 2, 0, GRID_ROWS - NA_KROWS)
    for h in range(NA_HEADS):
        for i in range(NA_QROWS):
            r = NA_QROWS * rb + i
            r0 = jnp.clip(r - NA_WIN_ROWS // 2, 0, GRID_ROWS - NA_WIN_ROWS)
            for j in range(NA_KROWS):
                kr = start + j
                valid = (kr >= r0) & (kr < r0 + NA_WIN_ROWS)
                didx = jnp.clip(kr - r + (NA_WIN_ROWS - 1), 0, 2 * NA_WIN_ROWS - 2)
                blk = jnp.where(valid, bm_scr[h, didx], NEG_INF)
                o_ref[0, h, i * w:(i + 1) * w, j * w:(j + 1) * w] = blk


def _na_bias(rpb):
    nrb = GRID_ROWS // NA_QROWS
    return pl.pallas_call(
        _nabias_body,
        grid=(nrb,),
        in_specs=[pl.BlockSpec(memory_space=pltpu.SMEM)],
        out_specs=pl.BlockSpec((1, NA_HEADS, NA_QROWS * GRID_W, NA_KROWS * GRID_W), lambda rb: (rb, 0, 0, 0)),
        out_shape=jax.ShapeDtypeStruct((nrb, NA_HEADS, NA_QROWS * GRID_W, NA_KROWS * GRID_W), F32),
        scratch_shapes=[pltpu.VMEM((NA_HEADS, 2 * NA_WIN_ROWS - 1, GRID_W, GRID_W), F32)],
        compiler_params=_cparams(("arbitrary",)),
        name="na_bias",
    )(rpb)


_NT = (((1,), (1,)), ((), ()))


def _na_body(q_ref, k_ref, v_ref, kc_ref, vc_ref, bias_ref, o_ref):
    rb = pl.program_id(1)
    start = jnp.clip(NA_QROWS * rb - NA_WIN_ROWS // 2, 0, GRID_ROWS - NA_KROWS)
    row0 = pl.multiple_of(start * GRID_W, GRID_W)
    nk = NA_KROWS * GRID_W
    scale = HEAD_DIM ** -0.5
    for h in range(NA_HEADS):
        cs = slice(h * HEAD_DIM, (h + 1) * HEAD_DIM)
        q = (q_ref[:, cs].astype(F32) * scale).astype(BF16)
        kw = k_ref[pl.ds(row0, nk), cs].astype(BF16)
        vw = v_ref[pl.ds(row0, nk), cs].astype(BF16)
        s = lax.dot_general(q, kw, _NT, preferred_element_type=F32)
        bm = bias_ref[0, h]
        s = jnp.where(bm > 0.5 * NEG_INF, s + bm, NEG_INF)
        sc = lax.dot_general(q, kc_ref[:, cs].astype(BF16), _NT, preferred_element_type=F32)
        m = jnp.maximum(jnp.max(s, axis=-1, keepdims=True), jnp.max(sc, axis=-1, keepdims=True))
        p = jnp.exp(s - m)
        pc = jnp.exp(sc - m)
        denom = jnp.sum(p, axis=-1, keepdims=True) + jnp.sum(pc, axis=-1, keepdims=True)
        o = (jnp.dot(p.astype(BF16), vw, preferred_element_type=F32)
             + jnp.dot(pc.astype(BF16), vc_ref[:, cs].astype(BF16), preferred_element_type=F32))
        o_ref[:, cs] = o / denom


def _na_latent(p, bias):
    nrb = GRID_ROWS // NA_QROWS
    tq = NA_QROWS * GRID_W
    cq, ck, cv = COL_NQ // NA_DIM, COL_NK // NA_DIM, COL_NV // NA_DIM
    cblk = T_LAT // CTX_LEN
    return pl.pallas_call(
        _na_body,
        grid=(BATCH, nrb),
        in_specs=[pl.BlockSpec((tq, NA_DIM), lambda b, r: (b * nrb + r, cq)),
                  pl.BlockSpec((SEQ, NA_DIM), lambda b, r: (b, ck)),
                  pl.BlockSpec((SEQ, NA_DIM), lambda b, r: (b, cv)),
                  pl.BlockSpec((CTX_LEN, NA_DIM), lambda b, r: (cblk + b, ck)),
                  pl.BlockSpec((CTX_LEN, NA_DIM), lambda b, r: (cblk + b, cv)),
                  pl.BlockSpec((1, NA_HEADS, tq, NA_KROWS * GRID_W), lambda b, r: (r, 0, 0, 0))],
        out_specs=pl.BlockSpec((tq, NA_DIM), lambda b, r: (b * nrb + r, 0)),
        out_shape=jax.ShapeDtypeStruct((T_LAT, NA_DIM), F32),
        compiler_params=_cparams(("parallel", "arbitrary")),
        name="na_latent",
    )(p, p, p, p, p, bias)


def _nactx_body(q_ref, k_ref, v_ref, o_ref):
    scale = HEAD_DIM ** -0.5
    for h in range(NA_HEADS):
        cs = slice(h * HEAD_DIM, (h + 1) * HEAD_DIM)
        q = (q_ref[:, cs].astype(F32) * scale).astype(BF16)
        s = lax.dot_general(q, k_ref[:, cs].astype(BF16), _NT, preferred_element_type=F32)
        m = jnp.max(s, axis=-1, keepdims=True)
        p = jnp.exp(s - m)
        denom = jnp.sum(p, axis=-1, keepdims=True)
        o = jnp.dot(p.astype(BF16), v_ref[:, cs].astype(BF16), preferred_element_type=F32)
        o_ref[:, cs] = o / denom


def _na_context(p):
    cq, ck, cv = COL_NQ // NA_DIM, COL_NK // NA_DIM, COL_NV // NA_DIM
    cblk = T_LAT // CTX_LEN
    spec = lambda c: pl.BlockSpec((CTX_LEN, NA_DIM), lambda b: (cblk + b, c))
    return pl.pallas_call(
        _nactx_body,
        grid=(BATCH,),
        in_specs=[spec(cq), spec(ck), spec(cv)],
        out_specs=pl.BlockSpec((CTX_LEN, NA_DIM), lambda b: (b, 0)),
        out_shape=jax.ShapeDtypeStruct((T_CTX, NA_DIM), F32),
        compiler_params=_cparams(("parallel",)),
        name="na_context",
    )(p, p, p)


_BNT = (((2,), (2,)), ((0,), (0,)))
_BNN = (((2,), (1,)), ((0,), (0,)))


def _rope_tables(lx):
    t = jnp.arange(lx)
    rows = (t // GRID_W).astype(F32)
    cols = (t % GRID_W).astype(F32)
    quarter = GLA_DK // 4
    inv = ROPE_BASE ** (-jnp.arange(quarter, dtype=F32) / quarter)
    ang_r = rows[:, None] * inv[None, :]
    ang_c = cols[:, None] * inv[None, :]
    cos = jnp.concatenate([jnp.cos(ang_r)] * 2 + [jnp.cos(ang_c)] * 2, axis=-1)
    sin = jnp.concatenate([-jnp.sin(ang_r), jnp.sin(ang_r), -jnp.sin(ang_c), jnp.sin(ang_c)], axis=-1)
    return jnp.tile(cos, (1, 2)), jnp.tile(sin, (1, 2))


def _log_sigmoid(z):
    return jnp.minimum(z, 0.0) - jnp.log(1.0 + jnp.exp(-jnp.abs(z)))


def _gla_body(*refs, lx, rope, with_out, with_init):
    it = iter(refs)
    q_ref = next(it) if with_out else None
    k_ref, v_ref = next(it), next(it)
    g_ref = next(it) if with_out else None
    ga_ref, w2_ref, b2_ref = next(it), next(it), next(it)
    cos_ref = sin_ref = None
    if rope:
        cos_ref, sin_ref = next(it), next(it)
    nw_ref = next(it) if with_out else None
    sf0_ref = sb0_ref = None
    if with_init:
        sf0_ref, sb0_ref = next(it), next(it)
    y_ref = next(it) if with_out else None
    sf_ref, sb_ref = next(it), next(it)
    ut_scr, stp_scr, dl_scr = next(it), next(it), next(it)

    c = GLA_CHUNK
    n = lx // c
    lane = lax.broadcasted_iota(jnp.int32, (lx, 2 * GLA_DK), 1)

    def rot(x):
        if not rope:
            return x
        even = ((lane // (GLA_DK // 4)) % 2) == 0
        partner = jnp.where(even, pltpu.roll(x, 2 * GLA_DK - GLA_DK // 4, 1), pltpu.roll(x, GLA_DK // 4, 1))
        return x * cos_ref[...] + partner * sin_ref[...]

    k = rot(k_ref[...].astype(F32))
    q = rot(q_ref[...].astype(F32) * (GLA_DK ** -0.5)) if with_out else None
    v3 = v_ref[...].reshape(n, c, 2 * GLA_DV)

    ga = ga_ref[...]
    log_f = _log_sigmoid(jnp.dot(ga, w2_ref[0], preferred_element_type=F32, precision=HIGHEST)
                         + b2_ref[0]) / GLA_GATE_NORM
    log_b = _log_sigmoid(jnp.dot(ga, w2_ref[1], preferred_element_type=F32, precision=HIGHEST)
                         + b2_ref[1]) / GLA_GATE_NORM
    pos = lax.broadcasted_iota(jnp.int32, (lx, 2 * GLA_DK), 0) % c
    cum_f, cum_b = log_f, log_b
    step = 1
    while step < c:
        cum_f = cum_f + jnp.where(pos >= step, pltpu.roll(cum_f, step, 0), 0.0)
        cum_b = cum_b + jnp.where(pos < c - step, pltpu.roll(cum_b, lx - step, 0), 0.0)
        step *= 2

    ti = lax.broadcasted_iota(jnp.int32, (n, c, c), 1)
    si = lax.broadcasted_iota(jnp.int32, (n, c, c), 2)
    o_acc = [None, None]
    for forward in (True, False):
        cum = cum_f if forward else cum_b
        cum3 = cum.reshape(n, c, 2 * GLA_DK)
        tot = cum3[:, c - 1:c, :] if forward else cum3[:, 0:1, :]
        kinv = (k * jnp.exp(-cum)).reshape(n, c, 2 * GLA_DK)
        kd = (kinv * jnp.exp(tot)).astype(BF16)
        kinv = kinv.astype(BF16)
        dl_scr[...] = jnp.exp(tot)
        qd3 = (q * jnp.exp(cum)).reshape(n, c, 2 * GLA_DK) if with_out else None
        keep = (ti >= si) if forward else (si >= ti)
        for hh in range(2):
            vh = v3[:, :, hh * GLA_DV:(hh + 1) * GLA_DV].astype(BF16)
            vt = jnp.swapaxes(vh, 1, 2)
            ut_scr[...] = lax.dot_general(vt, kd, _BNN, preferred_element_type=F32)
            st0 = (sf0_ref if forward else sb0_ref)[0, hh] if with_init else jnp.zeros((GLA_DV, 2 * GLA_DK), F32)

            def scan(i, st):
                ci = i if forward else n - 1 - i
                stp_scr[ci] = st
                return st * dl_scr[ci] + ut_scr[ci]

            st_fin = lax.fori_loop(0, n, scan, st0)
            (sf_ref if forward else sb_ref)[0, hh] = st_fin
            if with_out:
                in_head = (lax.broadcasted_iota(jnp.int32, (n, c, 2 * GLA_DK), 2) // GLA_DK) == hh
                qm = jnp.where(in_head, qd3, 0.0).astype(BF16)
                att = lax.dot_general(qm, kinv, _BNT, preferred_element_type=F32)
                att = jnp.where(keep, att, 0.0).astype(BF16)
                o = (lax.dot_general(att, vh, _BNN, preferred_element_type=F32)
                     + lax.dot_general(qm, stp_scr[...].astype(BF16), _BNT, preferred_element_type=F32))
                o_acc[hh] = o if o_acc[hh] is None else o_acc[hh] + o

    if with_out:
        for hh in range(2):
            vs = slice(hh * GLA_DV, (hh + 1) * GLA_DV)
            o = _rms_rows(o_acc[hh].reshape(lx, GLA_DV)) * nw_ref[...]
            y_ref[:, vs] = o * _silu(g_ref[:, vs].astype(F32))


def _gla(lx, row_block0, p, p_ga, w2cat, b2, norm_w, rope_tabs, init, with_out):
    npair = GLA_HEADS // 2
    rope = rope_tabs is not None
    with_init = init is not None
    rb = lambda b: row_block0 + b
    qk = lambda col: pl.BlockSpec((lx, 2 * GLA_DK), lambda b, h: (rb(b), col // (2 * GLA_DK) + h))
    vg = lambda col: pl.BlockSpec((lx, 2 * GLA_DV), lambda b, h: (rb(b), col // (2 * GLA_DV) + h))
    st_spec = pl.BlockSpec((1, 2, GLA_DV, 2 * GLA_DK), lambda b, h: (b, h, 0, 0))
    in_specs, args = [], []
    if with_out:
        in_specs.append(qk(COL_GQ)); args.append(p)
    in_specs += [qk(COL_GK), vg(COL_GV)]; args += [p, p]
    if with_out:
        in_specs.append(vg(COL_GG)); args.append(p)
    in_specs += [pl.BlockSpec((lx, 2 * GLA_RANK), lambda b, h: (rb(b), 0)),
                 pl.BlockSpec((2, 2 * GLA_RANK, 2 * GLA_DK), lambda b, h: (0, 0, h)),
                 pl.BlockSpec((2, 1, 2 * GLA_DK), lambda b, h: (0, 0, h))]
    args += [p_ga, w2cat, b2]
    if rope:
        in_specs += [pl.BlockSpec((lx, 2 * GLA_DK), lambda b, h: (0, 0))] * 2
        args += list(rope_tabs)
    if with_out:
        in_specs.append(pl.BlockSpec((1, GLA_DV), lambda b, h: (0, 0))); args.append(norm_w)
    if with_init:
        in_specs += [st_spec, st_spec]; args += list(init)
    st_shape = jax.ShapeDtypeStruct((BATCH, GLA_HEADS, GLA_DV, 2 * GLA_DK), F32)
    out_specs, out_shape = [], []
    if with_out:
        out_specs.append(pl.BlockSpec((lx, 2 * GLA_DV), lambda b, h: (b, h)))
        out_shape.append(jax.ShapeDtypeStruct((BATCH * lx, GLA_V), F32))
    out_specs += [st_spec, st_spec]
    out_shape += [st_shape, st_shape]
    n = lx // GLA_CHUNK
    outs = pl.pallas_call(
        functools.partial(_gla_body, lx=lx, rope=rope, with_out=with_out, with_init=with_init),
        grid=(BATCH, npair),
        in_specs=in_specs, out_specs=out_specs, out_shape=out_shape,
        scratch_shapes=[pltpu.VMEM((n, GLA_DV, 2 * GLA_DK), F32), pltpu.VMEM((n, GLA_DV, 2 * GLA_DK), F32),
                        pltpu.VMEM((n, 1, 2 * GLA_DK), F32)],
        compiler_params=_cparams(("parallel", "parallel")),
        name=f"gla_{lx}",
    )(*args)
    if with_out:
        return outs[0], outs[1], outs[2]
    return None, outs[0], outs[1]


def _outproj_body(*refs, with_ctx):
    it = iter(refs)
    lat = [next(it) for _ in range(3)]
    ctx = [next(it) for _ in range(3)] if with_ctx else None
    x_ref, mod_ref, hyw_ref, naw_ref, w_ref, g_ref, b_ref, o_ref = (next(it) for _ in range(8))

    def pick(j):
        if not with_ctx:
            return lat[j][...]
        return jnp.where(pl.program_id(0) >= T_LAT // ROW_TILE, ctx[j][...], lat[j][...])

    ya = (_rms_rows(pick(0)) * hyw_ref[...]).astype(BF16)
    yb = (_rms_rows(pick(1)) * naw_ref[...]).astype(BF16)
    yc = pick(2).astype(BF16)
    y = (jnp.dot(ya, w_ref[0:HY_CH, :], preferred_element_type=F32)
         + jnp.dot(yb, w_ref[HY_CH:HY_CH + NA_DIM, :], preferred_element_type=F32)
         + jnp.dot(yc, w_ref[HY_CH + NA_DIM:, :], preferred_element_type=F32))
    r = ALPHA * x_ref[...] + mod_ref[0, 2:3, :] * y
    o_ref[...] = _layer_norm_rows(r) * g_ref[...] + b_ref[...]


def _outproj(rows, lat, ctx, xa, mod, hy_norm_w, na_norm_w, w_out_bf, ln_g, ln_b):
    tm = ROW_TILE
    with_ctx = ctx is not None
    nlat = T_LAT // tm
    mrow = _mod_row(tm)
    widths = (HY_CH, NA_DIM, GLA_V)
    in_specs = [pl.BlockSpec((tm, w), lambda i: (jnp.minimum(i, nlat - 1), 0)) for w in widths]
    args = list(lat)
    if with_ctx:
        in_specs += [pl.BlockSpec((tm, w), lambda i: (jnp.maximum(i - nlat, 0), 0)) for w in widths]
        args += list(ctx)
    vec = lambda w: pl.BlockSpec((1, w), lambda i: (0, 0))
    in_specs += [pl.BlockSpec((tm, D_MODEL), lambda i: (i, 0)),
                 pl.BlockSpec((1, 6, D_MODEL), lambda i: (mrow(i), 0, 0)),
                 vec(HY_CH), vec(NA_DIM),
                 pl.BlockSpec((D_MODEL, D_MODEL), lambda i: (0, 0)),
                 vec(D_MODEL), vec(D_MODEL)]
    args += [xa, mod, hy_norm_w, na_norm_w, w_out_bf, ln_g, ln_b]
    return pl.pallas_call(
        functools.partial(_outproj_body, with_ctx=with_ctx),
        grid=(rows // tm,),
        in_specs=in_specs,
        out_specs=pl.BlockSpec((tm, D_MODEL), lambda i: (i, 0)),
        out_shape=jax.ShapeDtypeStruct((rows, D_MODEL), F32),
        compiler_params=_cparams(("parallel",)),
        name=f"outproj_{rows}",
    )(*args)


ROUTER_ROWS = 8 + N_EXPERTS


def _router_body(x_ref, mod_ref, w_ref, b_ref, h_ref, eid_ref, gw_ref):
    tm = ROW_TILE
    h2 = _layer_norm_rows(x_ref[...]) * (1.0 + mod_ref[0, 4:5, :]) + mod_ref[0, 3:4, :]
    h_ref[...] = h2
    logits = lax.dot_general(w_ref[...], h2, _NT, preferred_element_type=F32, precision=HIGHEST) + b_ref[...]
    lg = logits[0:N_GROUPS, :]
    eg = jnp.exp(lg - jnp.max(lg, axis=0, keepdims=True))
    pg = eg / jnp.sum(eg, axis=0, keepdims=True)
    top_pg = jnp.max(pg, axis=0, keepdims=True)
    gio = lax.broadcasted_iota(jnp.int32, (N_GROUPS, tm), 0)
    gi = jnp.min(jnp.where(pg == top_pg, gio, N_GROUPS), axis=0, keepdims=True)
    le = logits[8:8 + N_EXPERTS, :].reshape(N_GROUPS, EXPERTS_PER_GROUP, tm)
    gsel = lax.broadcasted_iota(jnp.int32, (N_GROUPS, EXPERTS_PER_GROUP, tm), 0) == gi[None]
    les = jnp.sum(jnp.where(gsel, le, 0.0), axis=0)
    eio = lax.broadcasted_iota(jnp.int32, (EXPERTS_PER_GROUP, tm), 0)
    v1 = jnp.max(les, axis=0, keepdims=True)
    i1 = jnp.min(jnp.where(les == v1, eio, EXPERTS_PER_GROUP), axis=0, keepdims=True)
    les2 = jnp.where(eio == i1, -jnp.inf, les)
    v2 = jnp.max(les2, axis=0, keepdims=True)
    i2 = jnp.min(jnp.where(les2 == v2, eio, EXPERTS_PER_GROUP), axis=0, keepdims=True)
    t = jnp.exp(v2 - v1)
    w1 = top_pg / (1.0 + t)
    eid_ref[...] = jnp.concatenate([gi * EXPERTS_PER_GROUP + i1, gi * EXPERTS_PER_GROUP + i2], axis=0)
    gw_ref[...] = jnp.concatenate([w1, w1 * t], axis=0)


def _router(rows, x1, mod, w_rt, b_rt):
    tm = ROW_TILE
    mrow = _mod_row(tm)
    return pl.pallas_call(
        _router_body,
        grid=(rows // tm,),
        in_specs=[pl.BlockSpec((tm, D_MODEL), lambda i: (i, 0)),
                  pl.BlockSpec((1, 6, D_MODEL), lambda i: (mrow(i), 0, 0)),
                  pl.BlockSpec((ROUTER_ROWS, D_MODEL), lambda i: (0, 0)),
                  pl.BlockSpec((ROUTER_ROWS, 1), lambda i: (0, 0))],
        out_specs=[pl.BlockSpec((tm, D_MODEL), lambda i: (i, 0)),
                   pl.BlockSpec((2, tm), lambda i: (0, i)),
                   pl.BlockSpec((2, tm), lambda i: (0, i))],
        out_shape=[jax.ShapeDtypeStruct((rows, D_MODEL), F32),
                   jax.ShapeDtypeStruct((2, rows), jnp.int32),
                   jax.ShapeDtypeStruct((2, rows), F32)],
        compiler_params=_cparams(("parallel",)),
        name=f"router_{rows}",
    )(x1, mod, w_rt, b_rt)


def _moe_tiles(rows):
    return -(-(2 * rows + N_EXPERTS * (MOE_TILE - 1)) // MOE_TILE)


def _dispatch_plan(rows, eid):
    nt = _moe_tiles(rows)
    e = eid.reshape(-1)
    experts = jnp.arange(N_EXPERTS, dtype=jnp.int32)
    onehot = (e[:, None] == experts[None, :]).astype(jnp.int32)
    csum = jnp.cumsum(onehot, axis=0)
    rank = jnp.sum(csum * onehot, axis=1) - 1
    counts = csum[-1]
    padded = ((counts + MOE_TILE - 1) // MOE_TILE) * MOE_TILE
    pend = jnp.cumsum(padded)
    pstart = pend - padded
    dest = jnp.sum(onehot * pstart[None, :], axis=1) + rank
    tile_start = jnp.arange(nt, dtype=jnp.int32) * MOE_TILE
    tile_valid = (tile_start < pend[-1]).astype(jnp.int32)
    tile_expert = jnp.sum((pend[None, :] <= tile_start[:, None]).astype(jnp.int32), axis=1)
    last_expert = jnp.max(jnp.where(counts > 0, experts, 0))
    tile_expert = jnp.minimum(tile_expert, last_expert)
    first = jnp.sum(((pstart[None, :] == tile_start[:, None]) & (padded[None, :] > 0)).astype(jnp.int32), axis=1)
    tile_first = jnp.minimum(first, 1) * tile_valid
    return dict(dest=dest, pend=pend, counts=counts, tile_expert=tile_expert, tile_valid=tile_valid,
                tile_first=tile_first, n_valid=(pend[-1] // MOE_TILE).reshape(1))


def _row_copy(src, src_row, dst, dst_row, sem):
    return pltpu.make_async_copy(src.at[pl.ds(src_row, 1), :], dst.at[pl.ds(dst_row, 1), :], sem)


def _dispatch_body(dest_ref, pend_ref, cnt_ref, h_hbm, xs_hbm, zero_scr, sem_z, sem, *, rows):
    i = pl.program_id(0)
    tm = ROW_TILE
    nt = xs_hbm.shape[0] // MOE_TILE

    def zero_tile(row0):
        return pltpu.make_async_copy(zero_scr, xs_hbm.at[pl.ds(pl.multiple_of(row0, MOE_TILE), MOE_TILE), :], sem_z)

    @pl.when(i == 0)
    def _():
        zero_scr[...] = jnp.zeros_like(zero_scr)
        n_used = pend_ref[N_EXPERTS - 1] // MOE_TILE
        for e in range(N_EXPERTS):
            @pl.when(cnt_ref[e] > 0)
            def _():
                zero_tile(pend_ref[e] - MOE_TILE).start()

        def start_unused(j, carry):
            zero_tile(j * MOE_TILE).start()
            return carry

        lax.fori_loop(n_used, nt, start_unused, 0)
        for e in range(N_EXPERTS):
            @pl.when(cnt_ref[e] > 0)
            def _():
                zero_tile(0).wait()

        def wait_unused(j, carry):
            zero_tile(0).wait()
            return carry

        lax.fori_loop(n_used, nt, wait_unused, 0)

    def start(r, carry):
        for s in range(2):
            _row_copy(h_hbm, i * tm + r, xs_hbm, dest_ref[s * rows + i * tm + r], sem).start()
        return carry

    lax.fori_loop(0, tm, start, 0, unroll=8)

    def wait(r, carry):
        for s in range(2):
            _row_copy(h_hbm, 0, xs_hbm, 0, sem).wait()
        return carry

    lax.fori_loop(0, tm, wait, 0)


def _dispatch(rows, plan, h2):
    grid_spec = pltpu.PrefetchScalarGridSpec(
        num_scalar_prefetch=3,
        grid=(rows // ROW_TILE,),
        in_specs=[pl.BlockSpec(memory_space=pl.ANY)],
        out_specs=pl.BlockSpec(memory_space=pl.ANY),
        scratch_shapes=[pltpu.VMEM((MOE_TILE, D_MODEL), F32), pltpu.SemaphoreType.DMA, pltpu.SemaphoreType.DMA],
    )
    return pl.pallas_call(
        functools.partial(_dispatch_body, rows=rows),
        grid_spec=grid_spec,
        out_shape=jax.ShapeDtypeStruct((_moe_tiles(rows) * MOE_TILE, D_MODEL), F32),
        compiler_params=_cparams(("arbitrary",)),
        name=f"moe_dispatch_{rows}",
    )(plan["dest"], plan["pend"], plan["counts"], h2)


def _moe_body(te_ref, tv_ref, tf_ref, nv_ref, x_ref, wup_ref, wdn_ref, o_ref, wup_bf, wdn_bf):
    i = pl.program_id(0)

    @pl.when(tv_ref[i] == 1)
    def _():
        @pl.when(tf_ref[i] == 1)
        def _():
            ch = 256
            for r in range(D_MODEL // ch):
                wup_bf[r * ch:(r + 1) * ch, :] = wup_ref[r * ch:(r + 1) * ch, :].astype(BF16)
            for r in range(D_EXPERT // ch):
                wdn_bf[r * ch:(r + 1) * ch, :] = wdn_ref[r * ch:(r + 1) * ch, :].astype(BF16)

        gu = jnp.dot(x_ref[...].astype(BF16), wup_bf[...], preferred_element_type=F32)
        act = _silu(gu[:, :D_EXPERT]) * gu[:, D_EXPERT:]
        o_ref[...] = jnp.dot(act.astype(BF16), wdn_bf[...], preferred_element_type=F32)

    @pl.when(tv_ref[i] == 0)
    def _():
        o_ref[...] = jnp.zeros_like(o_ref)


def _moe_ffn(rows, layer, plan, xs, w_up, w_down):
    nt = _moe_tiles(rows)
    tm = MOE_TILE

    def wmap(i, te, tv, tf, nv):
        return (layer, te[i] // EXPERTS_PER_GROUP, te[i] % EXPERTS_PER_GROUP, 0, 0)

    grid_spec = pltpu.PrefetchScalarGridSpec(
        num_scalar_prefetch=4,
        grid=(nt,),
        in_specs=[pl.BlockSpec((tm, D_MODEL), lambda i, te, tv, tf, nv: (jnp.minimum(i, nv[0] - 1), 0)),
                  pl.BlockSpec((None, None, None, D_MODEL, 2 * D_EXPERT), wmap),
                  pl.BlockSpec((None, None, None, D_EXPERT, D_MODEL), wmap)],
        out_specs=pl.BlockSpec((tm, D_MODEL), lambda i, te, tv, tf, nv: (i, 0)),
        scratch_shapes=[pltpu.VMEM((D_MODEL, 2 * D_EXPERT), BF16), pltpu.VMEM((D_EXPERT, D_MODEL), BF16)],
    )
    return pl.pallas_call(
        _moe_body,
        grid_spec=grid_spec,
        out_shape=jax.ShapeDtypeStruct((nt * tm, D_MODEL), F32),
        compiler_params=_cparams(("arbitrary",)),
        name=f"moe_ffn_{rows}",
    )(plan["tile_expert"], plan["tile_valid"], plan["tile_first"], plan["n_valid"], xs, w_up, w_down)


def _combine_body(pos_ref, y_hbm, gw_ref, x_ref, mod_ref, g_ref, b_ref, o_ref, y_scr, sem, *, rows):
    i = pl.program_id(0)
    tm = ROW_TILE

    def start(r, carry):
        for s in range(2):
            _row_copy(y_hbm, pos_ref[s * rows + i * tm + r], y_scr.at[s], r, sem).start()
        return carry

    lax.fori_loop(0, tm, start, 0, unroll=8)

    def wait(r, carry):
        for s in range(2):
            _row_copy(y_hbm, 0, y_scr.at[s], r, sem).wait()
        return carry

    lax.fori_loop(0, tm, wait, 0)
    f = y_scr[0] * gw_ref[:, 0:1] + y_scr[1] * gw_ref[:, 1:2]
    r = ALPHA * x_ref[...] + mod_ref[0, 5:6, :] * f
    o_ref[...] = _layer_norm_rows(r) * g_ref[...] + b_ref[...]


def _combine(rows, pos, ysort, gw_t, x1, mod, ln_g, ln_b):
    tm = ROW_TILE
    mrow = _mod_row(tm)
    vec = pl.BlockSpec((1, D_MODEL), lambda i, pos: (0, 0))
    grid_spec = pltpu.PrefetchScalarGridSpec(
        num_scalar_prefetch=1,
        grid=(rows // tm,),
        in_specs=[pl.BlockSpec(memory_space=pl.ANY),
                  pl.BlockSpec((tm, 2), lambda i, pos: (i, 0)),
                  pl.BlockSpec((tm, D_MODEL), lambda i, pos: (i, 0)),
                  pl.BlockSpec((1, 6, D_MODEL), lambda i, pos: (mrow(i), 0, 0)),
                  vec, vec],
        out_specs=pl.BlockSpec((tm, D_MODEL), lambda i, pos: (i, 0)),
        scratch_shapes=[pltpu.VMEM((2, tm, D_MODEL), F32), pltpu.SemaphoreType.DMA],
    )
    return pl.pallas_call(
        functools.partial(_combine_body, rows=rows),
        grid_spec=grid_spec,
        out_shape=jax.ShapeDtypeStruct((rows, D_MODEL), F32),
        compiler_params=_cparams(("arbitrary",)),
        name=f"combine_{rows}",
    )(pos, ysort, gw_t, x1, mod, ln_g, ln_b)


def _layer(xa, c8, tabs, w, big, layer, last):
    row = lambda v: v.reshape(1, -1)
    mod = _ada(c8, layer, big["w_ada"], w["b_ada"]).reshape(8, 6, D_MODEL)
    p, p_ga = _inproj(xa, mod, w["w_in"][:, :P_COLS].astype(BF16), w["w_in"][:, P_COLS:].astype(BF16))

    w1p = jnp.pad(w["hy_f_w1"], ((0, 128 - w["hy_f_w1"].shape[0]), (0, 0)))
    filt_args = (w1p, row(w["hy_f_b1"]), w["hy_f_w2"], row(w["hy_f_b2"]), w["hy_f_w3"], row(w["hy_sin_freq"]))
    short_b, hy_bias = row(w["hy_short_b"]), row(w["hy_bias"])
    kre, kim = _hyena_filter(SEQ, tabs["z_lat"], tabs["win_lat"], tabs["c_lat"], tabs["s_lat"], *filt_args)
    ya = _hyena(SEQ, 0, p, w["hy_short_w"], short_b, kre, kim, hy_bias, tabs["c_lat"], tabs["s_lat"])

    yb = _na_latent(p, _na_bias(w["na_rpb"]))

    zpad = jnp.zeros((GLA_RANK, GLA_K), F32)
    w2cat = jnp.stack([jnp.concatenate([w["gla_a_w2"][0], zpad], axis=0),
                       jnp.concatenate([zpad, w["gla_a_w2"][1]], axis=0)])
    b2 = w["gla_a_b"].reshape(2, 1, GLA_K)
    nw = row(w["gla_norm_w"])
    ctx_blk = T_LAT // CTX_LEN
    yc_ctx, sf, sb = _gla(CTX_LEN, ctx_blk, p, p_ga, w2cat, b2, nw, None, None, with_out=True)
    yc, _, _ = _gla(SEQ, 0, p, p_ga, w2cat, b2, nw, tabs["rope"], (sf, sb), with_out=True)

    lat = (ya, yb, yc)
    ctx = None
    if not last:
        kre_c, kim_c = _hyena_filter(CTX_LEN, tabs["z_ctx"], tabs["win_ctx"], tabs["c_ctx"], tabs["s_ctx"], *filt_args)
        ya_c = _hyena(CTX_LEN, ctx_blk, p, w["hy_short_w"], short_b, kre_c, kim_c, hy_bias,
                      tabs["c_ctx"], tabs["s_ctx"])
        ctx = (ya_c, _na_context(p), yc_ctx)

    rows = T_LAT if last else T_ALL
    x1 = _outproj(rows, lat, ctx, xa, mod, row(w["hy_norm_w"]), row(w["na_norm_w"]),
                  w["w_out"].astype(BF16), row(w["ln1_g"]), row(w["ln1_b"]))

    pad4 = jnp.zeros((8 - N_GROUPS, D_MODEL), F32)
    w_rt = jnp.concatenate([w["w_rg"].T, pad4, w["w_re"].T], axis=0)
    b_rt = jnp.concatenate([w["b_rg"], jnp.zeros((8 - N_GROUPS,), F32), w["b_re"]]).reshape(ROUTER_ROWS, 1)
    h2, eid, gw = _router(rows, x1, mod, w_rt, b_rt)
    plan = _dispatch_plan(rows, eid)
    ysort = _moe_ffn(rows, layer, plan, _dispatch(rows, plan, h2), big["w_up"], big["w_down"])
    return _combine(rows, plan["dest"], ysort, gw.T, x1, mod, row(w["ln2_g"]), row(w["ln2_b"]))


_BIG_WEIGHTS = ("w_ada", "w_up", "w_down")
_LAYER_WEIGHTS = ("w_ada", "b_ada", "w_in", "hy_short_w", "hy_short_b", "hy_f_w1", "hy_f_b1", "hy_f_w2", "hy_f_b2",
                  "hy_f_w3", "hy_sin_freq", "hy_bias", "hy_norm_w", "na_rpb", "na_norm_w", "gla_a_w2", "gla_a_b",
                  "gla_norm_w", "w_out", "ln1_g", "ln1_b", "w_rg", "b_rg", "w_re", "b_re", "w_up", "w_down",
                  "ln2_g", "ln2_b")


def kernel(x, c, ctx, c_ctx, w_ada, b_ada, w_in, hy_short_w, hy_short_b, hy_f_w1, hy_f_b1, hy_f_w2, hy_f_b2, hy_f_w3, hy_sin_freq, hy_bias, hy_norm_w, na_rpb, na_norm_w, gla_a_w2, gla_a_b, gla_norm_w, w_out, ln1_g, ln1_b, w_rg, b_rg, w_re, b_re, w_up, w_down, ln2_g, ln2_b):
    stacked = dict(zip(_LAYER_WEIGHTS, (w_ada, b_ada, w_in, hy_short_w, hy_short_b, hy_f_w1, hy_f_b1, hy_f_w2, hy_f_b2,
                                        hy_f_w3, hy_sin_freq, hy_bias, hy_norm_w, na_rpb, na_norm_w, gla_a_w2,
                                        gla_a_b, gla_norm_w, w_out, ln1_g, ln1_b, w_rg, b_rg, w_re, b_re, w_up,
                                        w_down, ln2_g, ln2_b)))
    tabs = {}
    tabs["c_lat"], tabs["s_lat"] = _dft_tables(SEQ)
    tabs["c_ctx"], tabs["s_ctx"] = _dft_tables(CTX_LEN)
    tabs["z_lat"], tabs["win_lat"] = _hyena_features(SEQ)
    tabs["z_ctx"], tabs["win_ctx"] = _hyena_features(CTX_LEN)
    tabs["rope"] = _rope_tables(SEQ)

    xa = jnp.concatenate([x.reshape(T_LAT, D_MODEL), ctx.reshape(T_CTX, D_MODEL)], axis=0)
    c8 = jnp.concatenate([c, c_ctx[None, :], jnp.zeros((8 - BATCH - 1, D_MODEL), F32)], axis=0)
    big = {k: stacked[k] for k in _BIG_WEIGHTS}
    for i in range(DEPTH):
        small = {k: v[i] for k, v in stacked.items() if k not in _BIG_WEIGHTS}
        xa = _layer(xa, c8, tabs, small, big, i, last=(i == DEPTH - 1))
    return xa.reshape(BATCH, SEQ, D_MODEL)
```

```python
import functools
import math

import jax
import jax.numpy as jnp
import numpy as np
from jax import lax
from jax.experimental import pallas as pl
from jax.experimental.pallas import tpu as pltpu

F32 = jnp.float32
BF16 = jnp.bfloat16
HIGHEST = lax.Precision.HIGHEST

D_MODEL = 2048
BATCH = 4
SEQ = 2048
DEPTH = 2
CTX_LEN = 256
GRID_W = 64
HEAD_DIM = 128
HY_CH = 512
NA_HEADS = 6
NA_DIM = 768
GLA_HEADS = 6
GLA_DK = 64
GLA_DV = 128
GLA_K = 384
GLA_V = 768
GLA_RANK = 16
GLA_GATE_NORM = 16.0
GLA_CHUNK = 64
ROPE_BASE = 10000.0
SHORT_CONV = 3
HY_EMB_BANDS = 16
HY_FILTER_WIDTH = 64
HY_MIN_DECAY = math.log(1e-2) / 1.5
HY_MAX_DECAY = math.log(1e-2) / 0.3
NA_WIN_ROWS = 8
NA_WIN_COLS = 16
NEG_INF = -1e30
N_GROUPS = 4
EXPERTS_PER_GROUP = 8
N_EXPERTS = N_GROUPS * EXPERTS_PER_GROUP
D_EXPERT = 512
ALPHA = (2 * DEPTH) ** 0.25
LN_EPS = 1e-6

T_LAT = BATCH * SEQ
T_CTX = BATCH * CTX_LEN
T_ALL = T_LAT + T_CTX
GRID_ROWS = SEQ // GRID_W

COL_HY = 0
COL_NQ = 3 * HY_CH
COL_NK = COL_NQ + NA_DIM
COL_NV = COL_NK + NA_DIM
COL_GQ = COL_NV + NA_DIM
COL_GK = COL_GQ + GLA_K
COL_GV = COL_GK + GLA_K
COL_GG = COL_GV + GLA_V
COL_GA = COL_GG + GLA_V
P_COLS = COL_GA

VMEM_LIMIT_BYTES = 56 * 1024 * 1024
ROW_TILE = 256
INPROJ_ROWS = 1024
INPROJ_COLS = 512
INPROJ_CHUNK = 128
HY_COLS = 256
DFT_CHUNK = 512
NA_QROWS = 4
NA_KROWS = NA_QROWS + NA_WIN_ROWS - 1
MOE_TILE = 256


def _cparams(sem, vmem=VMEM_LIMIT_BYTES):
    return pltpu.CompilerParams(dimension_semantics=sem, vmem_limit_bytes=vmem)


def _layer_norm_rows(x):
    mu = jnp.mean(x, axis=-1, keepdims=True)
    xc = x - mu
    return xc * lax.rsqrt(jnp.mean(xc * xc, axis=-1, keepdims=True) + LN_EPS)


def _rms_rows(x):
    return x * lax.rsqrt(jnp.mean(x * x, axis=-1, keepdims=True) + LN_EPS)


def _silu(x):
    return x / (1.0 + jnp.exp(-x))


def _mod_row(tile_rows):
    per_batch = SEQ // tile_rows
    return lambda i: jnp.minimum(i // per_batch, BATCH)


def _ada_body(c_ref, w_ref, b_ref, o_ref):
    s = _silu(c_ref[...])
    o_ref[...] = jnp.dot(s, w_ref[...], preferred_element_type=F32, precision=HIGHEST) + b_ref[...]


def _ada(c8, layer, w_ada, b_ada):
    tn = 1024
    n = 6 * D_MODEL
    return pl.pallas_call(
        _ada_body,
        grid=(n // tn,),
        in_specs=[pl.BlockSpec((8, D_MODEL), lambda j: (0, 0)),
                  pl.BlockSpec((None, D_MODEL, tn), lambda j: (layer, 0, j)),
                  pl.BlockSpec((1, tn), lambda j: (0, j))],
        out_specs=pl.BlockSpec((8, tn), lambda j: (0, j)),
        out_shape=jax.ShapeDtypeStruct((8, n), F32),
        compiler_params=_cparams(("parallel",)),
        name="ada_mod",
    )(c8, w_ada, b_ada.reshape(1, n))


def _inproj_body(x_ref, mod_ref, w_ref, wga_ref, p_ref, ga_ref, h_scr):
    @pl.when(pl.program_id(1) == 0)
    def _():
        sh = mod_ref[0, 0:1, :]
        sc = mod_ref[0, 1:2, :]

        def chunk(r, carry):
            rows = pl.ds(pl.multiple_of(r * INPROJ_CHUNK, INPROJ_CHUNK), INPROJ_CHUNK)
            h = _layer_norm_rows(x_ref[rows, :]) * (1.0 + sc) + sh
            h_scr[rows, :] = h.astype(BF16)
            return carry

        lax.fori_loop(0, INPROJ_ROWS // INPROJ_CHUNK, chunk, 0)
        ga_ref[...] = jnp.dot(h_scr[...], wga_ref[...], preferred_element_type=F32)

    p_ref[...] = jnp.dot(h_scr[...], w_ref[...], preferred_element_type=F32).astype(p_ref.dtype)


def _inproj(xa, mod, w_in_bf, w_ga_bf):
    tm, tn = INPROJ_ROWS, INPROJ_COLS
    mrow = _mod_row(tm)
    return pl.pallas_call(
        _inproj_body,
        grid=(T_ALL // tm, P_COLS // tn),
        in_specs=[pl.BlockSpec((tm, D_MODEL), lambda i, j: (i, 0)),
                  pl.BlockSpec((1, 6, D_MODEL), lambda i, j: (mrow(i), 0, 0)),
                  pl.BlockSpec((D_MODEL, tn), lambda i, j: (0, j)),
                  pl.BlockSpec((D_MODEL, 2 * GLA_RANK), lambda i, j: (0, 0))],
        out_specs=[pl.BlockSpec((tm, tn), lambda i, j: (i, j)),
                   pl.BlockSpec((tm, 2 * GLA_RANK), lambda i, j: (i, 0))],
        out_shape=[jax.ShapeDtypeStruct((T_ALL, P_COLS), BF16),
                   jax.ShapeDtypeStruct((T_ALL, 2 * GLA_RANK), F32)],
        scratch_shapes=[pltpu.VMEM((tm, D_MODEL), BF16)],
        compiler_params=_cparams(("parallel", "arbitrary")),
        name="inproj",
    )(xa, mod, w_in_bf, w_ga_bf)


def _dft_tables(lx):
    n = 2 * lx
    k = jnp.arange(lx, dtype=jnp.int32)
    idx = (k[:, None] * k[None, :]) % n
    ang = idx.astype(F32) * (2.0 * math.pi / n)
    return jnp.cos(ang).astype(BF16), jnp.sin(ang).astype(BF16)


def _hyena_features(lx):
    n = jnp.arange(lx, dtype=F32)
    t = n / max(lx - 1, 1)
    bands = jnp.linspace(1e-4, HY_EMB_BANDS - 1, HY_EMB_BANDS, dtype=F32)
    ang = (2.0 * math.pi / lx) * n[:, None] * bands[None, :]
    z = jnp.concatenate([t[:, None], jnp.cos(ang), -jnp.sin(ang)], axis=-1)
    z = jnp.pad(z, ((0, 0), (0, 128 - z.shape[1])))
    deltas = jnp.abs(jnp.linspace(HY_MIN_DECAY, HY_MAX_DECAY, HY_CH, dtype=F32))
    win = jnp.exp(-t[:, None] * deltas[None, :])
    return z, win


def _split_dot(tab, x):
    xh = x.astype(BF16)
    xl = (x - xh.astype(F32)).astype(BF16)
    return (jnp.dot(tab, xh, preferred_element_type=F32) + jnp.dot(tab, xl, preferred_element_type=F32))


def _hyfilt_body(z_ref, w1_ref, b1_ref, w2_ref, b2_ref, w3f_ref, w3b_ref, fr_ref, win_ref, c_ref, s_ref,
                 kre_ref, kim_ref):
    fr = fr_ref[...]
    hid = jnp.sin(fr * (jnp.dot(z_ref[...], w1_ref[...], preferred_element_type=F32, precision=HIGHEST) + b1_ref[...]))
    hid = jnp.sin(fr * (jnp.dot(hid, w2_ref[...], preferred_element_type=F32, precision=HIGHEST) + b2_ref[...]))
    win = win_ref[...]
    hf = jnp.dot(hid, w3f_ref[...], preferred_element_type=F32, precision=HIGHEST) * win
    hb = jnp.dot(hid, w3b_ref[...], preferred_element_type=F32, precision=HIGHEST) * win
    row = lax.broadcasted_iota(jnp.int32, hf.shape, 0)
    hb = jnp.where(row == 0, 0.0, hb)
    hs = hf + hb
    hd = hf - hb
    sgn = jnp.where((row & 1) == 0, 1.0, -1.0)
    k_nyq = jnp.sum(hs * sgn, axis=0, keepdims=True)
    kre_ref[...] = _split_dot(c_ref[...], hs)
    kim = -_split_dot(s_ref[...], hd)
    kim_ref[...] = jnp.where(row == 0, k_nyq, kim)


def _hyena_filter(lx, z, win, ctab, stab, w1p, b1, w2, b2, w3, fr):
    tc = HY_COLS
    nj = HY_CH // tc
    full = lambda shape: pl.BlockSpec(shape, lambda j: (0,) * len(shape))
    return pl.pallas_call(
        _hyfilt_body,
        grid=(nj,),
        in_specs=[full((lx, 128)), full((128, HY_FILTER_WIDTH)), full((1, HY_FILTER_WIDTH)),
                  full((HY_FILTER_WIDTH, HY_FILTER_WIDTH)), full((1, HY_FILTER_WIDTH)),
                  pl.BlockSpec((HY_FILTER_WIDTH, tc), lambda j: (0, j)),
                  pl.BlockSpec((HY_FILTER_WIDTH, tc), lambda j: (0, nj + j)),
                  full((1, HY_FILTER_WIDTH)),
                  pl.BlockSpec((lx, tc), lambda j: (0, j)),
                  full((lx, lx)), full((lx, lx))],
        out_specs=[pl.BlockSpec((lx, tc), lambda j: (0, j)), pl.BlockSpec((lx, tc), lambda j: (0, j))],
        out_shape=[jax.ShapeDtypeStruct((lx, HY_CH), F32), jax.ShapeDtypeStruct((lx, HY_CH), F32)],
        compiler_params=_cparams(("arbitrary",)),
        name=f"hyena_filter_{lx}",
    )(z, w1p, b1, w2, b2, w3, w3, fr, win, ctab, stab)


def _hyena_body(x0_ref, x1_ref, v_ref, w0_ref, w1_ref, w2_ref, b0_ref, b1_ref, b2_ref, kre_ref, kim_ref,
                bias_ref, c_ref, s_ref, o_ref, x0_scr, wv_scr, wb_scr, yre_scr, yim_scr, *, lx):
    tc = HY_COLS
    ch = min(DFT_CHUNK, lx)
    row = lax.broadcasted_iota(jnp.int32, (lx, tc), 0)
    first = row == 0
    last = row == lx - 1

    def short_conv(x_ref, w_ref, b_ref):
        x = x_ref[...].astype(F32)
        xm = jnp.where(first, 0.0, pltpu.roll(x, 1, 0))
        xp = jnp.where(last, 0.0, pltpu.roll(x, lx - 1, 0))
        return xm * w_ref[0:1, :] + x * w_ref[1:2, :] + xp * w_ref[2:3, :] + b_ref[...]

    x0_scr[...] = short_conv(x0_ref, w0_ref, b0_ref)
    wv = short_conv(x1_ref, w1_ref, b1_ref) * short_conv(v_ref, w2_ref, b2_ref)
    wv_scr[...] = wv
    wb_scr[...] = wv.astype(BF16)
    sgn = jnp.where((row & 1) == 0, 1.0, -1.0)
    u_nyq = jnp.sum(wv * sgn, axis=0, keepdims=True)
    y_nyq = u_nyq * kim_ref[0:1, :]

    for kc in range(lx // ch):
        rows = slice(kc * ch, (kc + 1) * ch)
        wb = wb_scr[...]
        ure = jnp.dot(c_ref[rows, :], wb, preferred_element_type=F32)
        uim = -jnp.dot(s_ref[rows, :], wb, preferred_element_type=F32)
        kre = kre_ref[rows, :]
        kim = kim_ref[rows, :]
        yre = ure * kre - uim * kim
        yim = ure * kim + uim * kre
        if kc == 0:
            r0 = lax.broadcasted_iota(jnp.int32, (ch, tc), 0) == 0
            yre = jnp.where(r0, 0.5 * ure * kre, yre)
            yim = jnp.where(r0, 0.0, yim)
        yre_scr[rows, :] = yre.astype(BF16)
        yim_scr[rows, :] = yim.astype(BF16)

    inv_n = 1.0 / (2 * lx)
    for nc in range(lx // ch):
        rows = slice(nc * ch, (nc + 1) * ch)
        y = (jnp.dot(c_ref[rows, :], yre_scr[...], preferred_element_type=F32)
             - jnp.dot(s_ref[rows, :], yim_scr[...], preferred_element_type=F32))
        rr = lax.broadcasted_iota(jnp.int32, (ch, tc), 0)
        sg = jnp.where((rr & 1) == 0, inv_n, -inv_n)
        y = y * (2.0 * inv_n) + sg * y_nyq
        o_ref[rows, :] = x0_scr[rows, :] * (y + wv_scr[rows, :] * bias_ref[...])


def _hyena(lx, row_block0, p, short_w, short_b, kre, kim, bias, ctab, stab):
    tc = HY_COLS
    nj = HY_CH // tc
    seg = lambda s: pl.BlockSpec((lx, tc), lambda b, j: (row_block0 + b, s * nj + j))
    wseg = lambda s: pl.BlockSpec((SHORT_CONV, tc), lambda b, j: (0, s * nj + j))
    bseg = lambda s: pl.BlockSpec((1, tc), lambda b, j: (0, s * nj + j))
    col = lambda rows: pl.BlockSpec((rows, tc), lambda b, j: (0, j))
    tab = pl.BlockSpec((lx, lx), lambda b, j: (0, 0))
    return pl.pallas_call(
        functools.partial(_hyena_body, lx=lx),
        grid=(BATCH, nj),
        in_specs=[seg(0), seg(1), seg(2), wseg(0), wseg(1), wseg(2), bseg(0), bseg(1), bseg(2),
                  col(lx), col(lx), col(1), tab, tab],
        out_specs=pl.BlockSpec((lx, tc), lambda b, j: (b, j)),
        out_shape=jax.ShapeDtypeStruct((BATCH * lx, HY_CH), F32),
        scratch_shapes=[pltpu.VMEM((lx, tc), F32), pltpu.VMEM((lx, tc), F32), pltpu.VMEM((lx, tc), BF16),
                        pltpu.VMEM((lx, tc), BF16), pltpu.VMEM((lx, tc), BF16)],
        compiler_params=_cparams(("parallel", "parallel")),
        name=f"hyena_{lx}",
    )(p, p, p, short_w, short_w, short_w, short_b, short_b, short_b, kre, kim, bias, ctab, stab)


def _nabias_body(rpb_ref, o_ref, bm_scr):
    rb = pl.program_id(0)
    w = GRID_W

    @pl.when(rb == 0)
    def _():
        qc = lax.broadcasted_iota(jnp.int32, (w, w), 0)
        kc = lax.broadcasted_iota(jnp.int32, (w, w), 1)
        d = jnp.clip(kc - qc + (NA_WIN_COLS - 1), 0, 2 * NA_WIN_COLS - 2)
        c0 = jnp.clip(qc - NA_WIN_COLS // 2, 0, w - NA_WIN_COLS)
        ok = (kc >= c0) & (kc < c0 + NA_WIN_COLS)
        for h in range(NA_HEADS):
            for dr in range(2 * NA_WIN_ROWS - 1):
                acc = jnp.zeros((w, w), F32)
                for j in range(2 * NA_WIN_COLS - 1):
                    acc = jnp.where(d == j, rpb_ref[h, dr, j], acc)
                bm_scr[h, dr] = jnp.where(ok, acc, NEG_INF)

    start = jnp.clip(NA_QROWS * rb - NA_WIN_ROWS // 2, 0, GRID_ROWS - NA_KROWS)
    for h in range(NA_HEADS):
        for i in range(NA_QROWS):
            r = NA_QROWS * rb + i
            r0 = jnp.clip(r - NA_WIN_ROWS // 2, 0, GRID_ROWS - NA_WIN_ROWS)
            for j in range(NA_KROWS):
                kr = start + j
                valid = (kr >= r0) & (kr < r0 + NA_WIN_ROWS)
                didx = jnp.clip(kr - r + (NA_WIN_ROWS - 1), 0, 2 * NA_WIN_ROWS - 2)
                blk = jnp.where(valid, bm_scr[h, didx], NEG_INF)
                o_ref[0, h, i * w:(i + 1) * w, j * w:(j + 1) * w] = blk


def _na_bias(rpb):
    nrb = GRID_ROWS // NA_QROWS
    return pl.pallas_call(
        _nabias_body,
        grid=(nrb,),
        in_specs=[pl.BlockSpec(memory_space=pltpu.SMEM)],
        out_specs=pl.BlockSpec((1, NA_HEADS, NA_QROWS * GRID_W, NA_KROWS * GRID_W), lambda rb: (rb, 0, 0, 0)),
        out_shape=jax.ShapeDtypeStruct((nrb, NA_HEADS, NA_QROWS * GRID_W, NA_KROWS * GRID_W), F32),
        scratch_shapes=[pltpu.VMEM((NA_HEADS, 2 * NA_WIN_ROWS - 1, GRID_W, GRID_W), F32)],
        compiler_params=_cparams(("arbitrary",)),
        name="na_bias",
    )(rpb)


_NT = (((1,), (1,)), ((), ()))


def _na_body(q_ref, k_ref, v_ref, kc_ref, vc_ref, bias_ref, o_ref):
    rb = pl.program_id(1)
    start = jnp.clip(NA_QROWS * rb - NA_WIN_ROWS // 2, 0, GRID_ROWS - NA_KROWS)
    row0 = pl.multiple_of(start * GRID_W, GRID_W)
    nk = NA_KROWS * GRID_W
    scale = HEAD_DIM ** -0.5
    for h in range(NA_HEADS):
        cs = slice(h * HEAD_DIM, (h + 1) * HEAD_DIM)
        q = (q_ref[:, cs].astype(F32) * scale).astype(BF16)
        kw = k_ref[pl.ds(row0, nk), cs].astype(BF16)
        vw = v_ref[pl.ds(row0, nk), cs].astype(BF16)
        s = lax.dot_general(q, kw, _NT, preferred_element_type=F32)
        bm = bias_ref[0, h]
        s = jnp.where(bm > 0.5 * NEG_INF, s + bm, NEG_INF)
        sc = lax.dot_general(q, kc_ref[:, cs].astype(BF16), _NT, preferred_element_type=F32)
        m = jnp.maximum(jnp.max(s, axis=-1, keepdims=True), jnp.max(sc, axis=-1, keepdims=True))
        p = jnp.exp(s - m)
        pc = jnp.exp(sc - m)
        denom = jnp.sum(p, axis=-1, keepdims=True) + jnp.sum(pc, axis=-1, keepdims=True)
        o = (jnp.dot(p.astype(BF16), vw, preferred_element_type=F32)
             + jnp.dot(pc.astype(BF16), vc_ref[:, cs].astype(BF16), preferred_element_type=F32))
        o_ref[:, cs] = o / denom


def _na_latent(p, bias):
    nrb = GRID_ROWS // NA_QROWS
    tq = NA_QROWS * GRID_W
    cq, ck, cv = COL_NQ // NA_DIM, COL_NK // NA_DIM, COL_NV // NA_DIM
    cblk = T_LAT // CTX_LEN
    return pl.pallas_call(
        _na_body,
        grid=(BATCH, nrb),
        in_specs=[pl.BlockSpec((tq, NA_DIM), lambda b, r: (b * nrb + r, cq)),
                  pl.BlockSpec((SEQ, NA_DIM), lambda b, r: (b, ck)),
                  pl.BlockSpec((SEQ, NA_DIM), lambda b, r: (b, cv)),
                  pl.BlockSpec((CTX_LEN, NA_DIM), lambda b, r: (cblk + b, ck)),
                  pl.BlockSpec((CTX_LEN, NA_DIM), lambda b, r: (cblk + b, cv)),
                  pl.BlockSpec((1, NA_HEADS, tq, NA_KROWS * GRID_W), lambda b, r: (r, 0, 0, 0))],
        out_specs=pl.BlockSpec((tq, NA_DIM), lambda b, r: (b * nrb + r, 0)),
        out_shape=jax.ShapeDtypeStruct((T_LAT, NA_DIM), F32),
        compiler_params=_cparams(("parallel", "arbitrary")),
        name="na_latent",
    )(p, p, p, p, p, bias)


def _nactx_body(q_ref, k_ref, v_ref, o_ref):
    scale = HEAD_DIM ** -0.5
    for h in range(NA_HEADS):
        cs = slice(h * HEAD_DIM, (h + 1) * HEAD_DIM)
        q = (q_ref[:, cs].astype(F32) * scale).astype(BF16)
        s = lax.dot_general(q, k_ref[:, cs].astype(BF16), _NT, preferred_element_type=F32)
        m = jnp.max(s, axis=-1, keepdims=True)
        p = jnp.exp(s - m)
        denom = jnp.sum(p, axis=-1, keepdims=True)
        o = jnp.dot(p.astype(BF16), v_ref[:, cs].astype(BF16), preferred_element_type=F32)
        o_ref[:, cs] = o / denom


def _na_context(p):
    cq, ck, cv = COL_NQ // NA_DIM, COL_NK // NA_DIM, COL_NV // NA_DIM
    cblk = T_LAT // CTX_LEN
    spec = lambda c: pl.BlockSpec((CTX_LEN, NA_DIM), lambda b: (cblk + b, c))
    return pl.pallas_call(
        _nactx_body,
        grid=(BATCH,),
        in_specs=[spec(cq), spec(ck), spec(cv)],
        out_specs=pl.BlockSpec((CTX_LEN, NA_DIM), lambda b: (b, 0)),
        out_shape=jax.ShapeDtypeStruct((T_CTX, NA_DIM), F32),
        compiler_params=_cparams(("parallel",)),
        name="na_context",
    )(p, p, p)


_BNT = (((2,), (2,)), ((0,), (0,)))
_BNN = (((2,), (1,)), ((0,), (0,)))


def _rope_tables(lx):
    t = jnp.arange(lx)
    rows = (t // GRID_W).astype(F32)
    cols = (t % GRID_W).astype(F32)
    quarter = GLA_DK // 4
    inv = ROPE_BASE ** (-jnp.arange(quarter, dtype=F32) / quarter)
    ang_r = rows[:, None] * inv[None, :]
    ang_c = cols[:, None] * inv[None, :]
    cos = jnp.concatenate([jnp.cos(ang_r)] * 2 + [jnp.cos(ang_c)] * 2, axis=-1)
    sin = jnp.concatenate([-jnp.sin(ang_r), jnp.sin(ang_r), -jnp.sin(ang_c), jnp.sin(ang_c)], axis=-1)
    return jnp.tile(cos, (1, 2)), jnp.tile(sin, (1, 2))


def _log_sigmoid(z):
    return jnp.minimum(z, 0.0) - jnp.log(1.0 + jnp.exp(-jnp.abs(z)))


def _gla_body(*refs, lx, rope, with_out, with_init):
    it = iter(refs)
    q_ref = next(it) if with_out else None
    k_ref, v_ref = next(it), next(it)
    g_ref = next(it) if with_out else None
    ga_ref, w2_ref, b2_ref = next(it), next(it), next(it)
    cos_ref = sin_ref = None
    if rope:
        cos_ref, sin_ref = next(it), next(it)
    nw_ref = next(it) if with_out else None
    sf0_ref = sb0_ref = None
    if with_init:
        sf0_ref, sb0_ref = next(it), next(it)
    y_ref = next(it) if with_out else None
    sf_ref, sb_ref = next(it), next(it)
    ut_scr, stp_scr, dl_scr = next(it), next(it), next(it)

    c = GLA_CHUNK
    n = lx // c
    lane = lax.broadcasted_iota(jnp.int32, (lx, 2 * GLA_DK), 1)

    def rot(x):
        if not rope:
            return x
        even = ((lane // (GLA_DK // 4)) % 2) == 0
        partner = jnp.where(even, pltpu.roll(x, 2 * GLA_DK - GLA_DK // 4, 1), pltpu.roll(x, GLA_DK // 4, 1))
        return x * cos_ref[...] + partner * sin_ref[...]

    k = rot(k_ref[...].astype(F32))
    q = rot(q_ref[...].astype(F32) * (GLA_DK ** -0.5)) if with_out else None
    v3 = v_ref[...].reshape(n, c, 2 * GLA_DV)

    ga = ga_ref[...]
    log_f = _log_sigmoid(jnp.dot(ga, w2_ref[0], preferred_element_type=F32, precision=HIGHEST)
                         + b2_ref[0]) / GLA_GATE_NORM
    log_b = _log_sigmoid(jnp.dot(ga, w2_ref[1], preferred_element_type=F32, precision=HIGHEST)
                         + b2_ref[1]) / GLA_GATE_NORM
    pos = lax.broadcasted_iota(jnp.int32, (lx, 2 * GLA_DK), 0) % c
    cum_f, cum_b = log_f, log_b
    step = 1
    while step < c:
        cum_f = cum_f + jnp.where(pos >= step, pltpu.roll(cum_f, step, 0), 0.0)
        cum_b = cum_b + jnp.where(pos < c - step, pltpu.roll(cum_b, lx - step, 0), 0.0)
        step *= 2

    ti = lax.broadcasted_iota(jnp.int32, (n, c, c), 1)
    si = lax.broadcasted_iota(jnp.int32, (n, c, c), 2)
    o_acc = [None, None]
    for forward in (True, False):
        cum = cum_f if forward else cum_b
        cum3 = cum.reshape(n, c, 2 * GLA_DK)
        tot = cum3[:, c - 1:c, :] if forward else cum3[:, 0:1, :]
        kinv = (k * jnp.exp(-cum)).reshape(n, c, 2 * GLA_DK)
        kd = (kinv * jnp.exp(tot)).astype(BF16)
        kinv = kinv.astype(BF16)
        dl_scr[...] = jnp.exp(tot)
        qd3 = (q * jnp.exp(cum)).reshape(n, c, 2 * GLA_DK) if with_out else None
        keep = (ti >= si) if forward else (si >= ti)
        for hh in range(2):
            vh = v3[:, :, hh * GLA_DV:(hh + 1) * GLA_DV].astype(BF16)
            vt = jnp.swapaxes(vh, 1, 2)
            ut_scr[...] = lax.dot_general(vt, kd, _BNN, preferred_element_type=F32)
            st0 = (sf0_ref if forward else sb0_ref)[0, hh] if with_init else jnp.zeros((GLA_DV, 2 * GLA_DK), F32)

            def scan(i, st):
                ci = i if forward else n - 1 - i
                stp_scr[ci] = st
                return st * dl_scr[ci] + ut_scr[ci]

            st_fin = lax.fori_loop(0, n, scan, st0)
            (sf_ref if forward else sb_ref)[0, hh] = st_fin
            if with_out:
                in_head = (lax.broadcasted_iota(jnp.int32, (n, c, 2 * GLA_DK), 2) // GLA_DK) == hh
                qm = jnp.where(in_head, qd3, 0.0).astype(BF16)
                att = lax.dot_general(qm, kinv, _BNT, preferred_element_type=F32)
                att = jnp.where(keep, att, 0.0).astype(BF16)
                o = (lax.dot_general(att, vh, _BNN, preferred_element_type=F32)
                     + lax.dot_general(qm, stp_scr[...].astype(BF16), _BNT, preferred_element_type=F32))
                o_acc[hh] = o if o_acc[hh] is None else o_acc[hh] + o

    if with_out:
        for hh in range(2):
            vs = slice(hh * GLA_DV, (hh + 1) * GLA_DV)
            o = _rms_rows(o_acc[hh].reshape(lx, GLA_DV)) * nw_ref[...]
            y_ref[:, vs] = o * _silu(g_ref[:, vs].astype(F32))


def _gla(lx, row_block0, p, p_ga, w2cat, b2, norm_w, rope_tabs, init, with_out):
    npair = GLA_HEADS // 2
    rope = rope_tabs is not None
    with_init = init is not None
    rb = lambda b: row_block0 + b
    qk = lambda col: pl.BlockSpec((lx, 2 * GLA_DK), lambda b, h: (rb(b), col // (2 * GLA_DK) + h))
    vg = lambda col: pl.BlockSpec((lx, 2 * GLA_DV), lambda b, h: (rb(b), col // (2 * GLA_DV) + h))
    st_spec = pl.BlockSpec((1, 2, GLA_DV, 2 * GLA_DK), lambda b, h: (b, h, 0, 0))
    in_specs, args = [], []
    if with_out:
        in_specs.append(qk(COL_GQ)); args.append(p)
    in_specs += [qk(COL_GK), vg(COL_GV)]; args += [p, p]
    if with_out:
        in_specs.append(vg(COL_GG)); args.append(p)
    in_specs += [pl.BlockSpec((lx, 2 * GLA_RANK), lambda b, h: (rb(b), 0)),
                 pl.BlockSpec((2, 2 * GLA_RANK, 2 * GLA_DK), lambda b, h: (0, 0, h)),
                 pl.BlockSpec((2, 1, 2 * GLA_DK), lambda b, h: (0, 0, h))]
    args += [p_ga, w2cat, b2]
    if rope:
        in_specs += [pl.BlockSpec((lx, 2 * GLA_DK), lambda b, h: (0, 0))] * 2
        args += list(rope_tabs)
    if with_out:
        in_specs.append(pl.BlockSpec((1, GLA_DV), lambda b, h: (0, 0))); args.append(norm_w)
    if with_init:
        in_specs += [st_spec, st_spec]; args += list(init)
    st_shape = jax.ShapeDtypeStruct((BATCH, GLA_HEADS, GLA_DV, 2 * GLA_DK), F32)
    out_specs, out_shape = [], []
    if with_out:
        out_specs.append(pl.BlockSpec((lx, 2 * GLA_DV), lambda b, h: (b, h)))
        out_shape.append(jax.ShapeDtypeStruct((BATCH * lx, GLA_V), F32))
    out_specs += [st_spec, st_spec]
    out_shape += [st_shape, st_shape]
    n = lx // GLA_CHUNK
    outs = pl.pallas_call(
        functools.partial(_gla_body, lx=lx, rope=rope, with_out=with_out, with_init=with_init),
        grid=(BATCH, npair),
        in_specs=in_specs, out_specs=out_specs, out_shape=out_shape,
        scratch_shapes=[pltpu.VMEM((n, GLA_DV, 2 * GLA_DK), F32), pltpu.VMEM((n, GLA_DV, 2 * GLA_DK), F32),
                        pltpu.VMEM((n, 1, 2 * GLA_DK), F32)],
        compiler_params=_cparams(("parallel", "parallel")),
        name=f"gla_{lx}",
    )(*args)
    if with_out:
        return outs[0], outs[1], outs[2]
    return None, outs[0], outs[1]


def _outproj_body(*refs, with_ctx):
    it = iter(refs)
    lat = [next(it) for _ in range(3)]
    ctx = [next(it) for _ in range(3)] if with_ctx else None
    x_ref, mod_ref, hyw_ref, naw_ref, w_ref, g_ref, b_ref, o_ref = (next(it) for _ in range(8))

    def pick(j):
        if not with_ctx:
            return lat[j][...]
        return jnp.where(pl.program_id(0) >= T_LAT // ROW_TILE, ctx[j][...], lat[j][...])

    ya = (_rms_rows(pick(0)) * hyw_ref[...]).astype(BF16)
    yb = (_rms_rows(pick(1)) * naw_ref[...]).astype(BF16)
    yc = pick(2).astype(BF16)
    y = (jnp.dot(ya, w_ref[0:HY_CH, :], preferred_element_type=F32)
         + jnp.dot(yb, w_ref[HY_CH:HY_CH + NA_DIM, :], preferred_element_type=F32)
         + jnp.dot(yc, w_ref[HY_CH + NA_DIM:, :], preferred_element_type=F32))
    r = ALPHA * x_ref[...] + mod_ref[0, 2:3, :] * y
    o_ref[...] = _layer_norm_rows(r) * g_ref[...] + b_ref[...]


def _outproj(rows, lat, ctx, xa, mod, hy_norm_w, na_norm_w, w_out_bf, ln_g, ln_b):
    tm = ROW_TILE
    with_ctx = ctx is not None
    nlat = T_LAT // tm
    mrow = _mod_row(tm)
    widths = (HY_CH, NA_DIM, GLA_V)
    in_specs = [pl.BlockSpec((tm, w), lambda i: (jnp.minimum(i, nlat - 1), 0)) for w in widths]
    args = list(lat)
    if with_ctx:
        in_specs += [pl.BlockSpec((tm, w), lambda i: (jnp.maximum(i - nlat, 0), 0)) for w in widths]
        args += list(ctx)
    vec = lambda w: pl.BlockSpec((1, w), lambda i: (0, 0))
    in_specs += [pl.BlockSpec((tm, D_MODEL), lambda i: (i, 0)),
                 pl.BlockSpec((1, 6, D_MODEL), lambda i: (mrow(i), 0, 0)),
                 vec(HY_CH), vec(NA_DIM),
                 pl.BlockSpec((D_MODEL, D_MODEL), lambda i: (0, 0)),
                 vec(D_MODEL), vec(D_MODEL)]
    args += [xa, mod, hy_norm_w, na_norm_w, w_out_bf, ln_g, ln_b]
    return pl.pallas_call(
        functools.partial(_outproj_body, with_ctx=with_ctx),
        grid=(rows // tm,),
        in_specs=in_specs,
        out_specs=pl.BlockSpec((tm, D_MODEL), lambda i: (i, 0)),
        out_shape=jax.ShapeDtypeStruct((rows, D_MODEL), F32),
        compiler_params=_cparams(("parallel",)),
        name=f"outproj_{rows}",
    )(*args)


ROUTER_ROWS = 8 + N_EXPERTS


def _router_body(x_ref, mod_ref, w_ref, b_ref, h_ref, eid_ref, gw_ref):
    tm = ROW_TILE
    h2 = _layer_norm_rows(x_ref[...]) * (1.0 + mod_ref[0, 4:5, :]) + mod_ref[0, 3:4, :]
    h_ref[...] = h2
    logits = lax.dot_general(w_ref[...], h2, _NT, preferred_element_type=F32, precision=HIGHEST) + b_ref[...]
    lg = logits[0:N_GROUPS, :]
    eg = jnp.exp(lg - jnp.max(lg, axis=0, keepdims=True))
    pg = eg / jnp.sum(eg, axis=0, keepdims=True)
    top_pg = jnp.max(pg, axis=0, keepdims=True)
    gio = lax.broadcasted_iota(jnp.int32, (N_GROUPS, tm), 0)
    gi = jnp.min(jnp.where(pg == top_pg, gio, N_GROUPS), axis=0, keepdims=True)
    le = logits[8:8 + N_EXPERTS, :].reshape(N_GROUPS, EXPERTS_PER_GROUP, tm)
    gsel = lax.broadcasted_iota(jnp.int32, (N_GROUPS, EXPERTS_PER_GROUP, tm), 0) == gi[None]
    les = jnp.sum(jnp.where(gsel, le, 0.0), axis=0)
    eio = lax.broadcasted_iota(jnp.int32, (EXPERTS_PER_GROUP, tm), 0)
    v1 = jnp.max(les, axis=0, keepdims=True)
    i1 = jnp.min(jnp.where(les == v1, eio, EXPERTS_PER_GROUP), axis=0, keepdims=True)
    les2 = jnp.where(eio == i1, -jnp.inf, les)
    v2 = jnp.max(les2, axis=0, keepdims=True)
    i2 = jnp.min(jnp.where(les2 == v2, eio, EXPERTS_PER_GROUP), axis=0, keepdims=True)
    t = jnp.exp(v2 - v1)
    w1 = top_pg / (1.0 + t)
    eid_ref[...] = jnp.concatenate([gi * EXPERTS_PER_GROUP + i1, gi * EXPERTS_PER_GROUP + i2], axis=0)
    gw_ref[...] = jnp.concatenate([w1, w1 * t], axis=0)


def _router(rows, x1, mod, w_rt, b_rt):
    tm = ROW_TILE
    mrow = _mod_row(tm)
    return pl.pallas_call(
        _router_body,
        grid=(rows // tm,),
        in_specs=[pl.BlockSpec((tm, D_MODEL), lambda i: (i, 0)),
                  pl.BlockSpec((1, 6, D_MODEL), lambda i: (mrow(i), 0, 0)),
                  pl.BlockSpec((ROUTER_ROWS, D_MODEL), lambda i: (0, 0)),
                  pl.BlockSpec((ROUTER_ROWS, 1), lambda i: (0, 0))],
        out_specs=[pl.BlockSpec((tm, D_MODEL), lambda i: (i, 0)),
                   pl.BlockSpec((2, tm), lambda i: (0, i)),
                   pl.BlockSpec((2, tm), lambda i: (0, i))],
        out_shape=[jax.ShapeDtypeStruct((rows, D_MODEL), F32),
                   jax.ShapeDtypeStruct((2, rows), jnp.int32),
                   jax.ShapeDtypeStruct((2, rows), F32)],
        compiler_params=_cparams(("parallel",)),
        name=f"router_{rows}",
    )(x1, mod, w_rt, b_rt)


def _moe_tiles(rows):
    return -(-(2 * rows + N_EXPERTS * (MOE_TILE - 1)) // MOE_TILE)


def _dispatch_plan(rows, eid):
    nt = _moe_tiles(rows)
    e = eid.reshape(-1)
    experts = jnp.arange(N_EXPERTS, dtype=jnp.int32)
    onehot = (e[:, None] == experts[None, :]).astype(jnp.int32)
    csum = jnp.cumsum(onehot, axis=0)
    rank = jnp.sum(csum * onehot, axis=1) - 1
    counts = csum[-1]
    padded = ((counts + MOE_TILE - 1) // MOE_TILE) * MOE_TILE
    pend = jnp.cumsum(padded)
    pstart = pend - padded
    dest = jnp.sum(onehot * pstart[None, :], axis=1) + rank
    tile_start = jnp.arange(nt, dtype=jnp.int32) * MOE_TILE
    tile_valid = (tile_start < pend[-1]).astype(jnp.int32)
    tile_expert = jnp.sum((pend[None, :] <= tile_start[:, None]).astype(jnp.int32), axis=1)
    last_expert = jnp.max(jnp.where(counts > 0, experts, 0))
    tile_expert = jnp.minimum(tile_expert, last_expert)
    first = jnp.sum(((pstart[None, :] == tile_start[:, None]) & (padded[None, :] > 0)).astype(jnp.int32), axis=1)
    tile_first = jnp.minimum(first, 1) * tile_valid
    return dict(dest=dest, pend=pend, counts=counts, tile_expert=tile_expert, tile_valid=tile_valid,
                tile_first=tile_first, n_valid=(pend[-1] // MOE_TILE).reshape(1))


def _row_copy(src, src_row, dst, dst_row, sem):
    return pltpu.make_async_copy(src.at[pl.ds(src_row, 1), :], dst.at[pl.ds(dst_row, 1), :], sem)


def _dispatch_body(dest_ref, pend_ref, cnt_ref, h_ref, xs_hbm, zero_scr, sem_z, sem, *, rows):
    i = pl.program_id(0)
    tm = ROW_TILE
    nt = xs_hbm.shape[0] // MOE_TILE

    def zero_tile(row0):
        return pltpu.make_async_copy(zero_scr, xs_hbm.at[pl.ds(pl.multiple_of(row0, MOE_TILE), MOE_TILE), :], sem_z)

    @pl.when(i == 0)
    def _():
        zero_scr[...] = jnp.zeros_like(zero_scr)
        n_used = pend_ref[N_EXPERTS - 1] // MOE_TILE
        for e in range(N_EXPERTS):
            @pl.when(cnt_ref[e] > 0)
            def _():
                zero_tile(pend_ref[e] - MOE_TILE).start()

        def start_unused(j, carry):
            zero_tile(j * MOE_TILE).start()
            return carry

        lax.fori_loop(n_used, nt, start_unused, 0)
        for e in range(N_EXPERTS):
            @pl.when(cnt_ref[e] > 0)
            def _():
                zero_tile(0).wait()

        def wait_unused(j, carry):
            zero_tile(0).wait()
            return carry

        lax.fori_loop(n_used, nt, wait_unused, 0)

    def start(r, carry):
        for s in range(2):
            _row_copy(h_ref, r, xs_hbm, dest_ref[s * rows + i * tm + r], sem).start()
        return carry

    lax.fori_loop(0, tm, start, 0, unroll=8)

    def wait(r, carry):
        for s in range(2):
            _row_copy(h_ref, 0, xs_hbm, 0, sem).wait()
        return carry

    lax.fori_loop(0, tm, wait, 0)


def _dispatch(rows, plan, h2):
    grid_spec = pltpu.PrefetchScalarGridSpec(
        num_scalar_prefetch=3,
        grid=(rows // ROW_TILE,),
        in_specs=[pl.BlockSpec((ROW_TILE, D_MODEL), lambda i, dest, pend, cnt: (i, 0))],
        out_specs=pl.BlockSpec(memory_space=pl.ANY),
        scratch_shapes=[pltpu.VMEM((MOE_TILE, D_MODEL), F32), pltpu.SemaphoreType.DMA, pltpu.SemaphoreType.DMA],
    )
    return pl.pallas_call(
        functools.partial(_dispatch_body, rows=rows),
        grid_spec=grid_spec,
        out_shape=jax.ShapeDtypeStruct((_moe_tiles(rows) * MOE_TILE, D_MODEL), F32),
        compiler_params=_cparams(("arbitrary",)),
        name=f"moe_dispatch_{rows}",
    )(plan["dest"], plan["pend"], plan["counts"], h2)


def _moe_body(te_ref, tv_ref, tf_ref, nv_ref, x_ref, wup_ref, wdn_ref, o_ref, wup_bf, wdn_bf):
    i = pl.program_id(0)

    @pl.when(tv_ref[i] == 1)
    def _():
        @pl.when(tf_ref[i] == 1)
        def _():
            ch = 256
            for r in range(D_MODEL // ch):
                wup_bf[r * ch:(r + 1) * ch, :] = wup_ref[r * ch:(r + 1) * ch, :].astype(BF16)
            for r in range(D_EXPERT // ch):
                wdn_bf[r * ch:(r + 1) * ch, :] = wdn_ref[r * ch:(r + 1) * ch, :].astype(BF16)

        gu = jnp.dot(x_ref[...].astype(BF16), wup_bf[...], preferred_element_type=F32)
        act = _silu(gu[:, :D_EXPERT]) * gu[:, D_EXPERT:]
        o_ref[...] = jnp.dot(act.astype(BF16), wdn_bf[...], preferred_element_type=F32)

    @pl.when(tv_ref[i] == 0)
    def _():
        o_ref[...] = jnp.zeros_like(o_ref)


def _moe_ffn(rows, layer, plan, xs, w_up, w_down):
    nt = _moe_tiles(rows)
    tm = MOE_TILE

    def wmap(i, te, tv, tf, nv):
        return (layer, te[i] // EXPERTS_PER_GROUP, te[i] % EXPERTS_PER_GROUP, 0, 0)

    grid_spec = pltpu.PrefetchScalarGridSpec(
        num_scalar_prefetch=4,
        grid=(nt,),
        in_specs=[pl.BlockSpec((tm, D_MODEL), lambda i, te, tv, tf, nv: (jnp.minimum(i, nv[0] - 1), 0)),
                  pl.BlockSpec((None, None, None, D_MODEL, 2 * D_EXPERT), wmap),
                  pl.BlockSpec((None, None, None, D_EXPERT, D_MODEL), wmap)],
        out_specs=pl.BlockSpec((tm, D_MODEL), lambda i, te, tv, tf, nv: (i, 0)),
        scratch_shapes=[pltpu.VMEM((D_MODEL, 2 * D_EXPERT), BF16), pltpu.VMEM((D_EXPERT, D_MODEL), BF16)],
    )
    return pl.pallas_call(
        _moe_body,
        grid_spec=grid_spec,
        out_shape=jax.ShapeDtypeStruct((nt * tm, D_MODEL), F32),
        compiler_params=_cparams(("arbitrary",)),
        name=f"moe_ffn_{rows}",
    )(plan["tile_expert"], plan["tile_valid"], plan["tile_first"], plan["n_valid"], xs, w_up, w_down)


def _combine_body(pos_ref, y_hbm, gw_ref, x_ref, mod_ref, g_ref, b_ref, o_ref, y_scr, sem, *, rows):
    i = pl.program_id(0)
    tm = ROW_TILE

    def start(r, carry):
        for s in range(2):
            _row_copy(y_hbm, pos_ref[s * rows + i * tm + r], y_scr.at[s], r, sem).start()
        return carry

    lax.fori_loop(0, tm, start, 0, unroll=8)

    def wait(r, carry):
        for s in range(2):
            _row_copy(y_hbm, 0, y_scr.at[s], r, sem).wait()
        return carry

    lax.fori_loop(0, tm, wait, 0)
    f = y_scr[0] * gw_ref[:, 0:1] + y_scr[1] * gw_ref[:, 1:2]
    r = ALPHA * x_ref[...] + mod_ref[0, 5:6, :] * f
    o_ref[...] = _layer_norm_rows(r) * g_ref[...] + b_ref[...]


def _combine(rows, pos, ysort, gw_t, x1, mod, ln_g, ln_b):
    tm = ROW_TILE
    mrow = _mod_row(tm)
    vec = pl.BlockSpec((1, D_MODEL), lambda i, pos: (0, 0))
    grid_spec = pltpu.PrefetchScalarGridSpec(
        num_scalar_prefetch=1,
        grid=(rows // tm,),
        in_specs=[pl.BlockSpec(memory_space=pl.ANY),
                  pl.BlockSpec((tm, 2), lambda i, pos: (i, 0)),
                  pl.BlockSpec((tm, D_MODEL), lambda i, pos: (i, 0)),
                  pl.BlockSpec((1, 6, D_MODEL), lambda i, pos: (mrow(i), 0, 0)),
                  vec, vec],
        out_specs=pl.BlockSpec((tm, D_MODEL), lambda i, pos: (i, 0)),
        scratch_shapes=[pltpu.VMEM((2, tm, D_MODEL), F32), pltpu.SemaphoreType.DMA],
    )
    return pl.pallas_call(
        functools.partial(_combine_body, rows=rows),
        grid_spec=grid_spec,
        out_shape=jax.ShapeDtypeStruct((rows, D_MODEL), F32),
        compiler_params=_cparams(("arbitrary",)),
        name=f"combine_{rows}",
    )(pos, ysort, gw_t, x1, mod, ln_g, ln_b)


def _layer(xa, c8, tabs, w, big, layer, last):
    row = lambda v: v.reshape(1, -1)
    mod = _ada(c8, layer, big["w_ada"], w["b_ada"]).reshape(8, 6, D_MODEL)
    p, p_ga = _inproj(xa, mod, w["w_in"][:, :P_COLS].astype(BF16), w["w_in"][:, P_COLS:].astype(BF16))

    w1p = jnp.pad(w["hy_f_w1"], ((0, 128 - w["hy_f_w1"].shape[0]), (0, 0)))
    filt_args = (w1p, row(w["hy_f_b1"]), w["hy_f_w2"], row(w["hy_f_b2"]), w["hy_f_w3"], row(w["hy_sin_freq"]))
    short_b, hy_bias = row(w["hy_short_b"]), row(w["hy_bias"])
    kre, kim = _hyena_filter(SEQ, tabs["z_lat"], tabs["win_lat"], tabs["c_lat"], tabs["s_lat"], *filt_args)
    ya = _hyena(SEQ, 0, p, w["hy_short_w"], short_b, kre, kim, hy_bias, tabs["c_lat"], tabs["s_lat"])

    yb = _na_latent(p, _na_bias(w["na_rpb"]))

    zpad = jnp.zeros((GLA_RANK, GLA_K), F32)
    w2cat = jnp.stack([jnp.concatenate([w["gla_a_w2"][0], zpad], axis=0),
                       jnp.concatenate([zpad, w["gla_a_w2"][1]], axis=0)])
    b2 = w["gla_a_b"].reshape(2, 1, GLA_K)
    nw = row(w["gla_norm_w"])
    ctx_blk = T_LAT // CTX_LEN
    yc_ctx, sf, sb = _gla(CTX_LEN, ctx_blk, p, p_ga, w2cat, b2, nw, None, None, with_out=True)
    yc, _, _ = _gla(SEQ, 0, p, p_ga, w2cat, b2, nw, tabs["rope"], (sf, sb), with_out=True)

    lat = (ya, yb, yc)
    ctx = None
    if not last:
        kre_c, kim_c = _hyena_filter(CTX_LEN, tabs["z_ctx"], tabs["win_ctx"], tabs["c_ctx"], tabs["s_ctx"], *filt_args)
        ya_c = _hyena(CTX_LEN, ctx_blk, p, w["hy_short_w"], short_b, kre_c, kim_c, hy_bias,
                      tabs["c_ctx"], tabs["s_ctx"])
        ctx = (ya_c, _na_context(p), yc_ctx)

    rows = T_LAT if last else T_ALL
    x1 = _outproj(rows, lat, ctx, xa, mod, row(w["hy_norm_w"]), row(w["na_norm_w"]),
                  w["w_out"].astype(BF16), row(w["ln1_g"]), row(w["ln1_b"]))

    pad4 = jnp.zeros((8 - N_GROUPS, D_MODEL), F32)
    w_rt = jnp.concatenate([w["w_rg"].T, pad4, w["w_re"].T], axis=0)
    b_rt = jnp.concatenate([w["b_rg"], jnp.zeros((8 - N_GROUPS,), F32), w["b_re"]]).reshape(ROUTER_ROWS, 1)
    h2, eid, gw = _router(rows, x1, mod, w_rt, b_rt)
    plan = _dispatch_plan(rows, eid)
    ysort = _moe_ffn(rows, layer, plan, _dispatch(rows, plan, h2), big["w_up"], big["w_down"])
    return _combine(rows, plan["dest"], ysort, gw.T, x1, mod, row(w["ln2_g"]), row(w["ln2_b"]))


_BIG_WEIGHTS = ("w_ada", "w_up", "w_down")
_LAYER_WEIGHTS = ("w_ada", "b_ada", "w_in", "hy_short_w", "hy_short_b", "hy_f_w1", "hy_f_b1", "hy_f_w2", "hy_f_b2",
                  "hy_f_w3", "hy_sin_freq", "hy_bias", "hy_norm_w", "na_rpb", "na_norm_w", "gla_a_w2", "gla_a_b",
                  "gla_norm_w", "w_out", "ln1_g", "ln1_b", "w_rg", "b_rg", "w_re", "b_re", "w_up", "w_down",
                  "ln2_g", "ln2_b")


def kernel(x, c, ctx, c_ctx, w_ada, b_ada, w_in, hy_short_w, hy_short_b, hy_f_w1, hy_f_b1, hy_f_w2, hy_f_b2, hy_f_w3, hy_sin_freq, hy_bias, hy_norm_w, na_rpb, na_norm_w, gla_a_w2, gla_a_b, gla_norm_w, w_out, ln1_g, ln1_b, w_rg, b_rg, w_re, b_re, w_up, w_down, ln2_g, ln2_b):
    stacked = dict(zip(_LAYER_WEIGHTS, (w_ada, b_ada, w_in, hy_short_w, hy_short_b, hy_f_w1, hy_f_b1, hy_f_w2, hy_f_b2,
                                        hy_f_w3, hy_sin_freq, hy_bias, hy_norm_w, na_rpb, na_norm_w, gla_a_w2,
                                        gla_a_b, gla_norm_w, w_out, ln1_g, ln1_b, w_rg, b_rg, w_re, b_re, w_up,
                                        w_down, ln2_g, ln2_b)))
    tabs = {}
    tabs["c_lat"], tabs["s_lat"] = _dft_tables(SEQ)
    tabs["c_ctx"], tabs["s_ctx"] = _dft_tables(CTX_LEN)
    tabs["z_lat"], tabs["win_lat"] = _hyena_features(SEQ)
    tabs["z_ctx"], tabs["win_ctx"] = _hyena_features(CTX_LEN)
    tabs["rope"] = _rope_tables(SEQ)

    xa = jnp.concatenate([x.reshape(T_LAT, D_MODEL), ctx.reshape(T_CTX, D_MODEL)], axis=0)
    c8 = jnp.concatenate([c, c_ctx[None, :], jnp.zeros((8 - BATCH - 1, D_MODEL), F32)], axis=0)
    big = {k: stacked[k] for k in _BIG_WEIGHTS}
    for i in range(DEPTH):
        small = {k: v[i] for k, v in stacked.items() if k not in _BIG_WEIGHTS}
        xa = _layer(xa, c8, tabs, small, big, i, last=(i == DEPTH - 1))
    return xa.reshape(BATCH, SEQ, D_MODEL)
```

```python
import functools
import math

import jax
import jax.numpy as jnp
import numpy as np
from jax import lax
from jax.experimental import pallas as pl
from jax.experimental.pallas import tpu as pltpu

F32 = jnp.float32
BF16 = jnp.bfloat16
HIGHEST = lax.Precision.HIGHEST

D_MODEL = 2048
BATCH = 4
SEQ = 2048
DEPTH = 2
CTX_LEN = 256
GRID_W = 64
HEAD_DIM = 128
HY_CH = 512
NA_HEADS = 6
NA_DIM = 768
GLA_HEADS = 6
GLA_DK = 64
GLA_DV = 128
GLA_K = 384
GLA_V = 768
GLA_RANK = 16
GLA_GATE_NORM = 16.0
GLA_CHUNK = 64
ROPE_BASE = 10000.0
SHORT_CONV = 3
HY_EMB_BANDS = 16
HY_FILTER_WIDTH = 64
HY_MIN_DECAY = math.log(1e-2) / 1.5
HY_MAX_DECAY = math.log(1e-2) / 0.3
NA_WIN_ROWS = 8
NA_WIN_COLS = 16
NEG_INF = -1e30
N_GROUPS = 4
EXPERTS_PER_GROUP = 8
N_EXPERTS = N_GROUPS * EXPERTS_PER_GROUP
D_EXPERT = 512
ALPHA = (2 * DEPTH) ** 0.25
LN_EPS = 1e-6

T_LAT = BATCH * SEQ
T_CTX = BATCH * CTX_LEN
T_ALL = T_LAT + T_CTX
GRID_ROWS = SEQ // GRID_W

COL_HY = 0
COL_NQ = 3 * HY_CH
COL_NK = COL_NQ + NA_DIM
COL_NV = COL_NK + NA_DIM
COL_GQ = COL_NV + NA_DIM
COL_GK = COL_GQ + GLA_K
COL_GV = COL_GK + GLA_K
COL_GG = COL_GV + GLA_V
COL_GA = COL_GG + GLA_V
P_COLS = COL_GA

VMEM_LIMIT_BYTES = 56 * 1024 * 1024
ROW_TILE = 256
INPROJ_ROWS = 1024
INPROJ_COLS = 512
INPROJ_CHUNK = 128
HY_COLS = 256
DFT_CHUNK = 512
NA_QROWS = 4
NA_KROWS = NA_QROWS + NA_WIN_ROWS - 1
MOE_TILE = 256


def _cparams(sem, vmem=VMEM_LIMIT_BYTES):
    return pltpu.CompilerParams(dimension_semantics=sem, vmem_limit_bytes=vmem)


def _layer_norm_rows(x):
    mu = jnp.mean(x, axis=-1, keepdims=True)
    xc = x - mu
    return xc * lax.rsqrt(jnp.mean(xc * xc, axis=-1, keepdims=True) + LN_EPS)


def _rms_rows(x):
    return x * lax.rsqrt(jnp.mean(x * x, axis=-1, keepdims=True) + LN_EPS)


def _silu(x):
    return x / (1.0 + jnp.exp(-x))


def _mod_row(tile_rows):
    per_batch = SEQ // tile_rows
    return lambda i: jnp.minimum(i // per_batch, BATCH)


def _ada_body(c_ref, w_ref, b_ref, o_ref):
    s = _silu(c_ref[...])
    o_ref[...] = jnp.dot(s, w_ref[...], preferred_element_type=F32, precision=HIGHEST) + b_ref[...]


def _ada(c8, layer, w_ada, b_ada):
    tn = 1024
    n = 6 * D_MODEL
    return pl.pallas_call(
        _ada_body,
        grid=(n // tn,),
        in_specs=[pl.BlockSpec((8, D_MODEL), lambda j: (0, 0)),
                  pl.BlockSpec((None, D_MODEL, tn), lambda j: (layer, 0, j)),
                  pl.BlockSpec((1, tn), lambda j: (0, j))],
        out_specs=pl.BlockSpec((8, tn), lambda j: (0, j)),
        out_shape=jax.ShapeDtypeStruct((8, n), F32),
        compiler_params=_cparams(("parallel",)),
        name="ada_mod",
    )(c8, w_ada, b_ada.reshape(1, n))


def _inproj_body(x_ref, mod_ref, w_ref, wga_ref, p_ref, ga_ref, h_scr):
    @pl.when(pl.program_id(1) == 0)
    def _():
        sh = mod_ref[0, 0:1, :]
        sc = mod_ref[0, 1:2, :]

        def chunk(r, carry):
            rows = pl.ds(pl.multiple_of(r * INPROJ_CHUNK, INPROJ_CHUNK), INPROJ_CHUNK)
            h = _layer_norm_rows(x_ref[rows, :]) * (1.0 + sc) + sh
            h_scr[rows, :] = h.astype(BF16)
            return carry

        lax.fori_loop(0, INPROJ_ROWS // INPROJ_CHUNK, chunk, 0)
        ga_ref[...] = jnp.dot(h_scr[...], wga_ref[...], preferred_element_type=F32)

    p_ref[...] = jnp.dot(h_scr[...], w_ref[...], preferred_element_type=F32).astype(p_ref.dtype)


def _inproj(xa, mod, w_in_bf, w_ga_bf):
    tm, tn = INPROJ_ROWS, INPROJ_COLS
    mrow = _mod_row(tm)
    return pl.pallas_call(
        _inproj_body,
        grid=(T_ALL // tm, P_COLS // tn),
        in_specs=[pl.BlockSpec((tm, D_MODEL), lambda i, j: (i, 0)),
                  pl.BlockSpec((1, 6, D_MODEL), lambda i, j: (mrow(i), 0, 0)),
                  pl.BlockSpec((D_MODEL, tn), lambda i, j: (0, j)),
                  pl.BlockSpec((D_MODEL, 2 * GLA_RANK), lambda i, j: (0, 0))],
        out_specs=[pl.BlockSpec((tm, tn), lambda i, j: (i, j)),
                   pl.BlockSpec((tm, 2 * GLA_RANK), lambda i, j: (i, 0))],
        out_shape=[jax.ShapeDtypeStruct((T_ALL, P_COLS), BF16),
                   jax.ShapeDtypeStruct((T_ALL, 2 * GLA_RANK), F32)],
        scratch_shapes=[pltpu.VMEM((tm, D_MODEL), BF16)],
        compiler_params=_cparams(("parallel", "arbitrary")),
        name="inproj",
    )(xa, mod, w_in_bf, w_ga_bf)


def _dft_tables(lx):
    n = 2 * lx
    k = jnp.arange(lx, dtype=jnp.int32)
    idx = (k[:, None] * k[None, :]) % n
    ang = idx.astype(F32) * (2.0 * math.pi / n)
    return jnp.cos(ang).astype(BF16), jnp.sin(ang).astype(BF16)


def _hyena_features(lx):
    n = jnp.arange(lx, dtype=F32)
    t = n / max(lx - 1, 1)
    bands = jnp.linspace(1e-4, HY_EMB_BANDS - 1, HY_EMB_BANDS, dtype=F32)
    ang = (2.0 * math.pi / lx) * n[:, None] * bands[None, :]
    z = jnp.concatenate([t[:, None], jnp.cos(ang), -jnp.sin(ang)], axis=-1)
    z = jnp.pad(z, ((0, 0), (0, 128 - z.shape[1])))
    deltas = jnp.abs(jnp.linspace(HY_MIN_DECAY, HY_MAX_DECAY, HY_CH, dtype=F32))
    win = jnp.exp(-t[:, None] * deltas[None, :])
    return z, win


def _split_dot(tab, x):
    xh = x.astype(BF16)
    xl = (x - xh.astype(F32)).astype(BF16)
    return (jnp.dot(tab, xh, preferred_element_type=F32) + jnp.dot(tab, xl, preferred_element_type=F32))


def _hyfilt_body(z_ref, w1_ref, b1_ref, w2_ref, b2_ref, w3f_ref, w3b_ref, fr_ref, win_ref, c_ref, s_ref,
                 kre_ref, kim_ref):
    fr = fr_ref[...]
    hid = jnp.sin(fr * (jnp.dot(z_ref[...], w1_ref[...], preferred_element_type=F32, precision=HIGHEST) + b1_ref[...]))
    hid = jnp.sin(fr * (jnp.dot(hid, w2_ref[...], preferred_element_type=F32, precision=HIGHEST) + b2_ref[...]))
    win = win_ref[...]
    hf = jnp.dot(hid, w3f_ref[...], preferred_element_type=F32, precision=HIGHEST) * win
    hb = jnp.dot(hid, w3b_ref[...], preferred_element_type=F32, precision=HIGHEST) * win
    row = lax.broadcasted_iota(jnp.int32, hf.shape, 0)
    hb = jnp.where(row == 0, 0.0, hb)
    hs = hf + hb
    hd = hf - hb
    sgn = jnp.where((row & 1) == 0, 1.0, -1.0)
    k_nyq = jnp.sum(hs * sgn, axis=0, keepdims=True)
    kre_ref[...] = _split_dot(c_ref[...], hs)
    kim = -_split_dot(s_ref[...], hd)
    kim_ref[...] = jnp.where(row == 0, k_nyq, kim)


def _hyena_filter(lx, z, win, ctab, stab, w1p, b1, w2, b2, w3, fr):
    tc = HY_COLS
    nj = HY_CH // tc
    full = lambda shape: pl.BlockSpec(shape, lambda j: (0,) * len(shape))
    return pl.pallas_call(
        _hyfilt_body,
        grid=(nj,),
        in_specs=[full((lx, 128)), full((128, HY_FILTER_WIDTH)), full((1, HY_FILTER_WIDTH)),
                  full((HY_FILTER_WIDTH, HY_FILTER_WIDTH)), full((1, HY_FILTER_WIDTH)),
                  pl.BlockSpec((HY_FILTER_WIDTH, tc), lambda j: (0, j)),
                  pl.BlockSpec((HY_FILTER_WIDTH, tc), lambda j: (0, nj + j)),
                  full((1, HY_FILTER_WIDTH)),
                  pl.BlockSpec((lx, tc), lambda j: (0, j)),
                  full((lx, lx)), full((lx, lx))],
        out_specs=[pl.BlockSpec((lx, tc), lambda j: (0, j)), pl.BlockSpec((lx, tc), lambda j: (0, j))],
        out_shape=[jax.ShapeDtypeStruct((lx, HY_CH), F32), jax.ShapeDtypeStruct((lx, HY_CH), F32)],
        compiler_params=_cparams(("arbitrary",)),
        name=f"hyena_filter_{lx}",
    )(z, w1p, b1, w2, b2, w3, w3, fr, win, ctab, stab)


def _hyena_body(x0_ref, x1_ref, v_ref, w0_ref, w1_ref, w2_ref, b0_ref, b1_ref, b2_ref, kre_ref, kim_ref,
                bias_ref, c_ref, s_ref, o_ref, x0_scr, wv_scr, wb_scr, yre_scr, yim_scr, *, lx):
    tc = HY_COLS
    ch = min(DFT_CHUNK, lx)
    row = lax.broadcasted_iota(jnp.int32, (lx, tc), 0)
    first = row == 0
    last = row == lx - 1

    def short_conv(x_ref, w_ref, b_ref):
        x = x_ref[...].astype(F32)
        xm = jnp.where(first, 0.0, pltpu.roll(x, 1, 0))
        xp = jnp.where(last, 0.0, pltpu.roll(x, lx - 1, 0))
        return xm * w_ref[0:1, :] + x * w_ref[1:2, :] + xp * w_ref[2:3, :] + b_ref[...]

    x0_scr[...] = short_conv(x0_ref, w0_ref, b0_ref)
    wv = short_conv(x1_ref, w1_ref, b1_ref) * short_conv(v_ref, w2_ref, b2_ref)
    wv_scr[...] = wv
    wb_scr[...] = wv.astype(BF16)
    sgn = jnp.where((row & 1) == 0, 1.0, -1.0)
    u_nyq = jnp.sum(wv * sgn, axis=0, keepdims=True)
    y_nyq = u_nyq * kim_ref[0:1, :]

    for kc in range(lx // ch):
        rows = slice(kc * ch, (kc + 1) * ch)
        wb = wb_scr[...]
        ure = jnp.dot(c_ref[rows, :], wb, preferred_element_type=F32)
        uim = -jnp.dot(s_ref[rows, :], wb, preferred_element_type=F32)
        kre = kre_ref[rows, :]
        kim = kim_ref[rows, :]
        yre = ure * kre - uim * kim
        yim = ure * kim + uim * kre
        if kc == 0:
            r0 = lax.broadcasted_iota(jnp.int32, (ch, tc), 0) == 0
            yre = jnp.where(r0, 0.5 * ure * kre, yre)
            yim = jnp.where(r0, 0.0, yim)
        yre_scr[rows, :] = yre.astype(BF16)
        yim_scr[rows, :] = yim.astype(BF16)

    inv_n = 1.0 / (2 * lx)
    for nc in range(lx // ch):
        rows = slice(nc * ch, (nc + 1) * ch)
        y = (jnp.dot(c_ref[rows, :], yre_scr[...], preferred_element_type=F32)
             - jnp.dot(s_ref[rows, :], yim_scr[...], preferred_element_type=F32))
        rr = lax.broadcasted_iota(jnp.int32, (ch, tc), 0)
        sg = jnp.where((rr & 1) == 0, inv_n, -inv_n)
        y = y * (2.0 * inv_n) + sg * y_nyq
        o_ref[rows, :] = x0_scr[rows, :] * (y + wv_scr[rows, :] * bias_ref[...])


def _hyena(lx, row_block0, p, short_w, short_b, kre, kim, bias, ctab, stab):
    tc = HY_COLS
    nj = HY_CH // tc
    seg = lambda s: pl.BlockSpec((lx, tc), lambda b, j: (row_block0 + b, s * nj + j))
    wseg = lambda s: pl.BlockSpec((SHORT_CONV, tc), lambda b, j: (0, s * nj + j))
    bseg = lambda s: pl.BlockSpec((1, tc), lambda b, j: (0, s * nj + j))
    col = lambda rows: pl.BlockSpec((rows, tc), lambda b, j: (0, j))
    tab = pl.BlockSpec((lx, lx), lambda b, j: (0, 0))
    return pl.pallas_call(
        functools.partial(_hyena_body, lx=lx),
        grid=(BATCH, nj),
        in_specs=[seg(0), seg(1), seg(2), wseg(0), wseg(1), wseg(2), bseg(0), bseg(1), bseg(2),
                  col(lx), col(lx), col(1), tab, tab],
        out_specs=pl.BlockSpec((lx, tc), lambda b, j: (b, j)),
        out_shape=jax.ShapeDtypeStruct((BATCH * lx, HY_CH), F32),
        scratch_shapes=[pltpu.VMEM((lx, tc), F32), pltpu.VMEM((lx, tc), F32), pltpu.VMEM((lx, tc), BF16),
                        pltpu.VMEM((lx, tc), BF16), pltpu.VMEM((lx, tc), BF16)],
        compiler_params=_cparams(("parallel", "parallel")),
        name=f"hyena_{lx}",
    )(p, p, p, short_w, short_w, short_w, short_b, short_b, short_b, kre, kim, bias, ctab, stab)


def _nabias_body(rpb_ref, o_ref, bm_scr):
    rb = pl.program_id(0)
    w = GRID_W

    @pl.when(rb == 0)
    def _():
        qc = lax.broadcasted_iota(jnp.int32, (w, w), 0)
        kc = lax.broadcasted_iota(jnp.int32, (w, w), 1)
        d = jnp.clip(kc - qc + (NA_WIN_COLS - 1), 0, 2 * NA_WIN_COLS - 2)
        c0 = jnp.clip(qc - NA_WIN_COLS // 2, 0, w - NA_WIN_COLS)
        ok = (kc >= c0) & (kc < c0 + NA_WIN_COLS)
        for h in range(NA_HEADS):
            for dr in range(2 * NA_WIN_ROWS - 1):
                acc = jnp.zeros((w, w), F32)
                for j in range(2 * NA_WIN_COLS - 1):
                    acc = jnp.where(d == j, rpb_ref[h, dr, j], acc)
                bm_scr[h, dr] = jnp.where(ok, acc, NEG_INF)

    start = jnp.clip(NA_QROWS * rb - NA_WIN_ROWS // 2, 0, GRID_ROWS - NA_KROWS)
    for h in range(NA_HEADS):
        for i in range(NA_QROWS):
            r = NA_QROWS * rb + i
            r0 = jnp.clip(r - NA_WIN_ROWS // 2, 0, GRID_ROWS - NA_WIN_ROWS)
            for j in range(NA_KROWS):
                kr = start + j
                valid = (kr >= r0) & (kr < r0 + NA_WIN_ROWS)
                didx = jnp.clip(kr - r + (NA_WIN_ROWS - 1), 0, 2 * NA_WIN_ROWS - 2)
                blk = jnp.where(valid, bm_scr[h, didx], NEG_INF)
                o_ref[0, h, i * w:(i + 1) * w, j * w:(j + 1) * w] = blk


def _na_bias(rpb):
    nrb = GRID_ROWS // NA_QROWS
    return pl.pallas_call(
        _nabias_body,
        grid=(nrb,),
        in_specs=[pl.BlockSpec(memory_space=pltpu.SMEM)],
        out_specs=pl.BlockSpec((1, NA_HEADS, NA_QROWS * GRID_W, NA_KROWS * GRID_W), lambda rb: (rb, 0, 0, 0)),
        out_shape=jax.ShapeDtypeStruct((nrb, NA_HEADS, NA_QROWS * GRID_W, NA_KROWS * GRID_W), F32),
        scratch_shapes=[pltpu.VMEM((NA_HEADS, 2 * NA_WIN_ROWS - 1, GRID_W, GRID_W), F32)],
        compiler_params=_cparams(("arbitrary",)),
        name="na_bias",
    )(rpb)


_NT = (((1,), (1,)), ((), ()))


def _na_body(q_ref, k_ref, v_ref, kc_ref, vc_ref, bias_ref, o_ref):
    rb = pl.program_id(1)
    start = jnp.clip(NA_QROWS * rb - NA_WIN_ROWS // 2, 0, GRID_ROWS - NA_KROWS)
    row0 = pl.multiple_of(start * GRID_W, GRID_W)
    nk = NA_KROWS * GRID_W
    scale = HEAD_DIM ** -0.5
    for h in range(NA_HEADS):
        cs = slice(h * HEAD_DIM, (h + 1) * HEAD_DIM)
        q = (q_ref[:, cs].astype(F32) * scale).astype(BF16)
        kw = k_ref[pl.ds(row0, nk), cs].astype(BF16)
        vw = v_ref[pl.ds(row0, nk), cs].astype(BF16)
        s = lax.dot_general(q, kw, _NT, preferred_element_type=F32)
        bm = bias_ref[0, h]
        s = jnp.where(bm > 0.5 * NEG_INF, s + bm, NEG_INF)
        sc = lax.dot_general(q, kc_ref[:, cs].astype(BF16), _NT, preferred_element_type=F32)
        m = jnp.maximum(jnp.max(s, axis=-1, keepdims=True), jnp.max(sc, axis=-1, keepdims=True))
        p = jnp.exp(s - m)
        pc = jnp.exp(sc - m)
        denom = jnp.sum(p, axis=-1, keepdims=True) + jnp.sum(pc, axis=-1, keepdims=True)
        o = (jnp.dot(p.astype(BF16), vw, preferred_element_type=F32)
             + jnp.dot(pc.astype(BF16), vc_ref[:, cs].astype(BF16), preferred_element_type=F32))
        o_ref[:, cs] = o / denom


def _na_latent(p, bias):
    nrb = GRID_ROWS // NA_QROWS
    tq = NA_QROWS * GRID_W
    cq, ck, cv = COL_NQ // NA_DIM, COL_NK // NA_DIM, COL_NV // NA_DIM
    cblk = T_LAT // CTX_LEN
    return pl.pallas_call(
        _na_body,
        grid=(BATCH, nrb),
        in_specs=[pl.BlockSpec((tq, NA_DIM), lambda b, r: (b * nrb + r, cq)),
                  pl.BlockSpec((SEQ, NA_DIM), lambda b, r: (b, ck)),
                  pl.BlockSpec((SEQ, NA_DIM), lambda b, r: (b, cv)),
                  pl.BlockSpec((CTX_LEN, NA_DIM), lambda b, r: (cblk + b, ck)),
                  pl.BlockSpec((CTX_LEN, NA_DIM), lambda b, r: (cblk + b, cv)),
                  pl.BlockSpec((1, NA_HEADS, tq, NA_KROWS * GRID_W), lambda b, r: (r, 0, 0, 0))],
        out_specs=pl.BlockSpec((tq, NA_DIM), lambda b, r: (b * nrb + r, 0)),
        out_shape=jax.ShapeDtypeStruct((T_LAT, NA_DIM), F32),
        compiler_params=_cparams(("parallel", "arbitrary")),
        name="na_latent",
    )(p, p, p, p, p, bias)


def _nactx_body(q_ref, k_ref, v_ref, o_ref):
    scale = HEAD_DIM ** -0.5
    for h in range(NA_HEADS):
        cs = slice(h * HEAD_DIM, (h + 1) * HEAD_DIM)
        q = (q_ref[:, cs].astype(F32) * scale).astype(BF16)
        s = lax.dot_general(q, k_ref[:, cs].astype(BF16), _NT, preferred_element_type=F32)
        m = jnp.max(s, axis=-1, keepdims=True)
        p = jnp.exp(s - m)
        denom = jnp.sum(p, axis=-1, keepdims=True)
        o = jnp.dot(p.astype(BF16), v_ref[:, cs].astype(BF16), preferred_element_type=F32)
        o_ref[:, cs] = o / denom


def _na_context(p):
    cq, ck, cv = COL_NQ // NA_DIM, COL_NK // NA_DIM, COL_NV // NA_DIM
    cblk = T_LAT // CTX_LEN
    spec = lambda c: pl.BlockSpec((CTX_LEN, NA_DIM), lambda b: (cblk + b, c))
    return pl.pallas_call(
        _nactx_body,
        grid=(BATCH,),
        in_specs=[spec(cq), spec(ck), spec(cv)],
        out_specs=pl.BlockSpec((CTX_LEN, NA_DIM), lambda b: (b, 0)),
        out_shape=jax.ShapeDtypeStruct((T_CTX, NA_DIM), F32),
        compiler_params=_cparams(("parallel",)),
        name="na_context",
    )(p, p, p)


_BNT = (((2,), (2,)), ((0,), (0,)))
_BNN = (((2,), (1,)), ((0,), (0,)))


def _rope_tables(lx):
    t = jnp.arange(lx)
    rows = (t // GRID_W).astype(F32)
    cols = (t % GRID_W).astype(F32)
    quarter = GLA_DK // 4
    inv = ROPE_BASE ** (-jnp.arange(quarter, dtype=F32) / quarter)
    ang_r = rows[:, None] * inv[None, :]
    ang_c = cols[:, None] * inv[None, :]
    cos = jnp.concatenate([jnp.cos(ang_r)] * 2 + [jnp.cos(ang_c)] * 2, axis=-1)
    sin = jnp.concatenate([-jnp.sin(ang_r), jnp.sin(ang_r), -jnp.sin(ang_c), jnp.sin(ang_c)], axis=-1)
    return jnp.tile(cos, (1, 2)), jnp.tile(sin, (1, 2))


def _log_sigmoid(z):
    return jnp.minimum(z, 0.0) - jnp.log(1.0 + jnp.exp(-jnp.abs(z)))


def _gla_body(*refs, lx, rope, with_out, with_init):
    it = iter(refs)
    q_ref = next(it) if with_out else None
    k_ref, v_ref = next(it), next(it)
    g_ref = next(it) if with_out else None
    ga_ref, w2_ref, b2_ref = next(it), next(it), next(it)
    cos_ref = sin_ref = None
    if rope:
        cos_ref, sin_ref = next(it), next(it)
    nw_ref = next(it) if with_out else None
    sf0_ref = sb0_ref = None
    if with_init:
        sf0_ref, sb0_ref = next(it), next(it)
    y_ref = next(it) if with_out else None
    sf_ref, sb_ref = next(it), next(it)
    ut_scr, stp_scr, dl_scr = next(it), next(it), next(it)

    c = GLA_CHUNK
    n = lx // c
    lane = lax.broadcasted_iota(jnp.int32, (lx, 2 * GLA_DK), 1)

    def rot(x):
        if not rope:
            return x
        even = ((lane // (GLA_DK // 4)) % 2) == 0
        partner = jnp.where(even, pltpu.roll(x, 2 * GLA_DK - GLA_DK // 4, 1), pltpu.roll(x, GLA_DK // 4, 1))
        return x * cos_ref[...] + partner * sin_ref[...]

    k = rot(k_ref[...].astype(F32))
    q = rot(q_ref[...].astype(F32) * (GLA_DK ** -0.5)) if with_out else None
    v3 = v_ref[...].reshape(n, c, 2 * GLA_DV)

    ga = ga_ref[...]
    log_f = _log_sigmoid(jnp.dot(ga, w2_ref[0], preferred_element_type=F32, precision=HIGHEST)
                         + b2_ref[0]) / GLA_GATE_NORM
    log_b = _log_sigmoid(jnp.dot(ga, w2_ref[1], preferred_element_type=F32, precision=HIGHEST)
                         + b2_ref[1]) / GLA_GATE_NORM
    pos = lax.broadcasted_iota(jnp.int32, (lx, 2 * GLA_DK), 0) % c
    cum_f, cum_b = log_f, log_b
    step = 1
    while step < c:
        cum_f = cum_f + jnp.where(pos >= step, pltpu.roll(cum_f, step, 0), 0.0)
        cum_b = cum_b + jnp.where(pos < c - step, pltpu.roll(cum_b, lx - step, 0), 0.0)
        step *= 2

    ti = lax.broadcasted_iota(jnp.int32, (n, c, c), 1)
    si = lax.broadcasted_iota(jnp.int32, (n, c, c), 2)
    o_acc = [None, None]
    for forward in (True, False):
        cum = cum_f if forward else cum_b
        cum3 = cum.reshape(n, c, 2 * GLA_DK)
        tot = cum3[:, c - 1:c, :] if forward else cum3[:, 0:1, :]
        kinv = (k * jnp.exp(-cum)).reshape(n, c, 2 * GLA_DK)
        kd = (kinv * jnp.exp(tot)).astype(BF16)
        kinv = kinv.astype(BF16)
        dl_scr[...] = jnp.exp(tot)
        qd3 = (q * jnp.exp(cum)).reshape(n, c, 2 * GLA_DK) if with_out else None
        keep = (ti >= si) if forward else (si >= ti)
        for hh in range(2):
            vh = v3[:, :, hh * GLA_DV:(hh + 1) * GLA_DV].astype(BF16)
            vt = jnp.swapaxes(vh, 1, 2)
            ut_scr[...] = lax.dot_general(vt, kd, _BNN, preferred_element_type=F32)
            st0 = (sf0_ref if forward else sb0_ref)[0, hh] if with_init else jnp.zeros((GLA_DV, 2 * GLA_DK), F32)

            def scan(i, st):
                ci = i if forward else n - 1 - i
                stp_scr[ci] = st
                return st * dl_scr[ci] + ut_scr[ci]

            st_fin = lax.fori_loop(0, n, scan, st0)
            (sf_ref if forward else sb_ref)[0, hh] = st_fin
            if with_out:
                in_head = (lax.broadcasted_iota(jnp.int32, (n, c, 2 * GLA_DK), 2) // GLA_DK) == hh
                qm = jnp.where(in_head, qd3, 0.0).astype(BF16)
                att = lax.dot_general(qm, kinv, _BNT, preferred_element_type=F32)
                att = jnp.where(keep, att, 0.0).astype(BF16)
                o = (lax.dot_general(att, vh, _BNN, preferred_element_type=F32)
                     + lax.dot_general(qm, stp_scr[...].astype(BF16), _BNT, preferred_element_type=F32))
                o_acc[hh] = o if o_acc[hh] is None else o_acc[hh] + o

    if with_out:
        for hh in range(2):
            vs = slice(hh * GLA_DV, (hh + 1) * GLA_DV)
            o = _rms_rows(o_acc[hh].reshape(lx, GLA_DV)) * nw_ref[...]
            y_ref[:, vs] = o * _silu(g_ref[:, vs].astype(F32))


def _gla(lx, row_block0, p, p_ga, w2cat, b2, norm_w, rope_tabs, init, with_out):
    npair = GLA_HEADS // 2
    rope = rope_tabs is not None
    with_init = init is not None
    rb = lambda b: row_block0 + b
    qk = lambda col: pl.BlockSpec((lx, 2 * GLA_DK), lambda b, h: (rb(b), col // (2 * GLA_DK) + h))
    vg = lambda col: pl.BlockSpec((lx, 2 * GLA_DV), lambda b, h: (rb(b), col // (2 * GLA_DV) + h))
    st_spec = pl.BlockSpec((1, 2, GLA_DV, 2 * GLA_DK), lambda b, h: (b, h, 0, 0))
    in_specs, args = [], []
    if with_out:
        in_specs.append(qk(COL_GQ)); args.append(p)
    in_specs += [qk(COL_GK), vg(COL_GV)]; args += [p, p]
    if with_out:
        in_specs.append(vg(COL_GG)); args.append(p)
    in_specs += [pl.BlockSpec((lx, 2 * GLA_RANK), lambda b, h: (rb(b), 0)),
                 pl.BlockSpec((2, 2 * GLA_RANK, 2 * GLA_DK), lambda b, h: (0, 0, h)),
                 pl.BlockSpec((2, 1, 2 * GLA_DK), lambda b, h: (0, 0, h))]
    args += [p_ga, w2cat, b2]
    if rope:
        in_specs += [pl.BlockSpec((lx, 2 * GLA_DK), lambda b, h: (0, 0))] * 2
        args += list(rope_tabs)
    if with_out:
        in_specs.append(pl.BlockSpec((1, GLA_DV), lambda b, h: (0, 0))); args.append(norm_w)
    if with_init:
        in_specs += [st_spec, st_spec]; args += list(init)
    st_shape = jax.ShapeDtypeStruct((BATCH, GLA_HEADS, GLA_DV, 2 * GLA_DK), F32)
    out_specs, out_shape = [], []
    if with_out:
        out_specs.append(pl.BlockSpec((lx, 2 * GLA_DV), lambda b, h: (b, h)))
        out_shape.append(jax.ShapeDtypeStruct((BATCH * lx, GLA_V), F32))
    out_specs += [st_spec, st_spec]
    out_shape += [st_shape, st_shape]
    n = lx // GLA_CHUNK
    outs = pl.pallas_call(
        functools.partial(_gla_body, lx=lx, rope=rope, with_out=with_out, with_init=with_init),
        grid=(BATCH, npair),
        in_specs=in_specs, out_specs=out_specs, out_shape=out_shape,
        scratch_shapes=[pltpu.VMEM((n, GLA_DV, 2 * GLA_DK), F32), pltpu.VMEM((n, GLA_DV, 2 * GLA_DK), F32),
                        pltpu.VMEM((n, 1, 2 * GLA_DK), F32)],
        compiler_params=_cparams(("parallel", "parallel")),
        name=f"gla_{lx}",
    )(*args)
    if with_out:
        return outs[0], outs[1], outs[2]
    return None, outs[0], outs[1]


def _outproj_body(*refs, with_ctx):
    it = iter(refs)
    lat = [next(it) for _ in range(3)]
    ctx = [next(it) for _ in range(3)] if with_ctx else None
    x_ref, mod_ref, hyw_ref, naw_ref, w_ref, g_ref, b_ref, o_ref = (next(it) for _ in range(8))

    def pick(j):
        if not with_ctx:
            return lat[j][...]
        return jnp.where(pl.program_id(0) >= T_LAT // ROW_TILE, ctx[j][...], lat[j][...])

    ya = (_rms_rows(pick(0)) * hyw_ref[...]).astype(BF16)
    yb = (_rms_rows(pick(1)) * naw_ref[...]).astype(BF16)
    yc = pick(2).astype(BF16)
    y = (jnp.dot(ya, w_ref[0:HY_CH, :], preferred_element_type=F32)
         + jnp.dot(yb, w_ref[HY_CH:HY_CH + NA_DIM, :], preferred_element_type=F32)
         + jnp.dot(yc, w_ref[HY_CH + NA_DIM:, :], preferred_element_type=F32))
    r = ALPHA * x_ref[...] + mod_ref[0, 2:3, :] * y
    o_ref[...] = _layer_norm_rows(r) * g_ref[...] + b_ref[...]


def _outproj(rows, lat, ctx, xa, mod, hy_norm_w, na_norm_w, w_out_bf, ln_g, ln_b):
    tm = ROW_TILE
    with_ctx = ctx is not None
    nlat = T_LAT // tm
    mrow = _mod_row(tm)
    widths = (HY_CH, NA_DIM, GLA_V)
    in_specs = [pl.BlockSpec((tm, w), lambda i: (jnp.minimum(i, nlat - 1), 0)) for w in widths]
    args = list(lat)
    if with_ctx:
        in_specs += [pl.BlockSpec((tm, w), lambda i: (jnp.maximum(i - nlat, 0), 0)) for w in widths]
        args += list(ctx)
    vec = lambda w: pl.BlockSpec((1, w), lambda i: (0, 0))
    in_specs += [pl.BlockSpec((tm, D_MODEL), lambda i: (i, 0)),
                 pl.BlockSpec((1, 6, D_MODEL), lambda i: (mrow(i), 0, 0)),
                 vec(HY_CH), vec(NA_DIM),
                 pl.BlockSpec((D_MODEL, D_MODEL), lambda i: (0, 0)),
                 vec(D_MODEL), vec(D_MODEL)]
    args += [xa, mod, hy_norm_w, na_norm_w, w_out_bf, ln_g, ln_b]
    return pl.pallas_call(
        functools.partial(_outproj_body, with_ctx=with_ctx),
        grid=(rows // tm,),
        in_specs=in_specs,
        out_specs=pl.BlockSpec((tm, D_MODEL), lambda i: (i, 0)),
        out_shape=jax.ShapeDtypeStruct((rows, D_MODEL), F32),
        compiler_params=_cparams(("parallel",)),
        name=f"outproj_{rows}",
    )(*args)


ROUTER_ROWS = 8 + N_EXPERTS


def _router_body(x_ref, mod_ref, w_ref, b_ref, h_ref, eid_ref, gw_ref):
    tm = ROW_TILE
    h2 = _layer_norm_rows(x_ref[...]) * (1.0 + mod_ref[0, 4:5, :]) + mod_ref[0, 3:4, :]
    h_ref[...] = h2
    logits = lax.dot_general(w_ref[...], h2, _NT, preferred_element_type=F32, precision=HIGHEST) + b_ref[...]
    lg = logits[0:N_GROUPS, :]
    eg = jnp.exp(lg - jnp.max(lg, axis=0, keepdims=True))
    pg = eg / jnp.sum(eg, axis=0, keepdims=True)
    top_pg = jnp.max(pg, axis=0, keepdims=True)
    gio = lax.broadcasted_iota(jnp.int32, (N_GROUPS, tm), 0)
    gi = jnp.min(jnp.where(pg == top_pg, gio, N_GROUPS), axis=0, keepdims=True)
    le = logits[8:8 + N_EXPERTS, :].reshape(N_GROUPS, EXPERTS_PER_GROUP, tm)
    gsel = lax.broadcasted_iota(jnp.int32, (N_GROUPS, EXPERTS_PER_GROUP, tm), 0) == gi[None]
    les = jnp.sum(jnp.where(gsel, le, 0.0), axis=0)
    eio = lax.broadcasted_iota(jnp.int32, (EXPERTS_PER_GROUP, tm), 0)
    v1 = jnp.max(les, axis=0, keepdims=True)
    i1 = jnp.min(jnp.where(les == v1, eio, EXPERTS_PER_GROUP), axis=0, keepdims=True)
    les2 = jnp.where(eio == i1, -jnp.inf, les)
    v2 = jnp.max(les2, axis=0, keepdims=True)
    i2 = jnp.min(jnp.where(les2 == v2, eio, EXPERTS_PER_GROUP), axis=0, keepdims=True)
    t = jnp.exp(v2 - v1)
    w1 = top_pg / (1.0 + t)
    eid_ref[...] = jnp.concatenate([gi * EXPERTS_PER_GROUP + i1, gi * EXPERTS_PER_GROUP + i2], axis=0)
    gw_ref[...] = jnp.concatenate([w1, w1 * t], axis=0)


def _router(rows, x1, mod, w_rt, b_rt):
    tm = ROW_TILE
    mrow = _mod_row(tm)
    return pl.pallas_call(
        _router_body,
        grid=(rows // tm,),
        in_specs=[pl.BlockSpec((tm, D_MODEL), lambda i: (i, 0)),
                  pl.BlockSpec((1, 6, D_MODEL), lambda i: (mrow(i), 0, 0)),
                  pl.BlockSpec((ROUTER_ROWS, D_MODEL), lambda i: (0, 0)),
                  pl.BlockSpec((ROUTER_ROWS, 1), lambda i: (0, 0))],
        out_specs=[pl.BlockSpec((tm, D_MODEL), lambda i: (i, 0)),
                   pl.BlockSpec((2, tm), lambda i: (0, i)),
                   pl.BlockSpec((2, tm), lambda i: (0, i))],
        out_shape=[jax.ShapeDtypeStruct((rows, D_MODEL), F32),
                   jax.ShapeDtypeStruct((2, rows), jnp.int32),
                   jax.ShapeDtypeStruct((2, rows), F32)],
        compiler_params=_cparams(("parallel",)),
        name=f"router_{rows}",
    )(x1, mod, w_rt, b_rt)


def _moe_tiles(rows):
    return -(-(2 * rows + N_EXPERTS * (MOE_TILE - 1)) // MOE_TILE)


def _dispatch_plan(rows, eid):
    nt = _moe_tiles(rows)
    e = eid.reshape(-1)
    experts = jnp.arange(N_EXPERTS, dtype=jnp.int32)
    onehot = (e[:, None] == experts[None, :]).astype(jnp.int32)
    csum = jnp.cumsum(onehot, axis=0)
    rank = jnp.sum(csum * onehot, axis=1) - 1
    counts = csum[-1]
    padded = ((counts + MOE_TILE - 1) // MOE_TILE) * MOE_TILE
    pend = jnp.cumsum(padded)
    pstart = pend - padded
    dest = jnp.sum(onehot * pstart[None, :], axis=1) + rank
    tile_start = jnp.arange(nt, dtype=jnp.int32) * MOE_TILE
    tile_valid = (tile_start < pend[-1]).astype(jnp.int32)
    tile_expert = jnp.sum((pend[None, :] <= tile_start[:, None]).astype(jnp.int32), axis=1)
    last_expert = jnp.max(jnp.where(counts > 0, experts, 0))
    tile_expert = jnp.minimum(tile_expert, last_expert)
    first = jnp.sum(((pstart[None, :] == tile_start[:, None]) & (padded[None, :] > 0)).astype(jnp.int32), axis=1)
    tile_first = jnp.minimum(first, 1) * tile_valid
    return dict(dest=dest, pend=pend, counts=counts, tile_expert=tile_expert, tile_valid=tile_valid,
                tile_first=tile_first, n_valid=(pend[-1] // MOE_TILE).reshape(1))


def _row_copy(src, src_row, dst, dst_row, sem):
    return pltpu.make_async_copy(src.at[pl.ds(src_row, 1), :], dst.at[pl.ds(dst_row, 1), :], sem)


def _invmap_body(dest_ref, tok_ref, *, rows):
    def clear(j, carry):
        tok_ref[j] = 0
        return carry

    lax.fori_loop(0, tok_ref.shape[0], clear, 0, unroll=8)

    def fill(a, carry):
        tok_ref[dest_ref[a]] = jnp.where(a >= rows, a - rows, a)
        return carry

    lax.fori_loop(0, 2 * rows, fill, 0, unroll=8)


def _invmap(rows, dest):
    return pl.pallas_call(
        functools.partial(_invmap_body, rows=rows),
        in_specs=[pl.BlockSpec(memory_space=pltpu.SMEM)],
        out_specs=pl.BlockSpec(memory_space=pltpu.SMEM),
        out_shape=jax.ShapeDtypeStruct((_moe_tiles(rows) * MOE_TILE,), jnp.int32),
        name=f"moe_invmap_{rows}",
    )(dest)


def _moe_body(te_ref, tv_ref, tf_ref, nv_ref, tok_ref, h_hbm, wup_ref, wdn_ref, o_ref, x_scr, wup_bf, wdn_bf, sem):
    i = pl.program_id(0)
    tm = MOE_TILE

    def gather(tile, slot):
        def start(r, carry):
            _row_copy(h_hbm, tok_ref[tile * tm + r], x_scr.at[slot], r, sem.at[slot]).start()
            return carry

        lax.fori_loop(0, tm, start, 0, unroll=8)

    @pl.when(i == 0)
    def _():
        gather(0, 0)

    for slot in range(2):
        @pl.when((tv_ref[i] == 1) & (i % 2 == slot))
        def _():
            @pl.when(i + 1 < nv_ref[0])
            def _():
                gather(i + 1, 1 - slot)

            pltpu.make_async_copy(h_hbm.at[pl.ds(0, tm), :], x_scr.at[slot], sem.at[slot]).wait()

            @pl.when(tf_ref[i] == 1)
            def _():
                ch = 256
                for r in range(D_MODEL // ch):
                    wup_bf[r * ch:(r + 1) * ch, :] = wup_ref[r * ch:(r + 1) * ch, :].astype(BF16)
                for r in range(D_EXPERT // ch):
                    wdn_bf[r * ch:(r + 1) * ch, :] = wdn_ref[r * ch:(r + 1) * ch, :].astype(BF16)

            gu = jnp.dot(x_scr[slot].astype(BF16), wup_bf[...], preferred_element_type=F32)
            act = _silu(gu[:, :D_EXPERT]) * gu[:, D_EXPERT:]
            o_ref[...] = jnp.dot(act.astype(BF16), wdn_bf[...], preferred_element_type=F32)

    @pl.when(tv_ref[i] == 0)
    def _():
        o_ref[...] = jnp.zeros_like(o_ref)


def _moe_ffn(rows, layer, plan, row_token, h2, w_up, w_down):
    nt = _moe_tiles(rows)
    tm = MOE_TILE

    def wmap(i, te, tv, tf, nv, tok):
        return (layer, te[i] // EXPERTS_PER_GROUP, te[i] % EXPERTS_PER_GROUP, 0, 0)

    grid_spec = pltpu.PrefetchScalarGridSpec(
        num_scalar_prefetch=5,
        grid=(nt,),
        in_specs=[pl.BlockSpec(memory_space=pl.ANY),
                  pl.BlockSpec((None, None, None, D_MODEL, 2 * D_EXPERT), wmap),
                  pl.BlockSpec((None, None, None, D_EXPERT, D_MODEL), wmap)],
        out_specs=pl.BlockSpec((tm, D_MODEL), lambda i, te, tv, tf, nv, tok: (i, 0)),
        scratch_shapes=[pltpu.VMEM((2, tm, D_MODEL), F32), pltpu.VMEM((D_MODEL, 2 * D_EXPERT), BF16),
                        pltpu.VMEM((D_EXPERT, D_MODEL), BF16), pltpu.SemaphoreType.DMA((2,))],
    )
    return pl.pallas_call(
        _moe_body,
        grid_spec=grid_spec,
        out_shape=jax.ShapeDtypeStruct((nt * tm, D_MODEL), F32),
        compiler_params=_cparams(("arbitrary",)),
        name=f"moe_ffn_{rows}",
    )(plan["tile_expert"], plan["tile_valid"], plan["tile_first"], plan["n_valid"], row_token, h2, w_up, w_down)


def _combine_body(pos_ref, y_hbm, gw_ref, x_ref, mod_ref, g_ref, b_ref, o_ref, y_scr, sem, *, rows):
    i = pl.program_id(0)
    tm = ROW_TILE

    def gather(tile, buf):
        def start(r, carry):
            for s in range(2):
                _row_copy(y_hbm, pos_ref[s * rows + tile * tm + r], y_scr.at[buf, s], r, sem.at[buf]).start()
            return carry

        lax.fori_loop(0, tm, start, 0, unroll=8)

    @pl.when(i == 0)
    def _():
        gather(0, 0)

    for buf in range(2):
        @pl.when(i % 2 == buf)
        def _():
            @pl.when(i + 1 < pl.num_programs(0))
            def _():
                gather(i + 1, 1 - buf)

            for s in range(2):
                pltpu.make_async_copy(y_hbm.at[pl.ds(0, tm), :], y_scr.at[buf, s], sem.at[buf]).wait()
            f = y_scr[buf, 0] * gw_ref[:, 0:1] + y_scr[buf, 1] * gw_ref[:, 1:2]
            r = ALPHA * x_ref[...] + mod_ref[0, 5:6, :] * f
            o_ref[...] = _layer_norm_rows(r) * g_ref[...] + b_ref[...]


def _combine(rows, pos, ysort, gw_t, x1, mod, ln_g, ln_b):
    tm = ROW_TILE
    mrow = _mod_row(tm)
    vec = pl.BlockSpec((1, D_MODEL), lambda i, pos: (0, 0))
    grid_spec = pltpu.PrefetchScalarGridSpec(
        num_scalar_prefetch=1,
        grid=(rows // tm,),
        in_specs=[pl.BlockSpec(memory_space=pl.ANY),
                  pl.BlockSpec((tm, 2), lambda i, pos: (i, 0)),
                  pl.BlockSpec((tm, D_MODEL), lambda i, pos: (i, 0)),
                  pl.BlockSpec((1, 6, D_MODEL), lambda i, pos: (mrow(i), 0, 0)),
                  vec, vec],
        out_specs=pl.BlockSpec((tm, D_MODEL), lambda i, pos: (i, 0)),
        scratch_shapes=[pltpu.VMEM((2, 2, tm, D_MODEL), F32), pltpu.SemaphoreType.DMA((2,))],
    )
    return pl.pallas_call(
        functools.partial(_combine_body, rows=rows),
        grid_spec=grid_spec,
        out_shape=jax.ShapeDtypeStruct((rows, D_MODEL), F32),
        compiler_params=_cparams(("arbitrary",)),
        name=f"combine_{rows}",
    )(pos, ysort, gw_t, x1, mod, ln_g, ln_b)


def _layer(xa, c8, tabs, w, big, layer, last):
    row = lambda v: v.reshape(1, -1)
    mod = _ada(c8, layer, big["w_ada"], w["b_ada"]).reshape(8, 6, D_MODEL)
    p, p_ga = _inproj(xa, mod, w["w_in"][:, :P_COLS].astype(BF16), w["w_in"][:, P_COLS:].astype(BF16))

    w1p = jnp.pad(w["hy_f_w1"], ((0, 128 - w["hy_f_w1"].shape[0]), (0, 0)))
    filt_args = (w1p, row(w["hy_f_b1"]), w["hy_f_w2"], row(w["hy_f_b2"]), w["hy_f_w3"], row(w["hy_sin_freq"]))
    short_b, hy_bias = row(w["hy_short_b"]), row(w["hy_bias"])
    kre, kim = _hyena_filter(SEQ, tabs["z_lat"], tabs["win_lat"], tabs["c_lat"], tabs["s_lat"], *filt_args)
    ya = _hyena(SEQ, 0, p, w["hy_short_w"], short_b, kre, kim, hy_bias, tabs["c_lat"], tabs["s_lat"])

    yb = _na_latent(p, _na_bias(w["na_rpb"]))

    zpad = jnp.zeros((GLA_RANK, GLA_K), F32)
    w2cat = jnp.stack([jnp.concatenate([w["gla_a_w2"][0], zpad], axis=0),
                       jnp.concatenate([zpad, w["gla_a_w2"][1]], axis=0)])
    b2 = w["gla_a_b"].reshape(2, 1, GLA_K)
    nw = row(w["gla_norm_w"])
    ctx_blk = T_LAT // CTX_LEN
    yc_ctx, sf, sb = _gla(CTX_LEN, ctx_blk, p, p_ga, w2cat, b2, nw, None, None, with_out=True)
    yc, _, _ = _gla(SEQ, 0, p, p_ga, w2cat, b2, nw, tabs["rope"], (sf, sb), with_out=True)

    lat = (ya, yb, yc)
    ctx = None
    if not last:
        kre_c, kim_c = _hyena_filter(CTX_LEN, tabs["z_ctx"], tabs["win_ctx"], tabs["c_ctx"], tabs["s_ctx"], *filt_args)
        ya_c = _hyena(CTX_LEN, ctx_blk, p, w["hy_short_w"], short_b, kre_c, kim_c, hy_bias,
                      tabs["c_ctx"], tabs["s_ctx"])
        ctx = (ya_c, _na_context(p), yc_ctx)

    rows = T_LAT if last else T_ALL
    x1 = _outproj(rows, lat, ctx, xa, mod, row(w["hy_norm_w"]), row(w["na_norm_w"]),
                  w["w_out"].astype(BF16), row(w["ln1_g"]), row(w["ln1_b"]))

    pad4 = jnp.zeros((8 - N_GROUPS, D_MODEL), F32)
    w_rt = jnp.concatenate([w["w_rg"].T, pad4, w["w_re"].T], axis=0)
    b_rt = jnp.concatenate([w["b_rg"], jnp.zeros((8 - N_GROUPS,), F32), w["b_re"]]).reshape(ROUTER_ROWS, 1)
    h2, eid, gw = _router(rows, x1, mod, w_rt, b_rt)
    plan = _dispatch_plan(rows, eid)
    ysort = _moe_ffn(rows, layer, plan, _invmap(rows, plan["dest"]), h2, big["w_up"], big["w_down"])
    return _combine(rows, plan["dest"], ysort, gw.T, x1, mod, row(w["ln2_g"]), row(w["ln2_b"]))


_BIG_WEIGHTS = ("w_ada", "w_up", "w_down")
_LAYER_WEIGHTS = ("w_ada", "b_ada", "w_in", "hy_short_w", "hy_short_b", "hy_f_w1", "hy_f_b1", "hy_f_w2", "hy_f_b2",
                  "hy_f_w3", "hy_sin_freq", "hy_bias", "hy_norm_w", "na_rpb", "na_norm_w", "gla_a_w2", "gla_a_b",
                  "gla_norm_w", "w_out", "ln1_g", "ln1_b", "w_rg", "b_rg", "w_re", "b_re", "w_up", "w_down",
                  "ln2_g", "ln2_b")


def kernel(x, c, ctx, c_ctx, w_ada, b_ada, w_in, hy_short_w, hy_short_b, hy_f_w1, hy_f_b1, hy_f_w2, hy_f_b2, hy_f_w3, hy_sin_freq, hy_bias, hy_norm_w, na_rpb, na_norm_w, gla_a_w2, gla_a_b, gla_norm_w, w_out, ln1_g, ln1_b, w_rg, b_rg, w_re, b_re, w_up, w_down, ln2_g, ln2_b):
    stacked = dict(zip(_LAYER_WEIGHTS, (w_ada, b_ada, w_in, hy_short_w, hy_short_b, hy_f_w1, hy_f_b1, hy_f_w2, hy_f_b2,
                                        hy_f_w3, hy_sin_freq, hy_bias, hy_norm_w, na_rpb, na_norm_w, gla_a_w2,
                                        gla_a_b, gla_norm_w, w_out, ln1_g, ln1_b, w_rg, b_rg, w_re, b_re, w_up,
                                        w_down, ln2_g, ln2_b)))
    tabs = {}
    tabs["c_lat"], tabs["s_lat"] = _dft_tables(SEQ)
    tabs["c_ctx"], tabs["s_ctx"] = _dft_tables(CTX_LEN)
    tabs["z_lat"], tabs["win_lat"] = _hyena_features(SEQ)
    tabs["z_ctx"], tabs["win_ctx"] = _hyena_features(CTX_LEN)
    tabs["rope"] = _rope_tables(SEQ)

    xa = jnp.concatenate([x.reshape(T_LAT, D_MODEL), ctx.reshape(T_CTX, D_MODEL)], axis=0)
    c8 = jnp.concatenate([c, c_ctx[None, :], jnp.zeros((8 - BATCH - 1, D_MODEL), F32)], axis=0)
    big = {k: stacked[k] for k in _BIG_WEIGHTS}
    for i in range(DEPTH):
        small = {k: v[i] for k, v in stacked.items() if k not in _BIG_WEIGHTS}
        xa = _layer(xa, c8, tabs, small, big, i, last=(i == DEPTH - 1))
    return xa.reshape(BATCH, SEQ, D_MODEL)
```

```python
import functools
import math

import jax
import jax.numpy as jnp
import numpy as np
from jax import lax
from jax.experimental import pallas as pl
from jax.experimental.pallas import tpu as pltpu

F32 = jnp.float32
BF16 = jnp.bfloat16
HIGHEST = lax.Precision.HIGHEST

D_MODEL = 2048
BATCH = 4
SEQ = 2048
DEPTH = 2
CTX_LEN = 256
GRID_W = 64
HEAD_DIM = 128
HY_CH = 512
NA_HEADS = 6
NA_DIM = 768
GLA_HEADS = 6
GLA_DK = 64
GLA_DV = 128
GLA_K = 384
GLA_V = 768
GLA_RANK = 16
GLA_GATE_NORM = 16.0
GLA_CHUNK = 64
ROPE_BASE = 10000.0
SHORT_CONV = 3
HY_EMB_BANDS = 16
HY_FILTER_WIDTH = 64
HY_MIN_DECAY = math.log(1e-2) / 1.5
HY_MAX_DECAY = math.log(1e-2) / 0.3
NA_WIN_ROWS = 8
NA_WIN_COLS = 16
NEG_INF = -1e30
N_GROUPS = 4
EXPERTS_PER_GROUP = 8
N_EXPERTS = N_GROUPS * EXPERTS_PER_GROUP
D_EXPERT = 512
ALPHA = (2 * DEPTH) ** 0.25
LN_EPS = 1e-6

T_LAT = BATCH * SEQ
T_CTX = BATCH * CTX_LEN
T_ALL = T_LAT + T_CTX
GRID_ROWS = SEQ // GRID_W

COL_HY = 0
COL_NQ = 3 * HY_CH
COL_NK = COL_NQ + NA_DIM
COL_NV = COL_NK + NA_DIM
COL_GQ = COL_NV + NA_DIM
COL_GK = COL_GQ + GLA_K
COL_GV = COL_GK + GLA_K
COL_GG = COL_GV + GLA_V
COL_GA = COL_GG + GLA_V
P_COLS = COL_GA

VMEM_LIMIT_BYTES = 56 * 1024 * 1024
ROW_TILE = 256
INPROJ_ROWS = 1024
INPROJ_COLS = 512
INPROJ_CHUNK = 128
HY_COLS = 256
DFT_CHUNK = 512
NA_QROWS = 4
NA_KROWS = NA_QROWS + NA_WIN_ROWS - 1
MOE_TILE = 256
LANES = 128
SLAB = D_MODEL // LANES


def _to_slabs(ref, x):
    n = x.shape[0]
    for c in range(SLAB):
        ref[pl.ds(c, n, stride=SLAB), :] = x[:, c * LANES:(c + 1) * LANES]


def _from_slabs(ref, n):
    return jnp.concatenate([ref[pl.ds(c, n, stride=SLAB), :] for c in range(SLAB)], axis=1)


def _cparams(sem, vmem=VMEM_LIMIT_BYTES):
    return pltpu.CompilerParams(dimension_semantics=sem, vmem_limit_bytes=vmem)


def _layer_norm_rows(x):
    mu = jnp.mean(x, axis=-1, keepdims=True)
    xc = x - mu
    return xc * lax.rsqrt(jnp.mean(xc * xc, axis=-1, keepdims=True) + LN_EPS)


def _rms_rows(x):
    return x * lax.rsqrt(jnp.mean(x * x, axis=-1, keepdims=True) + LN_EPS)


def _silu(x):
    return x / (1.0 + jnp.exp(-x))


def _mod_row(tile_rows):
    per_batch = SEQ // tile_rows
    return lambda i: jnp.minimum(i // per_batch, BATCH)


def _ada_body(c_ref, w_ref, b_ref, o_ref):
    s = _silu(c_ref[...])
    o_ref[...] = jnp.dot(s, w_ref[...], preferred_element_type=F32, precision=HIGHEST) + b_ref[...]


def _ada(c8, layer, w_ada, b_ada):
    tn = 1024
    n = 6 * D_MODEL
    return pl.pallas_call(
        _ada_body,
        grid=(n // tn,),
        in_specs=[pl.BlockSpec((8, D_MODEL), lambda j: (0, 0)),
                  pl.BlockSpec((None, D_MODEL, tn), lambda j: (layer, 0, j)),
                  pl.BlockSpec((1, tn), lambda j: (0, j))],
        out_specs=pl.BlockSpec((8, tn), lambda j: (0, j)),
        out_shape=jax.ShapeDtypeStruct((8, n), F32),
        compiler_params=_cparams(("parallel",)),
        name="ada_mod",
    )(c8, w_ada, b_ada.reshape(1, n))


def _inproj_body(x_ref, mod_ref, w_ref, wga_ref, p_ref, ga_ref, h_scr):
    @pl.when(pl.program_id(1) == 0)
    def _():
        sh = mod_ref[0, 0:1, :]
        sc = mod_ref[0, 1:2, :]

        def chunk(r, carry):
            rows = pl.ds(pl.multiple_of(r * INPROJ_CHUNK, INPROJ_CHUNK), INPROJ_CHUNK)
            h = _layer_norm_rows(x_ref[rows, :]) * (1.0 + sc) + sh
            h_scr[rows, :] = h.astype(BF16)
            return carry

        lax.fori_loop(0, INPROJ_ROWS // INPROJ_CHUNK, chunk, 0)
        ga_ref[...] = jnp.dot(h_scr[...], wga_ref[...], preferred_element_type=F32)

    p_ref[...] = jnp.dot(h_scr[...], w_ref[...], preferred_element_type=F32).astype(p_ref.dtype)


def _inproj(xa, mod, w_in_bf, w_ga_bf):
    tm, tn = INPROJ_ROWS, INPROJ_COLS
    mrow = _mod_row(tm)
    return pl.pallas_call(
        _inproj_body,
        grid=(T_ALL // tm, P_COLS // tn),
        in_specs=[pl.BlockSpec((tm, D_MODEL), lambda i, j: (i, 0)),
                  pl.BlockSpec((1, 6, D_MODEL), lambda i, j: (mrow(i), 0, 0)),
                  pl.BlockSpec((D_MODEL, tn), lambda i, j: (0, j)),
                  pl.BlockSpec((D_MODEL, 2 * GLA_RANK), lambda i, j: (0, 0))],
        out_specs=[pl.BlockSpec((tm, tn), lambda i, j: (i, j)),
                   pl.BlockSpec((tm, 2 * GLA_RANK), lambda i, j: (i, 0))],
        out_shape=[jax.ShapeDtypeStruct((T_ALL, P_COLS), BF16),
                   jax.ShapeDtypeStruct((T_ALL, 2 * GLA_RANK), F32)],
        scratch_shapes=[pltpu.VMEM((tm, D_MODEL), BF16)],
        compiler_params=_cparams(("parallel", "arbitrary")),
        name="inproj",
    )(xa, mod, w_in_bf, w_ga_bf)


def _dft_tables(lx):
    n = 2 * lx
    k = jnp.arange(lx, dtype=jnp.int32)
    idx = (k[:, None] * k[None, :]) % n
    ang = idx.astype(F32) * (2.0 * math.pi / n)
    return jnp.cos(ang).astype(BF16), jnp.sin(ang).astype(BF16)


def _hyena_features(lx):
    n = jnp.arange(lx, dtype=F32)
    t = n / max(lx - 1, 1)
    bands = jnp.linspace(1e-4, HY_EMB_BANDS - 1, HY_EMB_BANDS, dtype=F32)
    ang = (2.0 * math.pi / lx) * n[:, None] * bands[None, :]
    z = jnp.concatenate([t[:, None], jnp.cos(ang), -jnp.sin(ang)], axis=-1)
    z = jnp.pad(z, ((0, 0), (0, 128 - z.shape[1])))
    deltas = jnp.abs(jnp.linspace(HY_MIN_DECAY, HY_MAX_DECAY, HY_CH, dtype=F32))
    win = jnp.exp(-t[:, None] * deltas[None, :])
    return z, win


def _split_dot(tab, x):
    xh = x.astype(BF16)
    xl = (x - xh.astype(F32)).astype(BF16)
    return (jnp.dot(tab, xh, preferred_element_type=F32) + jnp.dot(tab, xl, preferred_element_type=F32))


def _hyfilt_body(z_ref, w1_ref, b1_ref, w2_ref, b2_ref, w3f_ref, w3b_ref, fr_ref, win_ref, c_ref, s_ref,
                 kre_ref, kim_ref):
    fr = fr_ref[...]
    hid = jnp.sin(fr * (jnp.dot(z_ref[...], w1_ref[...], preferred_element_type=F32, precision=HIGHEST) + b1_ref[...]))
    hid = jnp.sin(fr * (jnp.dot(hid, w2_ref[...], preferred_element_type=F32, precision=HIGHEST) + b2_ref[...]))
    win = win_ref[...]
    hf = jnp.dot(hid, w3f_ref[...], preferred_element_type=F32, precision=HIGHEST) * win
    hb = jnp.dot(hid, w3b_ref[...], preferred_element_type=F32, precision=HIGHEST) * win
    row = lax.broadcasted_iota(jnp.int32, hf.shape, 0)
    hb = jnp.where(row == 0, 0.0, hb)
    hs = hf + hb
    hd = hf - hb
    sgn = jnp.where((row & 1) == 0, 1.0, -1.0)
    k_nyq = jnp.sum(hs * sgn, axis=0, keepdims=True)
    kre_ref[...] = _split_dot(c_ref[...], hs)
    kim = -_split_dot(s_ref[...], hd)
    kim_ref[...] = jnp.where(row == 0, k_nyq, kim)


def _hyena_filter(lx, z, win, ctab, stab, w1p, b1, w2, b2, w3, fr):
    tc = HY_COLS
    nj = HY_CH // tc
    full = lambda shape: pl.BlockSpec(shape, lambda j: (0,) * len(shape))
    return pl.pallas_call(
        _hyfilt_body,
        grid=(nj,),
        in_specs=[full((lx, 128)), full((128, HY_FILTER_WIDTH)), full((1, HY_FILTER_WIDTH)),
                  full((HY_FILTER_WIDTH, HY_FILTER_WIDTH)), full((1, HY_FILTER_WIDTH)),
                  pl.BlockSpec((HY_FILTER_WIDTH, tc), lambda j: (0, j)),
                  pl.BlockSpec((HY_FILTER_WIDTH, tc), lambda j: (0, nj + j)),
                  full((1, HY_FILTER_WIDTH)),
                  pl.BlockSpec((lx, tc), lambda j: (0, j)),
                  full((lx, lx)), full((lx, lx))],
        out_specs=[pl.BlockSpec((lx, tc), lambda j: (0, j)), pl.BlockSpec((lx, tc), lambda j: (0, j))],
        out_shape=[jax.ShapeDtypeStruct((lx, HY_CH), F32), jax.ShapeDtypeStruct((lx, HY_CH), F32)],
        compiler_params=_cparams(("arbitrary",)),
        name=f"hyena_filter_{lx}",
    )(z, w1p, b1, w2, b2, w3, w3, fr, win, ctab, stab)


def _hyena_body(x0_ref, x1_ref, v_ref, w0_ref, w1_ref, w2_ref, b0_ref, b1_ref, b2_ref, kre_ref, kim_ref,
                bias_ref, c_ref, s_ref, o_ref, x0_scr, wv_scr, wb_scr, yre_scr, yim_scr, *, lx):
    tc = HY_COLS
    ch = min(DFT_CHUNK, lx)
    row = lax.broadcasted_iota(jnp.int32, (lx, tc), 0)
    first = row == 0
    last = row == lx - 1

    def short_conv(x_ref, w_ref, b_ref):
        x = x_ref[...].astype(F32)
        xm = jnp.where(first, 0.0, pltpu.roll(x, 1, 0))
        xp = jnp.where(last, 0.0, pltpu.roll(x, lx - 1, 0))
        return xm * w_ref[0:1, :] + x * w_ref[1:2, :] + xp * w_ref[2:3, :] + b_ref[...]

    x0_scr[...] = short_conv(x0_ref, w0_ref, b0_ref)
    wv = short_conv(x1_ref, w1_ref, b1_ref) * short_conv(v_ref, w2_ref, b2_ref)
    wv_scr[...] = wv
    wb_scr[...] = wv.astype(BF16)
    sgn = jnp.where((row & 1) == 0, 1.0, -1.0)
    u_nyq = jnp.sum(wv * sgn, axis=0, keepdims=True)
    y_nyq = u_nyq * kim_ref[0:1, :]

    for kc in range(lx // ch):
        rows = slice(kc * ch, (kc + 1) * ch)
        wb = wb_scr[...]
        ure = jnp.dot(c_ref[rows, :], wb, preferred_element_type=F32)
        uim = -jnp.dot(s_ref[rows, :], wb, preferred_element_type=F32)
        kre = kre_ref[rows, :]
        kim = kim_ref[rows, :]
        yre = ure * kre - uim * kim
        yim = ure * kim + uim * kre
        if kc == 0:
            r0 = lax.broadcasted_iota(jnp.int32, (ch, tc), 0) == 0
            yre = jnp.where(r0, 0.5 * ure * kre, yre)
            yim = jnp.where(r0, 0.0, yim)
        yre_scr[rows, :] = yre.astype(BF16)
        yim_scr[rows, :] = yim.astype(BF16)

    inv_n = 1.0 / (2 * lx)
    for nc in range(lx // ch):
        rows = slice(nc * ch, (nc + 1) * ch)
        y = (jnp.dot(c_ref[rows, :], yre_scr[...], preferred_element_type=F32)
             - jnp.dot(s_ref[rows, :], yim_scr[...], preferred_element_type=F32))
        rr = lax.broadcasted_iota(jnp.int32, (ch, tc), 0)
        sg = jnp.where((rr & 1) == 0, inv_n, -inv_n)
        y = y * (2.0 * inv_n) + sg * y_nyq
        o_ref[rows, :] = x0_scr[rows, :] * (y + wv_scr[rows, :] * bias_ref[...])


def _hyena(lx, row_block0, p, short_w, short_b, kre, kim, bias, ctab, stab):
    tc = HY_COLS
    nj = HY_CH // tc
    seg = lambda s: pl.BlockSpec((lx, tc), lambda b, j: (row_block0 + b, s * nj + j))
    wseg = lambda s: pl.BlockSpec((SHORT_CONV, tc), lambda b, j: (0, s * nj + j))
    bseg = lambda s: pl.BlockSpec((1, tc), lambda b, j: (0, s * nj + j))
    col = lambda rows: pl.BlockSpec((rows, tc), lambda b, j: (0, j))
    tab = pl.BlockSpec((lx, lx), lambda b, j: (0, 0))
    return pl.pallas_call(
        functools.partial(_hyena_body, lx=lx),
        grid=(BATCH, nj),
        in_specs=[seg(0), seg(1), seg(2), wseg(0), wseg(1), wseg(2), bseg(0), bseg(1), bseg(2),
                  col(lx), col(lx), col(1), tab, tab],
        out_specs=pl.BlockSpec((lx, tc), lambda b, j: (b, j)),
        out_shape=jax.ShapeDtypeStruct((BATCH * lx, HY_CH), F32),
        scratch_shapes=[pltpu.VMEM((lx, tc), F32), pltpu.VMEM((lx, tc), F32), pltpu.VMEM((lx, tc), BF16),
                        pltpu.VMEM((lx, tc), BF16), pltpu.VMEM((lx, tc), BF16)],
        compiler_params=_cparams(("parallel", "parallel")),
        name=f"hyena_{lx}",
    )(p, p, p, short_w, short_w, short_w, short_b, short_b, short_b, kre, kim, bias, ctab, stab)


def _nabias_body(rpb_ref, o_ref, bm_scr):
    rb = pl.program_id(0)
    w = GRID_W

    @pl.when(rb == 0)
    def _():
        qc = lax.broadcasted_iota(jnp.int32, (w, w), 0)
        kc = lax.broadcasted_iota(jnp.int32, (w, w), 1)
        d = jnp.clip(kc - qc + (NA_WIN_COLS - 1), 0, 2 * NA_WIN_COLS - 2)
        c0 = jnp.clip(qc - NA_WIN_COLS // 2, 0, w - NA_WIN_COLS)
        ok = (kc >= c0) & (kc < c0 + NA_WIN_COLS)
        for h in range(NA_HEADS):
            for dr in range(2 * NA_WIN_ROWS - 1):
                acc = jnp.zeros((w, w), F32)
                for j in range(2 * NA_WIN_COLS - 1):
                    acc = jnp.where(d == j, rpb_ref[h, dr, j], acc)
                bm_scr[h, dr] = jnp.where(ok, acc, NEG_INF)

    start = jnp.clip(NA_QROWS * rb - NA_WIN_ROWS // 2, 0, GRID_ROWS - NA_KROWS)
    for h in range(NA_HEADS):
        for i in range(NA_QROWS):
            r = NA_QROWS * rb + i
            r0 = jnp.clip(r - NA_WIN_ROWS // 2, 0, GRID_ROWS - NA_WIN_ROWS)
            for j in range(NA_KROWS):
                kr = start + j
                valid = (kr >= r0) & (kr < r0 + NA_WIN_ROWS)
                didx = jnp.clip(kr - r + (NA_WIN_ROWS - 1), 0, 2 * NA_WIN_ROWS - 2)
                blk = jnp.where(valid, bm_scr[h, didx], NEG_INF)
                o_ref[0, h, i * w:(i + 1) * w, j * w:(j + 1) * w] = blk


def _na_bias(rpb):
    nrb = GRID_ROWS // NA_QROWS
    return pl.pallas_call(
        _nabias_body,
        grid=(nrb,),
        in_specs=[pl.BlockSpec(memory_space=pltpu.SMEM)],
        out_specs=pl.BlockSpec((1, NA_HEADS, NA_QROWS * GRID_W, NA_KROWS * GRID_W), lambda rb: (rb, 0, 0, 0)),
        out_shape=jax.ShapeDtypeStruct((nrb, NA_HEADS, NA_QROWS * GRID_W, NA_KROWS * GRID_W), F32),
        scratch_shapes=[pltpu.VMEM((NA_HEADS, 2 * NA_WIN_ROWS - 1, GRID_W, GRID_W), F32)],
        compiler_params=_cparams(("arbitrary",)),
        name="na_bias",
    )(rpb)


_NT = (((1,), (1,)), ((), ()))


def _na_body(q_ref, k_ref, v_ref, kc_ref, vc_ref, bias_ref, o_ref):
    rb = pl.program_id(1)
    start = jnp.clip(NA_QROWS * rb - NA_WIN_ROWS // 2, 0, GRID_ROWS - NA_KROWS)
    row0 = pl.multiple_of(start * GRID_W, GRID_W)
    nk = NA_KROWS * GRID_W
    scale = HEAD_DIM ** -0.5
    for h in range(NA_HEADS):
        cs = slice(h * HEAD_DIM, (h + 1) * HEAD_DIM)
        q = (q_ref[:, cs].astype(F32) * scale).astype(BF16)
        kw = k_ref[pl.ds(row0, nk), cs].astype(BF16)
        vw = v_ref[pl.ds(row0, nk), cs].astype(BF16)
        s = lax.dot_general(q, kw, _NT, preferred_element_type=F32)
        bm = bias_ref[0, h]
        s = jnp.where(bm > 0.5 * NEG_INF, s + bm, NEG_INF)
        sc = lax.dot_general(q, kc_ref[:, cs].astype(BF16), _NT, preferred_element_type=F32)
        m = jnp.maximum(jnp.max(s, axis=-1, keepdims=True), jnp.max(sc, axis=-1, keepdims=True))
        p = jnp.exp(s - m)
        pc = jnp.exp(sc - m)
        denom = jnp.sum(p, axis=-1, keepdims=True) + jnp.sum(pc, axis=-1, keepdims=True)
        o = (jnp.dot(p.astype(BF16), vw, preferred_element_type=F32)
             + jnp.dot(pc.astype(BF16), vc_ref[:, cs].astype(BF16), preferred_element_type=F32))
        o_ref[:, cs] = o / denom


def _na_latent(p, bias):
    nrb = GRID_ROWS // NA_QROWS
    tq = NA_QROWS * GRID_W
    cq, ck, cv = COL_NQ // NA_DIM, COL_NK // NA_DIM, COL_NV // NA_DIM
    cblk = T_LAT // CTX_LEN
    return pl.pallas_call(
        _na_body,
        grid=(BATCH, nrb),
        in_specs=[pl.BlockSpec((tq, NA_DIM), lambda b, r: (b * nrb + r, cq)),
                  pl.BlockSpec((SEQ, NA_DIM), lambda b, r: (b, ck)),
                  pl.BlockSpec((SEQ, NA_DIM), lambda b, r: (b, cv)),
                  pl.BlockSpec((CTX_LEN, NA_DIM), lambda b, r: (cblk + b, ck)),
                  pl.BlockSpec((CTX_LEN, NA_DIM), lambda b, r: (cblk + b, cv)),
                  pl.BlockSpec((1, NA_HEADS, tq, NA_KROWS * GRID_W), lambda b, r: (r, 0, 0, 0))],
        out_specs=pl.BlockSpec((tq, NA_DIM), lambda b, r: (b * nrb + r, 0)),
        out_shape=jax.ShapeDtypeStruct((T_LAT, NA_DIM), F32),
        compiler_params=_cparams(("parallel", "arbitrary")),
        name="na_latent",
    )(p, p, p, p, p, bias)


def _nactx_body(q_ref, k_ref, v_ref, o_ref):
    scale = HEAD_DIM ** -0.5
    for h in range(NA_HEADS):
        cs = slice(h * HEAD_DIM, (h + 1) * HEAD_DIM)
        q = (q_ref[:, cs].astype(F32) * scale).astype(BF16)
        s = lax.dot_general(q, k_ref[:, cs].astype(BF16), _NT, preferred_element_type=F32)
        m = jnp.max(s, axis=-1, keepdims=True)
        p = jnp.exp(s - m)
        denom = jnp.sum(p, axis=-1, keepdims=True)
        o = jnp.dot(p.astype(BF16), v_ref[:, cs].astype(BF16), preferred_element_type=F32)
        o_ref[:, cs] = o / denom


def _na_context(p):
    cq, ck, cv = COL_NQ // NA_DIM, COL_NK // NA_DIM, COL_NV // NA_DIM
    cblk = T_LAT // CTX_LEN
    spec = lambda c: pl.BlockSpec((CTX_LEN, NA_DIM), lambda b: (cblk + b, c))
    return pl.pallas_call(
        _nactx_body,
        grid=(BATCH,),
        in_specs=[spec(cq), spec(ck), spec(cv)],
        out_specs=pl.BlockSpec((CTX_LEN, NA_DIM), lambda b: (b, 0)),
        out_shape=jax.ShapeDtypeStruct((T_CTX, NA_DIM), F32),
        compiler_params=_cparams(("parallel",)),
        name="na_context",
    )(p, p, p)


_BNT = (((2,), (2,)), ((0,), (0,)))
_BNN = (((2,), (1,)), ((0,), (0,)))


def _rope_tables(lx):
    t = jnp.arange(lx)
    rows = (t // GRID_W).astype(F32)
    cols = (t % GRID_W).astype(F32)
    quarter = GLA_DK // 4
    inv = ROPE_BASE ** (-jnp.arange(quarter, dtype=F32) / quarter)
    ang_r = rows[:, None] * inv[None, :]
    ang_c = cols[:, None] * inv[None, :]
    cos = jnp.concatenate([jnp.cos(ang_r)] * 2 + [jnp.cos(ang_c)] * 2, axis=-1)
    sin = jnp.concatenate([-jnp.sin(ang_r), jnp.sin(ang_r), -jnp.sin(ang_c), jnp.sin(ang_c)], axis=-1)
    return jnp.tile(cos, (1, 2)), jnp.tile(sin, (1, 2))


def _log_sigmoid(z):
    return jnp.minimum(z, 0.0) - jnp.log(1.0 + jnp.exp(-jnp.abs(z)))


def _gla_body(*refs, lx, rope, with_out, with_init):
    it = iter(refs)
    q_ref = next(it) if with_out else None
    k_ref, v_ref = next(it), next(it)
    g_ref = next(it) if with_out else None
    ga_ref, w2_ref, b2_ref = next(it), next(it), next(it)
    cos_ref = sin_ref = None
    if rope:
        cos_ref, sin_ref = next(it), next(it)
    nw_ref = next(it) if with_out else None
    sf0_ref = sb0_ref = None
    if with_init:
        sf0_ref, sb0_ref = next(it), next(it)
    y_ref = next(it) if with_out else None
    sf_ref, sb_ref = next(it), next(it)
    ut_scr, stp_scr, dl_scr = next(it), next(it), next(it)

    c = GLA_CHUNK
    n = lx // c
    lane = lax.broadcasted_iota(jnp.int32, (lx, 2 * GLA_DK), 1)

    def rot(x):
        if not rope:
            return x
        even = ((lane // (GLA_DK // 4)) % 2) == 0
        partner = jnp.where(even, pltpu.roll(x, 2 * GLA_DK - GLA_DK // 4, 1), pltpu.roll(x, GLA_DK // 4, 1))
        return x * cos_ref[...] + partner * sin_ref[...]

    k = rot(k_ref[...].astype(F32))
    q = rot(q_ref[...].astype(F32) * (GLA_DK ** -0.5)) if with_out else None
    v3 = v_ref[...].reshape(n, c, 2 * GLA_DV)

    ga = ga_ref[...]
    log_f = _log_sigmoid(jnp.dot(ga, w2_ref[0], preferred_element_type=F32, precision=HIGHEST)
                         + b2_ref[0]) / GLA_GATE_NORM
    log_b = _log_sigmoid(jnp.dot(ga, w2_ref[1], preferred_element_type=F32, precision=HIGHEST)
                         + b2_ref[1]) / GLA_GATE_NORM
    pos = lax.broadcasted_iota(jnp.int32, (lx, 2 * GLA_DK), 0) % c
    cum_f, cum_b = log_f, log_b
    step = 1
    while step < c:
        cum_f = cum_f + jnp.where(pos >= step, pltpu.roll(cum_f, step, 0), 0.0)
        cum_b = cum_b + jnp.where(pos < c - step, pltpu.roll(cum_b, lx - step, 0), 0.0)
        step *= 2

    ti = lax.broadcasted_iota(jnp.int32, (n, c, c), 1)
    si = lax.broadcasted_iota(jnp.int32, (n, c, c), 2)
    o_acc = [None, None]
    for forward in (True, False):
        cum = cum_f if forward else cum_b
        cum3 = cum.reshape(n, c, 2 * GLA_DK)
        tot = cum3[:, c - 1:c, :] if forward else cum3[:, 0:1, :]
        kinv = (k * jnp.exp(-cum)).reshape(n, c, 2 * GLA_DK)
        kd = (kinv * jnp.exp(tot)).astype(BF16)
        kinv = kinv.astype(BF16)
        dl_scr[...] = jnp.exp(tot)
        qd3 = (q * jnp.exp(cum)).reshape(n, c, 2 * GLA_DK) if with_out else None
        keep = (ti >= si) if forward else (si >= ti)
        for hh in range(2):
            vh = v3[:, :, hh * GLA_DV:(hh + 1) * GLA_DV].astype(BF16)
            vt = jnp.swapaxes(vh, 1, 2)
            ut_scr[...] = lax.dot_general(vt, kd, _BNN, preferred_element_type=F32)
            st0 = (sf0_ref if forward else sb0_ref)[0, hh] if with_init else jnp.zeros((GLA_DV, 2 * GLA_DK), F32)

            def scan(i, st):
                ci = i if forward else n - 1 - i
                stp_scr[ci] = st
                return st * dl_scr[ci] + ut_scr[ci]

            st_fin = lax.fori_loop(0, n, scan, st0)
            (sf_ref if forward else sb_ref)[0, hh] = st_fin
            if with_out:
                in_head = (lax.broadcasted_iota(jnp.int32, (n, c, 2 * GLA_DK), 2) // GLA_DK) == hh
                qm = jnp.where(in_head, qd3, 0.0).astype(BF16)
                att = lax.dot_general(qm, kinv, _BNT, preferred_element_type=F32)
                att = jnp.where(keep, att, 0.0).astype(BF16)
                o = (lax.dot_general(att, vh, _BNN, preferred_element_type=F32)
                     + lax.dot_general(qm, stp_scr[...].astype(BF16), _BNT, preferred_element_type=F32))
                o_acc[hh] = o if o_acc[hh] is None else o_acc[hh] + o

    if with_out:
        for hh in range(2):
            vs = slice(hh * GLA_DV, (hh + 1) * GLA_DV)
            o = _rms_rows(o_acc[hh].reshape(lx, GLA_DV)) * nw_ref[...]
            y_ref[:, vs] = o * _silu(g_ref[:, vs].astype(F32))


def _gla(lx, row_block0, p, p_ga, w2cat, b2, norm_w, rope_tabs, init, with_out):
    npair = GLA_HEADS // 2
    rope = rope_tabs is not None
    with_init = init is not None
    rb = lambda b: row_block0 + b
    qk = lambda col: pl.BlockSpec((lx, 2 * GLA_DK), lambda b, h: (rb(b), col // (2 * GLA_DK) + h))
    vg = lambda col: pl.BlockSpec((lx, 2 * GLA_DV), lambda b, h: (rb(b), col // (2 * GLA_DV) + h))
    st_spec = pl.BlockSpec((1, 2, GLA_DV, 2 * GLA_DK), lambda b, h: (b, h, 0, 0))
    in_specs, args = [], []
    if with_out:
        in_specs.append(qk(COL_GQ)); args.append(p)
    in_specs += [qk(COL_GK), vg(COL_GV)]; args += [p, p]
    if with_out:
        in_specs.append(vg(COL_GG)); args.append(p)
    in_specs += [pl.BlockSpec((lx, 2 * GLA_RANK), lambda b, h: (rb(b), 0)),
                 pl.BlockSpec((2, 2 * GLA_RANK, 2 * GLA_DK), lambda b, h: (0, 0, h)),
                 pl.BlockSpec((2, 1, 2 * GLA_DK), lambda b, h: (0, 0, h))]
    args += [p_ga, w2cat, b2]
    if rope:
        in_specs += [pl.BlockSpec((lx, 2 * GLA_DK), lambda b, h: (0, 0))] * 2
        args += list(rope_tabs)
    if with_out:
        in_specs.append(pl.BlockSpec((1, GLA_DV), lambda b, h: (0, 0))); args.append(norm_w)
    if with_init:
        in_specs += [st_spec, st_spec]; args += list(init)
    st_shape = jax.ShapeDtypeStruct((BATCH, GLA_HEADS, GLA_DV, 2 * GLA_DK), F32)
    out_specs, out_shape = [], []
    if with_out:
        out_specs.append(pl.BlockSpec((lx, 2 * GLA_DV), lambda b, h: (b, h)))
        out_shape.append(jax.ShapeDtypeStruct((BATCH * lx, GLA_V), F32))
    out_specs += [st_spec, st_spec]
    out_shape += [st_shape, st_shape]
    n = lx // GLA_CHUNK
    outs = pl.pallas_call(
        functools.partial(_gla_body, lx=lx, rope=rope, with_out=with_out, with_init=with_init),
        grid=(BATCH, npair),
        in_specs=in_specs, out_specs=out_specs, out_shape=out_shape,
        scratch_shapes=[pltpu.VMEM((n, GLA_DV, 2 * GLA_DK), F32), pltpu.VMEM((n, GLA_DV, 2 * GLA_DK), F32),
                        pltpu.VMEM((n, 1, 2 * GLA_DK), F32)],
        compiler_params=_cparams(("parallel", "parallel")),
        name=f"gla_{lx}",
    )(*args)
    if with_out:
        return outs[0], outs[1], outs[2]
    return None, outs[0], outs[1]


def _outproj_body(*refs, with_ctx):
    it = iter(refs)
    lat = [next(it) for _ in range(3)]
    ctx = [next(it) for _ in range(3)] if with_ctx else None
    x_ref, mod_ref, hyw_ref, naw_ref, w_ref, g_ref, b_ref, o_ref = (next(it) for _ in range(8))

    def pick(j):
        if not with_ctx:
            return lat[j][...]
        return jnp.where(pl.program_id(0) >= T_LAT // ROW_TILE, ctx[j][...], lat[j][...])

    ya = (_rms_rows(pick(0)) * hyw_ref[...]).astype(BF16)
    yb = (_rms_rows(pick(1)) * naw_ref[...]).astype(BF16)
    yc = pick(2).astype(BF16)
    y = (jnp.dot(ya, w_ref[0:HY_CH, :], preferred_element_type=F32)
         + jnp.dot(yb, w_ref[HY_CH:HY_CH + NA_DIM, :], preferred_element_type=F32)
         + jnp.dot(yc, w_ref[HY_CH + NA_DIM:, :], preferred_element_type=F32))
    r = ALPHA * x_ref[...] + mod_ref[0, 2:3, :] * y
    o_ref[...] = _layer_norm_rows(r) * g_ref[...] + b_ref[...]


def _outproj(rows, lat, ctx, xa, mod, hy_norm_w, na_norm_w, w_out_bf, ln_g, ln_b):
    tm = ROW_TILE
    with_ctx = ctx is not None
    nlat = T_LAT // tm
    mrow = _mod_row(tm)
    widths = (HY_CH, NA_DIM, GLA_V)
    in_specs = [pl.BlockSpec((tm, w), lambda i: (jnp.minimum(i, nlat - 1), 0)) for w in widths]
    args = list(lat)
    if with_ctx:
        in_specs += [pl.BlockSpec((tm, w), lambda i: (jnp.maximum(i - nlat, 0), 0)) for w in widths]
        args += list(ctx)
    vec = lambda w: pl.BlockSpec((1, w), lambda i: (0, 0))
    in_specs += [pl.BlockSpec((tm, D_MODEL), lambda i: (i, 0)),
                 pl.BlockSpec((1, 6, D_MODEL), lambda i: (mrow(i), 0, 0)),
                 vec(HY_CH), vec(NA_DIM),
                 pl.BlockSpec((D_MODEL, D_MODEL), lambda i: (0, 0)),
                 vec(D_MODEL), vec(D_MODEL)]
    args += [xa, mod, hy_norm_w, na_norm_w, w_out_bf, ln_g, ln_b]
    return pl.pallas_call(
        functools.partial(_outproj_body, with_ctx=with_ctx),
        grid=(rows // tm,),
        in_specs=in_specs,
        out_specs=pl.BlockSpec((tm, D_MODEL), lambda i: (i, 0)),
        out_shape=jax.ShapeDtypeStruct((rows, D_MODEL), F32),
        compiler_params=_cparams(("parallel",)),
        name=f"outproj_{rows}",
    )(*args)


ROUTER_ROWS = 8 + N_EXPERTS


def _router_body(x_ref, mod_ref, w_ref, b_ref, h_ref, eid_ref, gw_ref):
    tm = ROW_TILE
    h2 = _layer_norm_rows(x_ref[...]) * (1.0 + mod_ref[0, 4:5, :]) + mod_ref[0, 3:4, :]
    _to_slabs(h_ref, h2)
    logits = lax.dot_general(w_ref[...], h2, _NT, preferred_element_type=F32, precision=HIGHEST) + b_ref[...]
    lg = logits[0:N_GROUPS, :]
    eg = jnp.exp(lg - jnp.max(lg, axis=0, keepdims=True))
    pg = eg / jnp.sum(eg, axis=0, keepdims=True)
    top_pg = jnp.max(pg, axis=0, keepdims=True)
    gio = lax.broadcasted_iota(jnp.int32, (N_GROUPS, tm), 0)
    gi = jnp.min(jnp.where(pg == top_pg, gio, N_GROUPS), axis=0, keepdims=True)
    le = logits[8:8 + N_EXPERTS, :].reshape(N_GROUPS, EXPERTS_PER_GROUP, tm)
    gsel = lax.broadcasted_iota(jnp.int32, (N_GROUPS, EXPERTS_PER_GROUP, tm), 0) == gi[None]
    les = jnp.sum(jnp.where(gsel, le, 0.0), axis=0)
    eio = lax.broadcasted_iota(jnp.int32, (EXPERTS_PER_GROUP, tm), 0)
    v1 = jnp.max(les, axis=0, keepdims=True)
    i1 = jnp.min(jnp.where(les == v1, eio, EXPERTS_PER_GROUP), axis=0, keepdims=True)
    les2 = jnp.where(eio == i1, -jnp.inf, les)
    v2 = jnp.max(les2, axis=0, keepdims=True)
    i2 = jnp.min(jnp.where(les2 == v2, eio, EXPERTS_PER_GROUP), axis=0, keepdims=True)
    t = jnp.exp(v2 - v1)
    w1 = top_pg / (1.0 + t)
    eid_ref[...] = jnp.concatenate([gi * EXPERTS_PER_GROUP + i1, gi * EXPERTS_PER_GROUP + i2], axis=0)
    gw_ref[...] = jnp.concatenate([w1, w1 * t], axis=0)


def _router(rows, x1, mod, w_rt, b_rt):
    tm = ROW_TILE
    mrow = _mod_row(tm)
    return pl.pallas_call(
        _router_body,
        grid=(rows // tm,),
        in_specs=[pl.BlockSpec((tm, D_MODEL), lambda i: (i, 0)),
                  pl.BlockSpec((1, 6, D_MODEL), lambda i: (mrow(i), 0, 0)),
                  pl.BlockSpec((ROUTER_ROWS, D_MODEL), lambda i: (0, 0)),
                  pl.BlockSpec((ROUTER_ROWS, 1), lambda i: (0, 0))],
        out_specs=[pl.BlockSpec((tm * SLAB, LANES), lambda i: (i, 0)),
                   pl.BlockSpec((2, tm), lambda i: (0, i)),
                   pl.BlockSpec((2, tm), lambda i: (0, i))],
        out_shape=[jax.ShapeDtypeStruct((rows * SLAB, LANES), F32),
                   jax.ShapeDtypeStruct((2, rows), jnp.int32),
                   jax.ShapeDtypeStruct((2, rows), F32)],
        compiler_params=_cparams(("parallel",)),
        name=f"router_{rows}",
    )(x1, mod, w_rt, b_rt)


def _moe_tiles(rows):
    return -(-(2 * rows + N_EXPERTS * (MOE_TILE - 1)) // MOE_TILE)


def _dispatch_plan(rows, eid):
    nt = _moe_tiles(rows)
    e = eid.reshape(-1)
    experts = jnp.arange(N_EXPERTS, dtype=jnp.int32)
    onehot = (e[:, None] == experts[None, :]).astype(jnp.int32)
    csum = jnp.cumsum(onehot, axis=0)
    rank = jnp.sum(csum * onehot, axis=1) - 1
    counts = csum[-1]
    padded = ((counts + MOE_TILE - 1) // MOE_TILE) * MOE_TILE
    pend = jnp.cumsum(padded)
    pstart = pend - padded
    dest = jnp.sum(onehot * pstart[None, :], axis=1) + rank
    tile_start = jnp.arange(nt, dtype=jnp.int32) * MOE_TILE
    tile_valid = (tile_start < pend[-1]).astype(jnp.int32)
    tile_expert = jnp.sum((pend[None, :] <= tile_start[:, None]).astype(jnp.int32), axis=1)
    last_expert = jnp.max(jnp.where(counts > 0, experts, 0))
    tile_expert = jnp.minimum(tile_expert, last_expert)
    first = jnp.sum(((pstart[None, :] == tile_start[:, None]) & (padded[None, :] > 0)).astype(jnp.int32), axis=1)
    tile_first = jnp.minimum(first, 1) * tile_valid
    return dict(dest=dest, pend=pend, counts=counts, tile_expert=tile_expert, tile_valid=tile_valid,
                tile_first=tile_first, n_valid=(pend[-1] // MOE_TILE).reshape(1))


def _slab_copy(src, src_tok, dst, dst_tok, sem):
    s0 = pl.multiple_of(src_tok * SLAB, SLAB)
    d0 = pl.multiple_of(dst_tok * SLAB, SLAB)
    return pltpu.make_async_copy(src.at[pl.ds(s0, SLAB), :], dst.at[pl.ds(d0, SLAB), :], sem)


def _invmap_body(dest_ref, tok_ref, *, rows):
    def clear(j, carry):
        tok_ref[j] = 0
        return carry

    lax.fori_loop(0, tok_ref.shape[0], clear, 0, unroll=8)

    def fill(a, carry):
        tok_ref[dest_ref[a]] = jnp.where(a >= rows, a - rows, a)
        return carry

    lax.fori_loop(0, 2 * rows, fill, 0, unroll=8)


def _invmap(rows, dest):
    return pl.pallas_call(
        functools.partial(_invmap_body, rows=rows),
        in_specs=[pl.BlockSpec(memory_space=pltpu.SMEM)],
        out_specs=pl.BlockSpec(memory_space=pltpu.SMEM),
        out_shape=jax.ShapeDtypeStruct((_moe_tiles(rows) * MOE_TILE,), jnp.int32),
        name=f"moe_invmap_{rows}",
    )(dest)


def _moe_body(te_ref, tv_ref, tf_ref, nv_ref, tok_ref, h_hbm, wup_ref, wdn_ref, o_ref, x_scr, wup_bf, wdn_bf, sem):
    i = pl.program_id(0)
    tm = MOE_TILE

    def gather(tile, slot):
        def start(r, carry):
            _slab_copy(h_hbm, tok_ref[tile * tm + r], x_scr.at[slot], r, sem.at[slot]).start()
            return carry

        lax.fori_loop(0, tm, start, 0, unroll=8)

    @pl.when(i == 0)
    def _():
        gather(0, 0)

    for slot in range(2):
        @pl.when((tv_ref[i] == 1) & (i % 2 == slot))
        def _():
            @pl.when(i + 1 < nv_ref[0])
            def _():
                gather(i + 1, 1 - slot)

            pltpu.make_async_copy(h_hbm.at[pl.ds(0, tm * SLAB), :], x_scr.at[slot], sem.at[slot]).wait()

            @pl.when(tf_ref[i] == 1)
            def _():
                ch = 256
                for r in range(D_MODEL // ch):
                    wup_bf[r * ch:(r + 1) * ch, :] = wup_ref[r * ch:(r + 1) * ch, :].astype(BF16)
                for r in range(D_EXPERT // ch):
                    wdn_bf[r * ch:(r + 1) * ch, :] = wdn_ref[r * ch:(r + 1) * ch, :].astype(BF16)

            x = _from_slabs(x_scr.at[slot], tm).astype(BF16)
            gu = jnp.dot(x, wup_bf[...], preferred_element_type=F32)
            act = _silu(gu[:, :D_EXPERT]) * gu[:, D_EXPERT:]
            _to_slabs(o_ref, jnp.dot(act.astype(BF16), wdn_bf[...], preferred_element_type=F32))

    @pl.when(tv_ref[i] == 0)
    def _():
        o_ref[...] = jnp.zeros_like(o_ref)


def _moe_ffn(rows, layer, plan, row_token, h2, w_up, w_down):
    nt = _moe_tiles(rows)
    tm = MOE_TILE

    def wmap(i, te, tv, tf, nv, tok):
        return (layer, te[i] // EXPERTS_PER_GROUP, te[i] % EXPERTS_PER_GROUP, 0, 0)

    grid_spec = pltpu.PrefetchScalarGridSpec(
        num_scalar_prefetch=5,
        grid=(nt,),
        in_specs=[pl.BlockSpec(memory_space=pl.ANY),
                  pl.BlockSpec((None, None, None, D_MODEL, 2 * D_EXPERT), wmap),
                  pl.BlockSpec((None, None, None, D_EXPERT, D_MODEL), wmap)],
        out_specs=pl.BlockSpec((tm * SLAB, LANES), lambda i, te, tv, tf, nv, tok: (i, 0)),
        scratch_shapes=[pltpu.VMEM((2, tm * SLAB, LANES), F32), pltpu.VMEM((D_MODEL, 2 * D_EXPERT), BF16),
                        pltpu.VMEM((D_EXPERT, D_MODEL), BF16), pltpu.SemaphoreType.DMA((2,))],
    )
    return pl.pallas_call(
        _moe_body,
        grid_spec=grid_spec,
        out_shape=jax.ShapeDtypeStruct((nt * tm * SLAB, LANES), F32),
        compiler_params=_cparams(("arbitrary",)),
        name=f"moe_ffn_{rows}",
    )(plan["tile_expert"], plan["tile_valid"], plan["tile_first"], plan["n_valid"], row_token, h2, w_up, w_down)


def _combine_body(pos_ref, y_hbm, gw_ref, x_ref, mod_ref, g_ref, b_ref, o_ref, y_scr, sem, *, rows):
    i = pl.program_id(0)
    tm = ROW_TILE

    def gather(tile, buf):
        def start(r, carry):
            for s in range(2):
                _slab_copy(y_hbm, pos_ref[s * rows + tile * tm + r], y_scr.at[buf, s], r, sem.at[buf]).start()
            return carry

        lax.fori_loop(0, tm, start, 0, unroll=8)

    @pl.when(i == 0)
    def _():
        gather(0, 0)

    for buf in range(2):
        @pl.when(i % 2 == buf)
        def _():
            @pl.when(i + 1 < pl.num_programs(0))
            def _():
                gather(i + 1, 1 - buf)

            for s in range(2):
                pltpu.make_async_copy(y_hbm.at[pl.ds(0, tm * SLAB), :], y_scr.at[buf, s], sem.at[buf]).wait()
            f = (_from_slabs(y_scr.at[buf, 0], tm) * gw_ref[:, 0:1]
                 + _from_slabs(y_scr.at[buf, 1], tm) * gw_ref[:, 1:2])
            r = ALPHA * x_ref[...] + mod_ref[0, 5:6, :] * f
            o_ref[...] = _layer_norm_rows(r) * g_ref[...] + b_ref[...]


def _combine(rows, pos, ysort, gw_t, x1, mod, ln_g, ln_b):
    tm = ROW_TILE
    mrow = _mod_row(tm)
    vec = pl.BlockSpec((1, D_MODEL), lambda i, pos: (0, 0))
    grid_spec = pltpu.PrefetchScalarGridSpec(
        num_scalar_prefetch=1,
        grid=(rows // tm,),
        in_specs=[pl.BlockSpec(memory_space=pl.ANY),
                  pl.BlockSpec((tm, 2), lambda i, pos: (i, 0)),
                  pl.BlockSpec((tm, D_MODEL), lambda i, pos: (i, 0)),
                  pl.BlockSpec((1, 6, D_MODEL), lambda i, pos: (mrow(i), 0, 0)),
                  vec, vec],
        out_specs=pl.BlockSpec((tm, D_MODEL), lambda i, pos: (i, 0)),
        scratch_shapes=[pltpu.VMEM((2, 2, tm * SLAB, LANES), F32), pltpu.SemaphoreType.DMA((2,))],
    )
    return pl.pallas_call(
        functools.partial(_combine_body, rows=rows),
        grid_spec=grid_spec,
        out_shape=jax.ShapeDtypeStruct((rows, D_MODEL), F32),
        compiler_params=_cparams(("arbitrary",)),
        name=f"combine_{rows}",
    )(pos, ysort, gw_t, x1, mod, ln_g, ln_b)


def _layer(xa, c8, tabs, w, big, layer, last):
    row = lambda v: v.reshape(1, -1)
    mod = _ada(c8, layer, big["w_ada"], w["b_ada"]).reshape(8, 6, D_MODEL)
    p, p_ga = _inproj(xa, mod, w["w_in"][:, :P_COLS].astype(BF16), w["w_in"][:, P_COLS:].astype(BF16))

    w1p = jnp.pad(w["hy_f_w1"], ((0, 128 - w["hy_f_w1"].shape[0]), (0, 0)))
    filt_args = (w1p, row(w["hy_f_b1"]), w["hy_f_w2"], row(w["hy_f_b2"]), w["hy_f_w3"], row(w["hy_sin_freq"]))
    short_b, hy_bias = row(w["hy_short_b"]), row(w["hy_bias"])
    kre, kim = _hyena_filter(SEQ, tabs["z_lat"], tabs["win_lat"], tabs["c_lat"], tabs["s_lat"], *filt_args)
    ya = _hyena(SEQ, 0, p, w["hy_short_w"], short_b, kre, kim, hy_bias, tabs["c_lat"], tabs["s_lat"])

    yb = _na_latent(p, _na_bias(w["na_rpb"]))

    zpad = jnp.zeros((GLA_RANK, GLA_K), F32)
    w2cat = jnp.stack([jnp.concatenate([w["gla_a_w2"][0], zpad], axis=0),
                       jnp.concatenate([zpad, w["gla_a_w2"][1]], axis=0)])
    b2 = w["gla_a_b"].reshape(2, 1, GLA_K)
    nw = row(w["gla_norm_w"])
    ctx_blk = T_LAT // CTX_LEN
    yc_ctx, sf, sb = _gla(CTX_LEN, ctx_blk, p, p_ga, w2cat, b2, nw, None, None, with_out=True)
    yc, _, _ = _gla(SEQ, 0, p, p_ga, w2cat, b2, nw, tabs["rope"], (sf, sb), with_out=True)

    lat = (ya, yb, yc)
    ctx = None
    if not last:
        kre_c, kim_c = _hyena_filter(CTX_LEN, tabs["z_ctx"], tabs["win_ctx"], tabs["c_ctx"], tabs["s_ctx"], *filt_args)
        ya_c = _hyena(CTX_LEN, ctx_blk, p, w["hy_short_w"], short_b, kre_c, kim_c, hy_bias,
                      tabs["c_ctx"], tabs["s_ctx"])
        ctx = (ya_c, _na_context(p), yc_ctx)

    rows = T_LAT if last else T_ALL
    x1 = _outproj(rows, lat, ctx, xa, mod, row(w["hy_norm_w"]), row(w["na_norm_w"]),
                  w["w_out"].astype(BF16), row(w["ln1_g"]), row(w["ln1_b"]))

    pad4 = jnp.zeros((8 - N_GROUPS, D_MODEL), F32)
    w_rt = jnp.concatenate([w["w_rg"].T, pad4, w["w_re"].T], axis=0)
    b_rt = jnp.concatenate([w["b_rg"], jnp.zeros((8 - N_GROUPS,), F32), w["b_re"]]).reshape(ROUTER_ROWS, 1)
    h2, eid, gw = _router(rows, x1, mod, w_rt, b_rt)
    plan = _dispatch_plan(rows, eid)
    ysort = _moe_ffn(rows, layer, plan, _invmap(rows, plan["dest"]), h2, big["w_up"], big["w_down"])
    return _combine(rows, plan["dest"], ysort, gw.T, x1, mod, row(w["ln2_g"]), row(w["ln2_b"]))


_BIG_WEIGHTS = ("w_ada", "w_up", "w_down")
_LAYER_WEIGHTS = ("w_ada", "b_ada", "w_in", "hy_short_w", "hy_short_b", "hy_f_w1", "hy_f_b1", "hy_f_w2", "hy_f_b2",
                  "hy_f_w3", "hy_sin_freq", "hy_bias", "hy_norm_w", "na_rpb", "na_norm_w", "gla_a_w2", "gla_a_b",
                  "gla_norm_w", "w_out", "ln1_g", "ln1_b", "w_rg", "b_rg", "w_re", "b_re", "w_up", "w_down",
                  "ln2_g", "ln2_b")


def kernel(x, c, ctx, c_ctx, w_ada, b_ada, w_in, hy_short_w, hy_short_b, hy_f_w1, hy_f_b1, hy_f_w2, hy_f_b2, hy_f_w3, hy_sin_freq, hy_bias, hy_norm_w, na_rpb, na_norm_w, gla_a_w2, gla_a_b, gla_norm_w, w_out, ln1_g, ln1_b, w_rg, b_rg, w_re, b_re, w_up, w_down, ln2_g, ln2_b):
    stacked = dict(zip(_LAYER_WEIGHTS, (w_ada, b_ada, w_in, hy_short_w, hy_short_b, hy_f_w1, hy_f_b1, hy_f_w2, hy_f_b2,
                                        hy_f_w3, hy_sin_freq, hy_bias, hy_norm_w, na_rpb, na_norm_w, gla_a_w2,
                                        gla_a_b, gla_norm_w, w_out, ln1_g, ln1_b, w_rg, b_rg, w_re, b_re, w_up,
                                        w_down, ln2_g, ln2_b)))
    tabs = {}
    tabs["c_lat"], tabs["s_lat"] = _dft_tables(SEQ)
    tabs["c_ctx"], tabs["s_ctx"] = _dft_tables(CTX_LEN)
    tabs["z_lat"], tabs["win_lat"] = _hyena_features(SEQ)
    tabs["z_ctx"], tabs["win_ctx"] = _hyena_features(CTX_LEN)
    tabs["rope"] = _rope_tables(SEQ)

    xa = jnp.concatenate([x.reshape(T_LAT, D_MODEL), ctx.reshape(T_CTX, D_MODEL)], axis=0)
    c8 = jnp.concatenate([c, c_ctx[None, :], jnp.zeros((8 - BATCH - 1, D_MODEL), F32)], axis=0)
    big = {k: stacked[k] for k in _BIG_WEIGHTS}
    for i in range(DEPTH):
        small = {k: v[i] for k, v in stacked.items() if k not in _BIG_WEIGHTS}
        xa = _layer(xa, c8, tabs, small, big, i, last=(i == DEPTH - 1))
    return xa.reshape(BATCH, SEQ, D_MODEL)
```

```python
import functools
import math

import jax
import jax.numpy as jnp
import numpy as np
from jax import lax
from jax.experimental import pallas as pl
from jax.experimental.pallas import tpu as pltpu

F32 = jnp.float32
BF16 = jnp.bfloat16
HIGHEST = lax.Precision.HIGHEST

D_MODEL = 2048
BATCH = 4
SEQ = 2048
DEPTH = 2
CTX_LEN = 256
GRID_W = 64
HEAD_DIM = 128
HY_CH = 512
NA_HEADS = 6
NA_DIM = 768
GLA_HEADS = 6
GLA_DK = 64
GLA_DV = 128
GLA_K = 384
GLA_V = 768
GLA_RANK = 16
GLA_GATE_NORM = 16.0
GLA_CHUNK = 64
ROPE_BASE = 10000.0
SHORT_CONV = 3
HY_EMB_BANDS = 16
HY_FILTER_WIDTH = 64
HY_MIN_DECAY = math.log(1e-2) / 1.5
HY_MAX_DECAY = math.log(1e-2) / 0.3
NA_WIN_ROWS = 8
NA_WIN_COLS = 16
NEG_INF = -1e30
N_GROUPS = 4
EXPERTS_PER_GROUP = 8
N_EXPERTS = N_GROUPS * EXPERTS_PER_GROUP
D_EXPERT = 512
ALPHA = (2 * DEPTH) ** 0.25
LN_EPS = 1e-6

T_LAT = BATCH * SEQ
T_CTX = BATCH * CTX_LEN
T_ALL = T_LAT + T_CTX
GRID_ROWS = SEQ // GRID_W

COL_HY = 0
COL_NQ = 3 * HY_CH
COL_NK = COL_NQ + NA_DIM
COL_NV = COL_NK + NA_DIM
COL_GQ = COL_NV + NA_DIM
COL_GK = COL_GQ + GLA_K
COL_GV = COL_GK + GLA_K
COL_GG = COL_GV + GLA_V
COL_GA = COL_GG + GLA_V
P_COLS = COL_GA

VMEM_LIMIT_BYTES = 56 * 1024 * 1024
ROW_TILE = 256
INPROJ_ROWS = 1024
INPROJ_COLS = 512
INPROJ_CHUNK = 128
HY_COLS = 256
DFT_CHUNK = 512
NA_QROWS = 4
NA_KROWS = NA_QROWS + NA_WIN_ROWS - 1
MOE_TILE = 256
LANES = 128
SLAB = D_MODEL // LANES


def _to_slabs(ref, x):
    n = x.shape[0]
    for c in range(SLAB):
        ref[pl.ds(c, n, stride=SLAB), :] = x[:, c * LANES:(c + 1) * LANES]


def _from_slabs(ref, n):
    return jnp.concatenate([ref[pl.ds(c, n, stride=SLAB), :] for c in range(SLAB)], axis=1)


def _cparams(sem, vmem=VMEM_LIMIT_BYTES):
    return pltpu.CompilerParams(dimension_semantics=sem, vmem_limit_bytes=vmem)


def _layer_norm_rows(x):
    mu = jnp.mean(x, axis=-1, keepdims=True)
    xc = x - mu
    return xc * lax.rsqrt(jnp.mean(xc * xc, axis=-1, keepdims=True) + LN_EPS)


def _rms_rows(x):
    return x * lax.rsqrt(jnp.mean(x * x, axis=-1, keepdims=True) + LN_EPS)


def _silu(x):
    return x / (1.0 + jnp.exp(-x))


def _mod_row(tile_rows):
    per_batch = SEQ // tile_rows
    return lambda i: jnp.minimum(i // per_batch, BATCH)


def _ada_body(c_ref, w_ref, b_ref, o_ref):
    s = _silu(c_ref[...])
    o_ref[...] = jnp.dot(s, w_ref[...], preferred_element_type=F32, precision=HIGHEST) + b_ref[...]


def _ada(c8, layer, w_ada, b_ada):
    tn = 1024
    n = 6 * D_MODEL
    return pl.pallas_call(
        _ada_body,
        grid=(n // tn,),
        in_specs=[pl.BlockSpec((8, D_MODEL), lambda j: (0, 0)),
                  pl.BlockSpec((None, D_MODEL, tn), lambda j: (layer, 0, j)),
                  pl.BlockSpec((1, tn), lambda j: (0, j))],
        out_specs=pl.BlockSpec((8, tn), lambda j: (0, j)),
        out_shape=jax.ShapeDtypeStruct((8, n), F32),
        compiler_params=_cparams(("parallel",)),
        name="ada_mod",
    )(c8, w_ada, b_ada.reshape(1, n))


def _inproj_body(x_ref, mod_ref, w_ref, wga_ref, p_ref, ga_ref, h_scr):
    @pl.when(pl.program_id(1) == 0)
    def _():
        sh = mod_ref[0, 0:1, :]
        sc = mod_ref[0, 1:2, :]

        def chunk(r, carry):
            rows = pl.ds(pl.multiple_of(r * INPROJ_CHUNK, INPROJ_CHUNK), INPROJ_CHUNK)
            h = _layer_norm_rows(x_ref[rows, :]) * (1.0 + sc) + sh
            h_scr[rows, :] = h.astype(BF16)
            return carry

        lax.fori_loop(0, INPROJ_ROWS // INPROJ_CHUNK, chunk, 0)
        ga_ref[...] = jnp.dot(h_scr[...], wga_ref[...], preferred_element_type=F32)

    p_ref[...] = jnp.dot(h_scr[...], w_ref[...], preferred_element_type=F32).astype(p_ref.dtype)


def _inproj(xa, mod, w_in_bf, w_ga_bf):
    tm, tn = INPROJ_ROWS, INPROJ_COLS
    mrow = _mod_row(tm)
    return pl.pallas_call(
        _inproj_body,
        grid=(T_ALL // tm, P_COLS // tn),
        in_specs=[pl.BlockSpec((tm, D_MODEL), lambda i, j: (i, 0)),
                  pl.BlockSpec((1, 6, D_MODEL), lambda i, j: (mrow(i), 0, 0)),
                  pl.BlockSpec((D_MODEL, tn), lambda i, j: (0, j)),
                  pl.BlockSpec((D_MODEL, 2 * GLA_RANK), lambda i, j: (0, 0))],
        out_specs=[pl.BlockSpec((tm, tn), lambda i, j: (i, j)),
                   pl.BlockSpec((tm, 2 * GLA_RANK), lambda i, j: (i, 0))],
        out_shape=[jax.ShapeDtypeStruct((T_ALL, P_COLS), BF16),
                   jax.ShapeDtypeStruct((T_ALL, 2 * GLA_RANK), F32)],
        scratch_shapes=[pltpu.VMEM((tm, D_MODEL), BF16)],
        compiler_params=_cparams(("parallel", "arbitrary")),
        name="inproj",
    )(xa, mod, w_in_bf, w_ga_bf)


def _dft_tables(lx):
    n = 2 * lx
    k = jnp.arange(lx, dtype=jnp.int32)
    idx = (k[:, None] * k[None, :]) % n
    ang = idx.astype(F32) * (2.0 * math.pi / n)
    return jnp.cos(ang).astype(BF16), jnp.sin(ang).astype(BF16)


def _hyena_features(lx):
    n = jnp.arange(lx, dtype=F32)
    t = n / max(lx - 1, 1)
    bands = jnp.linspace(1e-4, HY_EMB_BANDS - 1, HY_EMB_BANDS, dtype=F32)
    ang = (2.0 * math.pi / lx) * n[:, None] * bands[None, :]
    z = jnp.concatenate([t[:, None], jnp.cos(ang), -jnp.sin(ang)], axis=-1)
    z = jnp.pad(z, ((0, 0), (0, 128 - z.shape[1])))
    deltas = jnp.abs(jnp.linspace(HY_MIN_DECAY, HY_MAX_DECAY, HY_CH, dtype=F32))
    win = jnp.exp(-t[:, None] * deltas[None, :])
    return z, win


def _split_dot(tab, x):
    xh = x.astype(BF16)
    xl = (x - xh.astype(F32)).astype(BF16)
    return (jnp.dot(tab, xh, preferred_element_type=F32) + jnp.dot(tab, xl, preferred_element_type=F32))


def _hyfilt_body(z_ref, w1_ref, b1_ref, w2_ref, b2_ref, w3f_ref, w3b_ref, fr_ref, win_ref, c_ref, s_ref,
                 kre_ref, kim_ref):
    fr = fr_ref[...]
    hid = jnp.sin(fr * (jnp.dot(z_ref[...], w1_ref[...], preferred_element_type=F32, precision=HIGHEST) + b1_ref[...]))
    hid = jnp.sin(fr * (jnp.dot(hid, w2_ref[...], preferred_element_type=F32, precision=HIGHEST) + b2_ref[...]))
    win = win_ref[...]
    hf = jnp.dot(hid, w3f_ref[...], preferred_element_type=F32, precision=HIGHEST) * win
    hb = jnp.dot(hid, w3b_ref[...], preferred_element_type=F32, precision=HIGHEST) * win
    row = lax.broadcasted_iota(jnp.int32, hf.shape, 0)
    hb = jnp.where(row == 0, 0.0, hb)
    hs = hf + hb
    hd = hf - hb
    sgn = jnp.where((row & 1) == 0, 1.0, -1.0)
    k_nyq = jnp.sum(hs * sgn, axis=0, keepdims=True)
    kre_ref[...] = _split_dot(c_ref[...], hs)
    kim = -_split_dot(s_ref[...], hd)
    kim_ref[...] = jnp.where(row == 0, k_nyq, kim)


def _hyena_filter(lx, z, win, ctab, stab, w1p, b1, w2, b2, w3, fr):
    tc = HY_COLS
    nj = HY_CH // tc
    full = lambda shape: pl.BlockSpec(shape, lambda j: (0,) * len(shape))
    return pl.pallas_call(
        _hyfilt_body,
        grid=(nj,),
        in_specs=[full((lx, 128)), full((128, HY_FILTER_WIDTH)), full((1, HY_FILTER_WIDTH)),
                  full((HY_FILTER_WIDTH, HY_FILTER_WIDTH)), full((1, HY_FILTER_WIDTH)),
                  pl.BlockSpec((HY_FILTER_WIDTH, tc), lambda j: (0, j)),
                  pl.BlockSpec((HY_FILTER_WIDTH, tc), lambda j: (0, nj + j)),
                  full((1, HY_FILTER_WIDTH)),
                  pl.BlockSpec((lx, tc), lambda j: (0, j)),
                  full((lx, lx)), full((lx, lx))],
        out_specs=[pl.BlockSpec((lx, tc), lambda j: (0, j)), pl.BlockSpec((lx, tc), lambda j: (0, j))],
        out_shape=[jax.ShapeDtypeStruct((lx, HY_CH), F32), jax.ShapeDtypeStruct((lx, HY_CH), F32)],
        compiler_params=_cparams(("arbitrary",)),
        name=f"hyena_filter_{lx}",
    )(z, w1p, b1, w2, b2, w3, w3, fr, win, ctab, stab)


def _hyena_body(x0_ref, x1_ref, v_ref, w0_ref, w1_ref, w2_ref, b0_ref, b1_ref, b2_ref, kre_ref, kim_ref,
                bias_ref, c_ref, s_ref, o_ref, x0_scr, wv_scr, wb_scr, yre_scr, yim_scr, *, lx):
    tc = HY_COLS
    ch = min(DFT_CHUNK, lx)
    row = lax.broadcasted_iota(jnp.int32, (lx, tc), 0)
    first = row == 0
    last = row == lx - 1

    def short_conv(x_ref, w_ref, b_ref):
        x = x_ref[...].astype(F32)
        xm = jnp.where(first, 0.0, pltpu.roll(x, 1, 0))
        xp = jnp.where(last, 0.0, pltpu.roll(x, lx - 1, 0))
        return xm * w_ref[0:1, :] + x * w_ref[1:2, :] + xp * w_ref[2:3, :] + b_ref[...]

    x0_scr[...] = short_conv(x0_ref, w0_ref, b0_ref)
    wv = short_conv(x1_ref, w1_ref, b1_ref) * short_conv(v_ref, w2_ref, b2_ref)
    wv_scr[...] = wv
    wb_scr[...] = wv.astype(BF16)
    sgn = jnp.where((row & 1) == 0, 1.0, -1.0)
    u_nyq = jnp.sum(wv * sgn, axis=0, keepdims=True)
    y_nyq = u_nyq * kim_ref[0:1, :]

    for kc in range(lx // ch):
        rows = slice(kc * ch, (kc + 1) * ch)
        wb = wb_scr[...]
        ure = jnp.dot(c_ref[rows, :], wb, preferred_element_type=F32)
        uim = -jnp.dot(s_ref[rows, :], wb, preferred_element_type=F32)
        kre = kre_ref[rows, :]
        kim = kim_ref[rows, :]
        yre = ure * kre - uim * kim
        yim = ure * kim + uim * kre
        if kc == 0:
            r0 = lax.broadcasted_iota(jnp.int32, (ch, tc), 0) == 0
            yre = jnp.where(r0, 0.5 * ure * kre, yre)
            yim = jnp.where(r0, 0.0, yim)
        yre_scr[rows, :] = yre.astype(BF16)
        yim_scr[rows, :] = yim.astype(BF16)

    inv_n = 1.0 / (2 * lx)
    for nc in range(lx // ch):
        rows = slice(nc * ch, (nc + 1) * ch)
        y = (jnp.dot(c_ref[rows, :], yre_scr[...], preferred_element_type=F32)
             - jnp.dot(s_ref[rows, :], yim_scr[...], preferred_element_type=F32))
        rr = lax.broadcasted_iota(jnp.int32, (ch, tc), 0)
        sg = jnp.where((rr & 1) == 0, inv_n, -inv_n)
        y = y * (2.0 * inv_n) + sg * y_nyq
        o_ref[rows, :] = x0_scr[rows, :] * (y + wv_scr[rows, :] * bias_ref[...])


def _hyena(lx, row_block0, p, short_w, short_b, kre, kim, bias, ctab, stab):
    tc = HY_COLS
    nj = HY_CH // tc
    seg = lambda s: pl.BlockSpec((lx, tc), lambda b, j: (row_block0 + b, s * nj + j))
    wseg = lambda s: pl.BlockSpec((SHORT_CONV, tc), lambda b, j: (0, s * nj + j))
    bseg = lambda s: pl.BlockSpec((1, tc), lambda b, j: (0, s * nj + j))
    col = lambda rows: pl.BlockSpec((rows, tc), lambda b, j: (0, j))
    tab = pl.BlockSpec((lx, lx), lambda b, j: (0, 0))
    return pl.pallas_call(
        functools.partial(_hyena_body, lx=lx),
        grid=(BATCH, nj),
        in_specs=[seg(0), seg(1), seg(2), wseg(0), wseg(1), wseg(2), bseg(0), bseg(1), bseg(2),
                  col(lx), col(lx), col(1), tab, tab],
        out_specs=pl.BlockSpec((lx, tc), lambda b, j: (b, j)),
        out_shape=jax.ShapeDtypeStruct((BATCH * lx, HY_CH), F32),
        scratch_shapes=[pltpu.VMEM((lx, tc), F32), pltpu.VMEM((lx, tc), F32), pltpu.VMEM((lx, tc), BF16),
                        pltpu.VMEM((lx, tc), BF16), pltpu.VMEM((lx, tc), BF16)],
        compiler_params=_cparams(("parallel", "parallel")),
        name=f"hyena_{lx}",
    )(p, p, p, short_w, short_w, short_w, short_b, short_b, short_b, kre, kim, bias, ctab, stab)


def _nabias_body(rpb_ref, o_ref, bm_scr):
    rb = pl.program_id(0)
    w = GRID_W

    @pl.when(rb == 0)
    def _():
        qc = lax.broadcasted_iota(jnp.int32, (w, w), 0)
        kc = lax.broadcasted_iota(jnp.int32, (w, w), 1)
        d = jnp.clip(kc - qc + (NA_WIN_COLS - 1), 0, 2 * NA_WIN_COLS - 2)
        c0 = jnp.clip(qc - NA_WIN_COLS // 2, 0, w - NA_WIN_COLS)
        ok = (kc >= c0) & (kc < c0 + NA_WIN_COLS)
        for h in range(NA_HEADS):
            for dr in range(2 * NA_WIN_ROWS - 1):
                acc = jnp.zeros((w, w), F32)
                for j in range(2 * NA_WIN_COLS - 1):
                    acc = jnp.where(d == j, rpb_ref[h, dr, j], acc)
                bm_scr[h, dr] = jnp.where(ok, acc, NEG_INF)

    start = jnp.clip(NA_QROWS * rb - NA_WIN_ROWS // 2, 0, GRID_ROWS - NA_KROWS)
    for h in range(NA_HEADS):
        for i in range(NA_QROWS):
            r = NA_QROWS * rb + i
            r0 = jnp.clip(r - NA_WIN_ROWS // 2, 0, GRID_ROWS - NA_WIN_ROWS)
            for j in range(NA_KROWS):
                kr = start + j
                valid = (kr >= r0) & (kr < r0 + NA_WIN_ROWS)
                didx = jnp.clip(kr - r + (NA_WIN_ROWS - 1), 0, 2 * NA_WIN_ROWS - 2)
                blk = jnp.where(valid, bm_scr[h, didx], NEG_INF)
                o_ref[0, h, i * w:(i + 1) * w, j * w:(j + 1) * w] = blk


def _na_bias(rpb):
    nrb = GRID_ROWS // NA_QROWS
    return pl.pallas_call(
        _nabias_body,
        grid=(nrb,),
        in_specs=[pl.BlockSpec(memory_space=pltpu.SMEM)],
        out_specs=pl.BlockSpec((1, NA_HEADS, NA_QROWS * GRID_W, NA_KROWS * GRID_W), lambda rb: (rb, 0, 0, 0)),
        out_shape=jax.ShapeDtypeStruct((nrb, NA_HEADS, NA_QROWS * GRID_W, NA_KROWS * GRID_W), F32),
        scratch_shapes=[pltpu.VMEM((NA_HEADS, 2 * NA_WIN_ROWS - 1, GRID_W, GRID_W), F32)],
        compiler_params=_cparams(("arbitrary",)),
        name="na_bias",
    )(rpb)


_NT = (((1,), (1,)), ((), ()))


def _na_body(q_ref, k_ref, v_ref, kc_ref, vc_ref, bias_ref, o_ref):
    rb = pl.program_id(1)
    start = jnp.clip(NA_QROWS * rb - NA_WIN_ROWS // 2, 0, GRID_ROWS - NA_KROWS)
    row0 = pl.multiple_of(start * GRID_W, GRID_W)
    nk = NA_KROWS * GRID_W
    scale = HEAD_DIM ** -0.5
    for h in range(NA_HEADS):
        cs = slice(h * HEAD_DIM, (h + 1) * HEAD_DIM)
        q = (q_ref[:, cs].astype(F32) * scale).astype(BF16)
        kw = k_ref[pl.ds(row0, nk), cs].astype(BF16)
        vw = v_ref[pl.ds(row0, nk), cs].astype(BF16)
        s = lax.dot_general(q, kw, _NT, preferred_element_type=F32)
        bm = bias_ref[0, h]
        s = jnp.where(bm > 0.5 * NEG_INF, s + bm, NEG_INF)
        sc = lax.dot_general(q, kc_ref[:, cs].astype(BF16), _NT, preferred_element_type=F32)
        m = jnp.maximum(jnp.max(s, axis=-1, keepdims=True), jnp.max(sc, axis=-1, keepdims=True))
        p = jnp.exp(s - m)
        pc = jnp.exp(sc - m)
        denom = jnp.sum(p, axis=-1, keepdims=True) + jnp.sum(pc, axis=-1, keepdims=True)
        o = (jnp.dot(p.astype(BF16), vw, preferred_element_type=F32)
             + jnp.dot(pc.astype(BF16), vc_ref[:, cs].astype(BF16), preferred_element_type=F32))
        o_ref[:, cs] = o / denom


def _na_latent(p, bias):
    nrb = GRID_ROWS // NA_QROWS
    tq = NA_QROWS * GRID_W
    cq, ck, cv = COL_NQ // NA_DIM, COL_NK // NA_DIM, COL_NV // NA_DIM
    cblk = T_LAT // CTX_LEN
    return pl.pallas_call(
        _na_body,
        grid=(BATCH, nrb),
        in_specs=[pl.BlockSpec((tq, NA_DIM), lambda b, r: (b * nrb + r, cq)),
                  pl.BlockSpec((SEQ, NA_DIM), lambda b, r: (b, ck)),
                  pl.BlockSpec((SEQ, NA_DIM), lambda b, r: (b, cv)),
                  pl.BlockSpec((CTX_LEN, NA_DIM), lambda b, r: (cblk + b, ck)),
                  pl.BlockSpec((CTX_LEN, NA_DIM), lambda b, r: (cblk + b, cv)),
                  pl.BlockSpec((1, NA_HEADS, tq, NA_KROWS * GRID_W), lambda b, r: (r, 0, 0, 0))],
        out_specs=pl.BlockSpec((tq, NA_DIM), lambda b, r: (b * nrb + r, 0)),
        out_shape=jax.ShapeDtypeStruct((T_LAT, NA_DIM), F32),
        compiler_params=_cparams(("parallel", "arbitrary")),
        name="na_latent",
    )(p, p, p, p, p, bias)


def _nactx_body(q_ref, k_ref, v_ref, o_ref):
    scale = HEAD_DIM ** -0.5
    for h in range(NA_HEADS):
        cs = slice(h * HEAD_DIM, (h + 1) * HEAD_DIM)
        q = (q_ref[:, cs].astype(F32) * scale).astype(BF16)
        s = lax.dot_general(q, k_ref[:, cs].astype(BF16), _NT, preferred_element_type=F32)
        m = jnp.max(s, axis=-1, keepdims=True)
        p = jnp.exp(s - m)
        denom = jnp.sum(p, axis=-1, keepdims=True)
        o = jnp.dot(p.astype(BF16), v_ref[:, cs].astype(BF16), preferred_element_type=F32)
        o_ref[:, cs] = o / denom


def _na_context(p):
    cq, ck, cv = COL_NQ // NA_DIM, COL_NK // NA_DIM, COL_NV // NA_DIM
    cblk = T_LAT // CTX_LEN
    spec = lambda c: pl.BlockSpec((CTX_LEN, NA_DIM), lambda b: (cblk + b, c))
    return pl.pallas_call(
        _nactx_body,
        grid=(BATCH,),
        in_specs=[spec(cq), spec(ck), spec(cv)],
        out_specs=pl.BlockSpec((CTX_LEN, NA_DIM), lambda b: (b, 0)),
        out_shape=jax.ShapeDtypeStruct((T_CTX, NA_DIM), F32),
        compiler_params=_cparams(("parallel",)),
        name="na_context",
    )(p, p, p)


_BNT = (((2,), (2,)), ((0,), (0,)))
_BNN = (((2,), (1,)), ((0,), (0,)))


def _rope_tables(lx):
    t = jnp.arange(lx)
    rows = (t // GRID_W).astype(F32)
    cols = (t % GRID_W).astype(F32)
    quarter = GLA_DK // 4
    inv = ROPE_BASE ** (-jnp.arange(quarter, dtype=F32) / quarter)
    ang_r = rows[:, None] * inv[None, :]
    ang_c = cols[:, None] * inv[None, :]
    cos = jnp.concatenate([jnp.cos(ang_r)] * 2 + [jnp.cos(ang_c)] * 2, axis=-1)
    sin = jnp.concatenate([-jnp.sin(ang_r), jnp.sin(ang_r), -jnp.sin(ang_c), jnp.sin(ang_c)], axis=-1)
    return jnp.tile(cos, (1, 2)), jnp.tile(sin, (1, 2))


def _log_sigmoid(z):
    return jnp.minimum(z, 0.0) - jnp.log(1.0 + jnp.exp(-jnp.abs(z)))


def _gla_body(*refs, lx, rope, with_out, with_init):
    it = iter(refs)
    q_ref = next(it) if with_out else None
    k_ref, v_ref = next(it), next(it)
    g_ref = next(it) if with_out else None
    ga_ref, w2_ref, b2_ref = next(it), next(it), next(it)
    cos_ref = sin_ref = None
    if rope:
        cos_ref, sin_ref = next(it), next(it)
    nw_ref = next(it) if with_out else None
    sf0_ref = sb0_ref = None
    if with_init:
        sf0_ref, sb0_ref = next(it), next(it)
    y_ref = next(it) if with_out else None
    sf_ref, sb_ref = next(it), next(it)
    ut_scr, stp_scr, dl_scr = next(it), next(it), next(it)

    c = GLA_CHUNK
    n = lx // c
    lane = lax.broadcasted_iota(jnp.int32, (lx, 2 * GLA_DK), 1)

    def rot(x):
        if not rope:
            return x
        even = ((lane // (GLA_DK // 4)) % 2) == 0
        partner = jnp.where(even, pltpu.roll(x, 2 * GLA_DK - GLA_DK // 4, 1), pltpu.roll(x, GLA_DK // 4, 1))
        return x * cos_ref[...] + partner * sin_ref[...]

    k = rot(k_ref[...].astype(F32))
    q = rot(q_ref[...].astype(F32) * (GLA_DK ** -0.5)) if with_out else None
    v3 = v_ref[...].reshape(n, c, 2 * GLA_DV)

    ga = ga_ref[...]
    log_f = _log_sigmoid(jnp.dot(ga, w2_ref[0], preferred_element_type=F32, precision=HIGHEST)
                         + b2_ref[0]) / GLA_GATE_NORM
    log_b = _log_sigmoid(jnp.dot(ga, w2_ref[1], preferred_element_type=F32, precision=HIGHEST)
                         + b2_ref[1]) / GLA_GATE_NORM
    pos = lax.broadcasted_iota(jnp.int32, (lx, 2 * GLA_DK), 0) % c
    cum_f, cum_b = log_f, log_b
    step = 1
    while step < c:
        cum_f = cum_f + jnp.where(pos >= step, pltpu.roll(cum_f, step, 0), 0.0)
        cum_b = cum_b + jnp.where(pos < c - step, pltpu.roll(cum_b, lx - step, 0), 0.0)
        step *= 2

    ti = lax.broadcasted_iota(jnp.int32, (n, c, c), 1)
    si = lax.broadcasted_iota(jnp.int32, (n, c, c), 2)
    o_acc = [None, None]
    for forward in (True, False):
        cum = cum_f if forward else cum_b
        cum3 = cum.reshape(n, c, 2 * GLA_DK)
        tot = cum3[:, c - 1:c, :] if forward else cum3[:, 0:1, :]
        kinv = (k * jnp.exp(-cum)).reshape(n, c, 2 * GLA_DK)
        kd = (kinv * jnp.exp(tot)).astype(BF16)
        kinv = kinv.astype(BF16)
        dl_scr[...] = jnp.exp(tot)
        qd3 = (q * jnp.exp(cum)).reshape(n, c, 2 * GLA_DK) if with_out else None
        keep = (ti >= si) if forward else (si >= ti)
        for hh in range(2):
            vh = v3[:, :, hh * GLA_DV:(hh + 1) * GLA_DV].astype(BF16)
            vt = jnp.swapaxes(vh, 1, 2)
            ut_scr[...] = lax.dot_general(vt, kd, _BNN, preferred_element_type=F32)
            st0 = (sf0_ref if forward else sb0_ref)[0, hh] if with_init else jnp.zeros((GLA_DV, 2 * GLA_DK), F32)

            def scan(i, st):
                ci = i if forward else n - 1 - i
                stp_scr[ci] = st
                return st * dl_scr[ci] + ut_scr[ci]

            st_fin = lax.fori_loop(0, n, scan, st0)
            (sf_ref if forward else sb_ref)[0, hh] = st_fin
            if with_out:
                in_head = (lax.broadcasted_iota(jnp.int32, (n, c, 2 * GLA_DK), 2) // GLA_DK) == hh
                qm = jnp.where(in_head, qd3, 0.0).astype(BF16)
                att = lax.dot_general(qm, kinv, _BNT, preferred_element_type=F32)
                att = jnp.where(keep, att, 0.0).astype(BF16)
                o = (lax.dot_general(att, vh, _BNN, preferred_element_type=F32)
                     + lax.dot_general(qm, stp_scr[...].astype(BF16), _BNT, preferred_element_type=F32))
                o_acc[hh] = o if o_acc[hh] is None else o_acc[hh] + o

    if with_out:
        for hh in range(2):
            vs = slice(hh * GLA_DV, (hh + 1) * GLA_DV)
            o = _rms_rows(o_acc[hh].reshape(lx, GLA_DV)) * nw_ref[...]
            y_ref[:, vs] = o * _silu(g_ref[:, vs].astype(F32))


def _gla(lx, row_block0, p, p_ga, w2cat, b2, norm_w, rope_tabs, init, with_out):
    npair = GLA_HEADS // 2
    rope = rope_tabs is not None
    with_init = init is not None
    rb = lambda b: row_block0 + b
    qk = lambda col: pl.BlockSpec((lx, 2 * GLA_DK), lambda b, h: (rb(b), col // (2 * GLA_DK) + h))
    vg = lambda col: pl.BlockSpec((lx, 2 * GLA_DV), lambda b, h: (rb(b), col // (2 * GLA_DV) + h))
    st_spec = pl.BlockSpec((1, 2, GLA_DV, 2 * GLA_DK), lambda b, h: (b, h, 0, 0))
    in_specs, args = [], []
    if with_out:
        in_specs.append(qk(COL_GQ)); args.append(p)
    in_specs += [qk(COL_GK), vg(COL_GV)]; args += [p, p]
    if with_out:
        in_specs.append(vg(COL_GG)); args.append(p)
    in_specs += [pl.BlockSpec((lx, 2 * GLA_RANK), lambda b, h: (rb(b), 0)),
                 pl.BlockSpec((2, 2 * GLA_RANK, 2 * GLA_DK), lambda b, h: (0, 0, h)),
                 pl.BlockSpec((2, 1, 2 * GLA_DK), lambda b, h: (0, 0, h))]
    args += [p_ga, w2cat, b2]
    if rope:
        in_specs += [pl.BlockSpec((lx, 2 * GLA_DK), lambda b, h: (0, 0))] * 2
        args += list(rope_tabs)
    if with_out:
        in_specs.append(pl.BlockSpec((1, GLA_DV), lambda b, h: (0, 0))); args.append(norm_w)
    if with_init:
        in_specs += [st_spec, st_spec]; args += list(init)
    st_shape = jax.ShapeDtypeStruct((BATCH, GLA_HEADS, GLA_DV, 2 * GLA_DK), F32)
    out_specs, out_shape = [], []
    if with_out:
        out_specs.append(pl.BlockSpec((lx, 2 * GLA_DV), lambda b, h: (b, h)))
        out_shape.append(jax.ShapeDtypeStruct((BATCH * lx, GLA_V), F32))
    out_specs += [st_spec, st_spec]
    out_shape += [st_shape, st_shape]
    n = lx // GLA_CHUNK
    outs = pl.pallas_call(
        functools.partial(_gla_body, lx=lx, rope=rope, with_out=with_out, with_init=with_init),
        grid=(BATCH, npair),
        in_specs=in_specs, out_specs=out_specs, out_shape=out_shape,
        scratch_shapes=[pltpu.VMEM((n, GLA_DV, 2 * GLA_DK), F32), pltpu.VMEM((n, GLA_DV, 2 * GLA_DK), F32),
                        pltpu.VMEM((n, 1, 2 * GLA_DK), F32)],
        compiler_params=_cparams(("parallel", "parallel")),
        name=f"gla_{lx}",
    )(*args)
    if with_out:
        return outs[0], outs[1], outs[2]
    return None, outs[0], outs[1]


def _outproj_body(*refs, with_ctx):
    it = iter(refs)
    lat = [next(it) for _ in range(3)]
    ctx = [next(it) for _ in range(3)] if with_ctx else None
    x_ref, mod_ref, hyw_ref, naw_ref, w_ref, g_ref, b_ref, o_ref = (next(it) for _ in range(8))

    def pick(j):
        if not with_ctx:
            return lat[j][...]
        return jnp.where(pl.program_id(0) >= T_LAT // ROW_TILE, ctx[j][...], lat[j][...])

    ya = (_rms_rows(pick(0)) * hyw_ref[...]).astype(BF16)
    yb = (_rms_rows(pick(1)) * naw_ref[...]).astype(BF16)
    yc = pick(2).astype(BF16)
    y = (jnp.dot(ya, w_ref[0:HY_CH, :], preferred_element_type=F32)
         + jnp.dot(yb, w_ref[HY_CH:HY_CH + NA_DIM, :], preferred_element_type=F32)
         + jnp.dot(yc, w_ref[HY_CH + NA_DIM:, :], preferred_element_type=F32))
    r = ALPHA * x_ref[...] + mod_ref[0, 2:3, :] * y
    o_ref[...] = _layer_norm_rows(r) * g_ref[...] + b_ref[...]


def _outproj(rows, lat, ctx, xa, mod, hy_norm_w, na_norm_w, w_out_bf, ln_g, ln_b):
    tm = ROW_TILE
    with_ctx = ctx is not None
    nlat = T_LAT // tm
    mrow = _mod_row(tm)
    widths = (HY_CH, NA_DIM, GLA_V)
    in_specs = [pl.BlockSpec((tm, w), lambda i: (jnp.minimum(i, nlat - 1), 0)) for w in widths]
    args = list(lat)
    if with_ctx:
        in_specs += [pl.BlockSpec((tm, w), lambda i: (jnp.maximum(i - nlat, 0), 0)) for w in widths]
        args += list(ctx)
    vec = lambda w: pl.BlockSpec((1, w), lambda i: (0, 0))
    in_specs += [pl.BlockSpec((tm, D_MODEL), lambda i: (i, 0)),
                 pl.BlockSpec((1, 6, D_MODEL), lambda i: (mrow(i), 0, 0)),
                 vec(HY_CH), vec(NA_DIM),
                 pl.BlockSpec((D_MODEL, D_MODEL), lambda i: (0, 0)),
                 vec(D_MODEL), vec(D_MODEL)]
    args += [xa, mod, hy_norm_w, na_norm_w, w_out_bf, ln_g, ln_b]
    return pl.pallas_call(
        functools.partial(_outproj_body, with_ctx=with_ctx),
        grid=(rows // tm,),
        in_specs=in_specs,
        out_specs=pl.BlockSpec((tm, D_MODEL), lambda i: (i, 0)),
        out_shape=jax.ShapeDtypeStruct((rows, D_MODEL), F32),
        compiler_params=_cparams(("parallel",)),
        name=f"outproj_{rows}",
    )(*args)


ROUTER_ROWS = 8 + N_EXPERTS


def _router_body(x_ref, mod_ref, w_ref, b_ref, h_ref, eid_ref, gw_ref):
    tm = ROW_TILE
    h2 = _layer_norm_rows(x_ref[...]) * (1.0 + mod_ref[0, 4:5, :]) + mod_ref[0, 3:4, :]
    _to_slabs(h_ref, h2)
    logits = lax.dot_general(w_ref[...], h2, _NT, preferred_element_type=F32, precision=HIGHEST) + b_ref[...]
    lg = logits[0:N_GROUPS, :]
    eg = jnp.exp(lg - jnp.max(lg, axis=0, keepdims=True))
    pg = eg / jnp.sum(eg, axis=0, keepdims=True)
    top_pg = jnp.max(pg, axis=0, keepdims=True)
    gio = lax.broadcasted_iota(jnp.int32, (N_GROUPS, tm), 0)
    gi = jnp.min(jnp.where(pg == top_pg, gio, N_GROUPS), axis=0, keepdims=True)
    le = logits[8:8 + N_EXPERTS, :].reshape(N_GROUPS, EXPERTS_PER_GROUP, tm)
    gsel = lax.broadcasted_iota(jnp.int32, (N_GROUPS, EXPERTS_PER_GROUP, tm), 0) == gi[None]
    les = jnp.sum(jnp.where(gsel, le, 0.0), axis=0)
    eio = lax.broadcasted_iota(jnp.int32, (EXPERTS_PER_GROUP, tm), 0)
    v1 = jnp.max(les, axis=0, keepdims=True)
    i1 = jnp.min(jnp.where(les == v1, eio, EXPERTS_PER_GROUP), axis=0, keepdims=True)
    les2 = jnp.where(eio == i1, -jnp.inf, les)
    v2 = jnp.max(les2, axis=0, keepdims=True)
    i2 = jnp.min(jnp.where(les2 == v2, eio, EXPERTS_PER_GROUP), axis=0, keepdims=True)
    t = jnp.exp(v2 - v1)
    w1 = top_pg / (1.0 + t)
    eid_ref[...] = jnp.concatenate([gi * EXPERTS_PER_GROUP + i1, gi * EXPERTS_PER_GROUP + i2], axis=0)
    gw_ref[...] = jnp.concatenate([w1, w1 * t], axis=0)


def _router(rows, x1, mod, w_rt, b_rt):
    tm = ROW_TILE
    mrow = _mod_row(tm)
    return pl.pallas_call(
        _router_body,
        grid=(rows // tm,),
        in_specs=[pl.BlockSpec((tm, D_MODEL), lambda i: (i, 0)),
                  pl.BlockSpec((1, 6, D_MODEL), lambda i: (mrow(i), 0, 0)),
                  pl.BlockSpec((ROUTER_ROWS, D_MODEL), lambda i: (0, 0)),
                  pl.BlockSpec((ROUTER_ROWS, 1), lambda i: (0, 0))],
        out_specs=[pl.BlockSpec((tm * SLAB, LANES), lambda i: (i, 0)),
                   pl.BlockSpec((2, tm), lambda i: (0, i)),
                   pl.BlockSpec((2, tm), lambda i: (0, i))],
        out_shape=[jax.ShapeDtypeStruct((rows * SLAB, LANES), F32),
                   jax.ShapeDtypeStruct((2, rows), jnp.int32),
                   jax.ShapeDtypeStruct((2, rows), F32)],
        compiler_params=_cparams(("parallel",)),
        name=f"router_{rows}",
    )(x1, mod, w_rt, b_rt)


def _moe_tiles(rows):
    return -(-(2 * rows + N_EXPERTS * (MOE_TILE - 1)) // MOE_TILE)


def _dispatch_plan(rows, eid):
    nt = _moe_tiles(rows)
    e = eid.reshape(-1)
    experts = jnp.arange(N_EXPERTS, dtype=jnp.int32)
    onehot = (e[:, None] == experts[None, :]).astype(jnp.int32)
    csum = jnp.cumsum(onehot, axis=0)
    rank = jnp.sum(csum * onehot, axis=1) - 1
    counts = csum[-1]
    padded = ((counts + MOE_TILE - 1) // MOE_TILE) * MOE_TILE
    pend = jnp.cumsum(padded)
    pstart = pend - padded
    dest = jnp.sum(onehot * pstart[None, :], axis=1) + rank
    tile_start = jnp.arange(nt, dtype=jnp.int32) * MOE_TILE
    tile_valid = (tile_start < pend[-1]).astype(jnp.int32)
    tile_expert = jnp.sum((pend[None, :] <= tile_start[:, None]).astype(jnp.int32), axis=1)
    last_expert = jnp.max(jnp.where(counts > 0, experts, 0))
    tile_expert = jnp.minimum(tile_expert, last_expert)
    first = jnp.sum(((pstart[None, :] == tile_start[:, None]) & (padded[None, :] > 0)).astype(jnp.int32), axis=1)
    tile_first = jnp.minimum(first, 1) * tile_valid
    return dict(dest=dest, pend=pend, counts=counts, tile_expert=tile_expert, tile_valid=tile_valid,
                tile_first=tile_first, n_valid=(pend[-1] // MOE_TILE).reshape(1))


def _slab_copy(src, src_tok, dst, dst_tok, sem):
    s0 = pl.multiple_of(src_tok * SLAB, SLAB)
    d0 = pl.multiple_of(dst_tok * SLAB, SLAB)
    return pltpu.make_async_copy(src.at[pl.ds(s0, SLAB), :], dst.at[pl.ds(d0, SLAB), :], sem)


def _dispatch_body(dest_ref, pend_ref, cnt_ref, h_ref, xs_hbm, zero_scr, sem_z, sem, *, rows):
    i = pl.program_id(0)
    tm = ROW_TILE
    tile_rows = MOE_TILE * SLAB
    nt = xs_hbm.shape[0] // tile_rows

    def zero_tile(tile):
        row0 = pl.multiple_of(tile * tile_rows, tile_rows)
        return pltpu.make_async_copy(zero_scr, xs_hbm.at[pl.ds(row0, tile_rows), :], sem_z)

    @pl.when(i == 0)
    def _():
        zero_scr[...] = jnp.zeros_like(zero_scr)
        n_used = pend_ref[N_EXPERTS - 1] // MOE_TILE
        for e in range(N_EXPERTS):
            @pl.when(cnt_ref[e] > 0)
            def _():
                zero_tile(pend_ref[e] // MOE_TILE - 1).start()

        def start_unused(j, carry):
            zero_tile(j).start()
            return carry

        lax.fori_loop(n_used, nt, start_unused, 0)
        for e in range(N_EXPERTS):
            @pl.when(cnt_ref[e] > 0)
            def _():
                zero_tile(0).wait()

        def wait_unused(j, carry):
            zero_tile(0).wait()
            return carry

        lax.fori_loop(n_used, nt, wait_unused, 0)

    def start(r, carry):
        for s in range(2):
            _slab_copy(h_ref, r, xs_hbm, dest_ref[s * rows + i * tm + r], sem).start()
        return carry

    lax.fori_loop(0, tm, start, 0, unroll=8)
    for s in range(2):
        pltpu.make_async_copy(h_ref, xs_hbm.at[pl.ds(0, tm * SLAB), :], sem).wait()


def _dispatch(rows, plan, h2):
    grid_spec = pltpu.PrefetchScalarGridSpec(
        num_scalar_prefetch=3,
        grid=(rows // ROW_TILE,),
        in_specs=[pl.BlockSpec((ROW_TILE * SLAB, LANES), lambda i, dest, pend, cnt: (i, 0))],
        out_specs=pl.BlockSpec(memory_space=pl.ANY),
        scratch_shapes=[pltpu.VMEM((MOE_TILE * SLAB, LANES), F32), pltpu.SemaphoreType.DMA, pltpu.SemaphoreType.DMA],
    )
    return pl.pallas_call(
        functools.partial(_dispatch_body, rows=rows),
        grid_spec=grid_spec,
        out_shape=jax.ShapeDtypeStruct((_moe_tiles(rows) * MOE_TILE * SLAB, LANES), F32),
        compiler_params=_cparams(("arbitrary",)),
        name=f"moe_dispatch_{rows}",
    )(plan["dest"], plan["pend"], plan["counts"], h2)


def _moe_body(te_ref, tv_ref, tf_ref, nv_ref, x_ref, wup_ref, wdn_ref, o_ref, wup_bf, wdn_bf):
    i = pl.program_id(0)
    tm = MOE_TILE

    @pl.when(tv_ref[i] == 1)
    def _():
        @pl.when(tf_ref[i] == 1)
        def _():
            ch = 256
            for r in range(D_MODEL // ch):
                wup_bf[r * ch:(r + 1) * ch, :] = wup_ref[r * ch:(r + 1) * ch, :].astype(BF16)
            for r in range(D_EXPERT // ch):
                wdn_bf[r * ch:(r + 1) * ch, :] = wdn_ref[r * ch:(r + 1) * ch, :].astype(BF16)

        x = _from_slabs(x_ref, tm).astype(BF16)
        gu = jnp.dot(x, wup_bf[...], preferred_element_type=F32)
        act = _silu(gu[:, :D_EXPERT]) * gu[:, D_EXPERT:]
        _to_slabs(o_ref, jnp.dot(act.astype(BF16), wdn_bf[...], preferred_element_type=F32))

    @pl.when(tv_ref[i] == 0)
    def _():
        o_ref[...] = jnp.zeros_like(o_ref)


def _moe_ffn(rows, layer, plan, xs, w_up, w_down):
    nt = _moe_tiles(rows)
    tm = MOE_TILE

    def wmap(i, te, tv, tf, nv):
        return (layer, te[i] // EXPERTS_PER_GROUP, te[i] % EXPERTS_PER_GROUP, 0, 0)

    grid_spec = pltpu.PrefetchScalarGridSpec(
        num_scalar_prefetch=4,
        grid=(nt,),
        in_specs=[pl.BlockSpec((tm * SLAB, LANES), lambda i, te, tv, tf, nv: (jnp.minimum(i, nv[0] - 1), 0)),
                  pl.BlockSpec((None, None, None, D_MODEL, 2 * D_EXPERT), wmap),
                  pl.BlockSpec((None, None, None, D_EXPERT, D_MODEL), wmap)],
        out_specs=pl.BlockSpec((tm * SLAB, LANES), lambda i, te, tv, tf, nv: (i, 0)),
        scratch_shapes=[pltpu.VMEM((D_MODEL, 2 * D_EXPERT), BF16), pltpu.VMEM((D_EXPERT, D_MODEL), BF16)],
    )
    return pl.pallas_call(
        _moe_body,
        grid_spec=grid_spec,
        out_shape=jax.ShapeDtypeStruct((nt * tm * SLAB, LANES), F32),
        compiler_params=_cparams(("arbitrary",)),
        name=f"moe_ffn_{rows}",
    )(plan["tile_expert"], plan["tile_valid"], plan["tile_first"], plan["n_valid"], xs, w_up, w_down)


def _combine_body(pos_ref, y_hbm, gw_ref, x_ref, mod_ref, g_ref, b_ref, o_ref, y_scr, sem, *, rows):
    i = pl.program_id(0)
    tm = ROW_TILE

    def gather(tile, buf):
        def start(r, carry):
            for s in range(2):
                _slab_copy(y_hbm, pos_ref[s * rows + tile * tm + r], y_scr.at[buf, s], r, sem.at[buf]).start()
            return carry

        lax.fori_loop(0, tm, start, 0, unroll=8)

    @pl.when(i == 0)
    def _():
        gather(0, 0)

    for buf in range(2):
        @pl.when(i % 2 == buf)
        def _():
            @pl.when(i + 1 < pl.num_programs(0))
            def _():
                gather(i + 1, 1 - buf)

            for s in range(2):
                pltpu.make_async_copy(y_hbm.at[pl.ds(0, tm * SLAB), :], y_scr.at[buf, s], sem.at[buf]).wait()
            f = (_from_slabs(y_scr.at[buf, 0], tm) * gw_ref[:, 0:1]
                 + _from_slabs(y_scr.at[buf, 1], tm) * gw_ref[:, 1:2])
            r = ALPHA * x_ref[...] + mod_ref[0, 5:6, :] * f
            o_ref[...] = _layer_norm_rows(r) * g_ref[...] + b_ref[...]


def _combine(rows, pos, ysort, gw_t, x1, mod, ln_g, ln_b):
    tm = ROW_TILE
    mrow = _mod_row(tm)
    vec = pl.BlockSpec((1, D_MODEL), lambda i, pos: (0, 0))
    grid_spec = pltpu.PrefetchScalarGridSpec(
        num_scalar_prefetch=1,
        grid=(rows // tm,),
        in_specs=[pl.BlockSpec(memory_space=pl.ANY),
                  pl.BlockSpec((tm, 2), lambda i, pos: (i, 0)),
                  pl.BlockSpec((tm, D_MODEL), lambda i, pos: (i, 0)),
                  pl.BlockSpec((1, 6, D_MODEL), lambda i, pos: (mrow(i), 0, 0)),
                  vec, vec],
        out_specs=pl.BlockSpec((tm, D_MODEL), lambda i, pos: (i, 0)),
        scratch_shapes=[pltpu.VMEM((2, 2, tm * SLAB, LANES), F32), pltpu.SemaphoreType.DMA((2,))],
    )
    return pl.pallas_call(
        functools.partial(_combine_body, rows=rows),
        grid_spec=grid_spec,
        out_shape=jax.ShapeDtypeStruct((rows, D_MODEL), F32),
        compiler_params=_cparams(("arbitrary",)),
        name=f"combine_{rows}",
    )(pos, ysort, gw_t, x1, mod, ln_g, ln_b)


def _layer(xa, c8, tabs, w, big, layer, last):
    row = lambda v: v.reshape(1, -1)
    mod = _ada(c8, layer, big["w_ada"], w["b_ada"]).reshape(8, 6, D_MODEL)
    p, p_ga = _inproj(xa, mod, w["w_in"][:, :P_COLS].astype(BF16), w["w_in"][:, P_COLS:].astype(BF16))

    w1p = jnp.pad(w["hy_f_w1"], ((0, 128 - w["hy_f_w1"].shape[0]), (0, 0)))
    filt_args = (w1p, row(w["hy_f_b1"]), w["hy_f_w2"], row(w["hy_f_b2"]), w["hy_f_w3"], row(w["hy_sin_freq"]))
    short_b, hy_bias = row(w["hy_short_b"]), row(w["hy_bias"])
    kre, kim = _hyena_filter(SEQ, tabs["z_lat"], tabs["win_lat"], tabs["c_lat"], tabs["s_lat"], *filt_args)
    ya = _hyena(SEQ, 0, p, w["hy_short_w"], short_b, kre, kim, hy_bias, tabs["c_lat"], tabs["s_lat"])

    yb = _na_latent(p, _na_bias(w["na_rpb"]))

    zpad = jnp.zeros((GLA_RANK, GLA_K), F32)
    w2cat = jnp.stack([jnp.concatenate([w["gla_a_w2"][0], zpad], axis=0),
                       jnp.concatenate([zpad, w["gla_a_w2"][1]], axis=0)])
    b2 = w["gla_a_b"].reshape(2, 1, GLA_K)
    nw = row(w["gla_norm_w"])
    ctx_blk = T_LAT // CTX_LEN
    yc_ctx, sf, sb = _gla(CTX_LEN, ctx_blk, p, p_ga, w2cat, b2, nw, None, None, with_out=True)
    yc, _, _ = _gla(SEQ, 0, p, p_ga, w2cat, b2, nw, tabs["rope"], (sf, sb), with_out=True)

    lat = (ya, yb, yc)
    ctx = None
    if not last:
        kre_c, kim_c = _hyena_filter(CTX_LEN, tabs["z_ctx"], tabs["win_ctx"], tabs["c_ctx"], tabs["s_ctx"], *filt_args)
        ya_c = _hyena(CTX_LEN, ctx_blk, p, w["hy_short_w"], short_b, kre_c, kim_c, hy_bias,
                      tabs["c_ctx"], tabs["s_ctx"])
        ctx = (ya_c, _na_context(p), yc_ctx)

    rows = T_LAT if last else T_ALL
    x1 = _outproj(rows, lat, ctx, xa, mod, row(w["hy_norm_w"]), row(w["na_norm_w"]),
                  w["w_out"].astype(BF16), row(w["ln1_g"]), row(w["ln1_b"]))

    pad4 = jnp.zeros((8 - N_GROUPS, D_MODEL), F32)
    w_rt = jnp.concatenate([w["w_rg"].T, pad4, w["w_re"].T], axis=0)
    b_rt = jnp.concatenate([w["b_rg"], jnp.zeros((8 - N_GROUPS,), F32), w["b_re"]]).reshape(ROUTER_ROWS, 1)
    h2, eid, gw = _router(rows, x1, mod, w_rt, b_rt)
    plan = _dispatch_plan(rows, eid)
    ysort = _moe_ffn(rows, layer, plan, _dispatch(rows, plan, h2), big["w_up"], big["w_down"])
    return _combine(rows, plan["dest"], ysort, gw.T, x1, mod, row(w["ln2_g"]), row(w["ln2_b"]))


_BIG_WEIGHTS = ("w_ada", "w_up", "w_down")
_LAYER_WEIGHTS = ("w_ada", "b_ada", "w_in", "hy_short_w", "hy_short_b", "hy_f_w1", "hy_f_b1", "hy_f_w2", "hy_f_b2",
                  "hy_f_w3", "hy_sin_freq", "hy_bias", "hy_norm_w", "na_rpb", "na_norm_w", "gla_a_w2", "gla_a_b",
                  "gla_norm_w", "w_out", "ln1_g", "ln1_b", "w_rg", "b_rg", "w_re", "b_re", "w_up", "w_down",
                  "ln2_g", "ln2_b")


def kernel(x, c, ctx, c_ctx, w_ada, b_ada, w_in, hy_short_w, hy_short_b, hy_f_w1, hy_f_b1, hy_f_w2, hy_f_b2, hy_f_w3, hy_sin_freq, hy_bias, hy_norm_w, na_rpb, na_norm_w, gla_a_w2, gla_a_b, gla_norm_w, w_out, ln1_g, ln1_b, w_rg, b_rg, w_re, b_re, w_up, w_down, ln2_g, ln2_b):
    stacked = dict(zip(_LAYER_WEIGHTS, (w_ada, b_ada, w_in, hy_short_w, hy_short_b, hy_f_w1, hy_f_b1, hy_f_w2, hy_f_b2,
                                        hy_f_w3, hy_sin_freq, hy_bias, hy_norm_w, na_rpb, na_norm_w, gla_a_w2,
                                        gla_a_b, gla_norm_w, w_out, ln1_g, ln1_b, w_rg, b_rg, w_re, b_re, w_up,
                                        w_down, ln2_g, ln2_b)))
    tabs = {}
    tabs["c_lat"], tabs["s_lat"] = _dft_tables(SEQ)
    tabs["c_ctx"], tabs["s_ctx"] = _dft_tables(CTX_LEN)
    tabs["z_lat"], tabs["win_lat"] = _hyena_features(SEQ)
    tabs["z_ctx"], tabs["win_ctx"] = _hyena_features(CTX_LEN)
    tabs["rope"] = _rope_tables(SEQ)

    xa = jnp.concatenate([x.reshape(T_LAT, D_MODEL), ctx.reshape(T_CTX, D_MODEL)], axis=0)
    c8 = jnp.concatenate([c, c_ctx[None, :], jnp.zeros((8 - BATCH - 1, D_MODEL), F32)], axis=0)
    big = {k: stacked[k] for k in _BIG_WEIGHTS}
    for i in range(DEPTH):
        small = {k: v[i] for k, v in stacked.items() if k not in _BIG_WEIGHTS}
        xa = _layer(xa, c8, tabs, small, big, i, last=(i == DEPTH - 1))
    return xa.reshape(BATCH, SEQ, D_MODEL)
```

```python
import functools
import math

import jax
import jax.numpy as jnp
import numpy as np
from jax import lax
from jax.experimental import pallas as pl
from jax.experimental.pallas import tpu as pltpu

F32 = jnp.float32
BF16 = jnp.bfloat16
HIGHEST = lax.Precision.HIGHEST

D_MODEL = 2048
BATCH = 4
SEQ = 2048
DEPTH = 2
CTX_LEN = 256
GRID_W = 64
HEAD_DIM = 128
HY_CH = 512
NA_HEADS = 6
NA_DIM = 768
GLA_HEADS = 6
GLA_DK = 64
GLA_DV = 128
GLA_K = 384
GLA_V = 768
GLA_RANK = 16
GLA_GATE_NORM = 16.0
GLA_CHUNK = 64
ROPE_BASE = 10000.0
SHORT_CONV = 3
HY_EMB_BANDS = 16
HY_FILTER_WIDTH = 64
HY_MIN_DECAY = math.log(1e-2) / 1.5
HY_MAX_DECAY = math.log(1e-2) / 0.3
NA_WIN_ROWS = 8
NA_WIN_COLS = 16
NEG_INF = -1e30
N_GROUPS = 4
EXPERTS_PER_GROUP = 8
N_EXPERTS = N_GROUPS * EXPERTS_PER_GROUP
D_EXPERT = 512
ALPHA = (2 * DEPTH) ** 0.25
LN_EPS = 1e-6

T_LAT = BATCH * SEQ
T_CTX = BATCH * CTX_LEN
T_ALL = T_LAT + T_CTX
GRID_ROWS = SEQ // GRID_W

COL_HY = 0
COL_NQ = 3 * HY_CH
COL_NK = COL_NQ + NA_DIM
COL_NV = COL_NK + NA_DIM
COL_GQ = COL_NV + NA_DIM
COL_GK = COL_GQ + GLA_K
COL_GV = COL_GK + GLA_K
COL_GG = COL_GV + GLA_V
COL_GA = COL_GG + GLA_V
P_COLS = COL_GA

VMEM_LIMIT_BYTES = 56 * 1024 * 1024
ROW_TILE = 256
INPROJ_ROWS = 1024
INPROJ_COLS = 512
INPROJ_CHUNK = 128
HY_COLS = 256
DFT_CHUNK = 512
NA_QROWS = 4
NA_KROWS = NA_QROWS + NA_WIN_ROWS - 1
MOE_TILE = 256
LANES = 128
SLAB = D_MODEL // LANES


def _to_slabs(ref, x):
    n = x.shape[0]
    for c in range(SLAB):
        ref[pl.ds(c, n, stride=SLAB), :] = x[:, c * LANES:(c + 1) * LANES]


def _from_slabs(ref, n):
    return jnp.concatenate([ref[pl.ds(c, n, stride=SLAB), :] for c in range(SLAB)], axis=1)


def _cparams(sem, vmem=VMEM_LIMIT_BYTES):
    return pltpu.CompilerParams(dimension_semantics=sem, vmem_limit_bytes=vmem)


def _layer_norm_rows(x):
    mu = jnp.mean(x, axis=-1, keepdims=True)
    xc = x - mu
    return xc * lax.rsqrt(jnp.mean(xc * xc, axis=-1, keepdims=True) + LN_EPS)


def _rms_rows(x):
    return x * lax.rsqrt(jnp.mean(x * x, axis=-1, keepdims=True) + LN_EPS)


def _silu(x):
    return x / (1.0 + jnp.exp(-x))


def _mod_row(tile_rows):
    per_batch = SEQ // tile_rows
    return lambda i: jnp.minimum(i // per_batch, BATCH)


def _ada_body(c_ref, w_ref, b_ref, o_ref):
    s = _silu(c_ref[...])
    o_ref[...] = jnp.dot(s.astype(BF16), w_ref[...].astype(BF16), preferred_element_type=F32) + b_ref[...]


def _ada(c8, layer, w_ada, b_ada):
    tn = 1024
    n = 6 * D_MODEL
    return pl.pallas_call(
        _ada_body,
        grid=(n // tn,),
        in_specs=[pl.BlockSpec((8, D_MODEL), lambda j: (0, 0)),
                  pl.BlockSpec((None, D_MODEL, tn), lambda j: (layer, 0, j)),
                  pl.BlockSpec((1, tn), lambda j: (0, j))],
        out_specs=pl.BlockSpec((8, tn), lambda j: (0, j)),
        out_shape=jax.ShapeDtypeStruct((8, n), F32),
        compiler_params=_cparams(("parallel",)),
        name="ada_mod",
    )(c8, w_ada, b_ada.reshape(1, n))


def _inproj_body(x_ref, mod_ref, w_ref, wga_ref, p_ref, ga_ref, h_scr):
    @pl.when(pl.program_id(1) == 0)
    def _():
        sh = mod_ref[0, 0:1, :]
        sc = mod_ref[0, 1:2, :]

        def chunk(r, carry):
            rows = pl.ds(pl.multiple_of(r * INPROJ_CHUNK, INPROJ_CHUNK), INPROJ_CHUNK)
            h = _layer_norm_rows(x_ref[rows, :]) * (1.0 + sc) + sh
            h_scr[rows, :] = h.astype(BF16)
            return carry

        lax.fori_loop(0, INPROJ_ROWS // INPROJ_CHUNK, chunk, 0)
        ga_ref[...] = jnp.dot(h_scr[...], wga_ref[...], preferred_element_type=F32)

    p_ref[...] = jnp.dot(h_scr[...], w_ref[...], preferred_element_type=F32).astype(p_ref.dtype)


def _inproj(xa, mod, w_in_bf, w_ga_bf):
    tm, tn = INPROJ_ROWS, INPROJ_COLS
    mrow = _mod_row(tm)
    return pl.pallas_call(
        _inproj_body,
        grid=(T_ALL // tm, P_COLS // tn),
        in_specs=[pl.BlockSpec((tm, D_MODEL), lambda i, j: (i, 0)),
                  pl.BlockSpec((1, 6, D_MODEL), lambda i, j: (mrow(i), 0, 0)),
                  pl.BlockSpec((D_MODEL, tn), lambda i, j: (0, j)),
                  pl.BlockSpec((D_MODEL, 2 * GLA_RANK), lambda i, j: (0, 0))],
        out_specs=[pl.BlockSpec((tm, tn), lambda i, j: (i, j)),
                   pl.BlockSpec((tm, 2 * GLA_RANK), lambda i, j: (i, 0))],
        out_shape=[jax.ShapeDtypeStruct((T_ALL, P_COLS), BF16),
                   jax.ShapeDtypeStruct((T_ALL, 2 * GLA_RANK), F32)],
        scratch_shapes=[pltpu.VMEM((tm, D_MODEL), BF16)],
        compiler_params=_cparams(("parallel", "arbitrary")),
        name="inproj",
    )(xa, mod, w_in_bf, w_ga_bf)


def _dft_tables(lx):
    n = 2 * lx
    k = jnp.arange(lx, dtype=jnp.int32)
    idx = (k[:, None] * k[None, :]) % n
    ang = idx.astype(F32) * (2.0 * math.pi / n)
    return jnp.cos(ang).astype(BF16), jnp.sin(ang).astype(BF16)


def _hyena_features(lx):
    n = jnp.arange(lx, dtype=F32)
    t = n / max(lx - 1, 1)
    bands = jnp.linspace(1e-4, HY_EMB_BANDS - 1, HY_EMB_BANDS, dtype=F32)
    ang = (2.0 * math.pi / lx) * n[:, None] * bands[None, :]
    z = jnp.concatenate([t[:, None], jnp.cos(ang), -jnp.sin(ang)], axis=-1)
    z = jnp.pad(z, ((0, 0), (0, 128 - z.shape[1])))
    deltas = jnp.abs(jnp.linspace(HY_MIN_DECAY, HY_MAX_DECAY, HY_CH, dtype=F32))
    win = jnp.exp(-t[:, None] * deltas[None, :])
    return z, win


def _split_dot(tab, x):
    xh = x.astype(BF16)
    xl = (x - xh.astype(F32)).astype(BF16)
    return (jnp.dot(tab, xh, preferred_element_type=F32) + jnp.dot(tab, xl, preferred_element_type=F32))


def _hyfilt_body(z_ref, w1_ref, b1_ref, w2_ref, b2_ref, w3f_ref, w3b_ref, fr_ref, win_ref, c_ref, s_ref,
                 kre_ref, kim_ref):
    fr = fr_ref[...]
    hid = jnp.sin(fr * (jnp.dot(z_ref[...], w1_ref[...], preferred_element_type=F32, precision=HIGHEST) + b1_ref[...]))
    hid = jnp.sin(fr * (jnp.dot(hid, w2_ref[...], preferred_element_type=F32, precision=HIGHEST) + b2_ref[...]))
    win = win_ref[...]
    hf = jnp.dot(hid, w3f_ref[...], preferred_element_type=F32, precision=HIGHEST) * win
    hb = jnp.dot(hid, w3b_ref[...], preferred_element_type=F32, precision=HIGHEST) * win
    row = lax.broadcasted_iota(jnp.int32, hf.shape, 0)
    hb = jnp.where(row == 0, 0.0, hb)
    hs = hf + hb
    hd = hf - hb
    sgn = jnp.where((row & 1) == 0, 1.0, -1.0)
    k_nyq = jnp.sum(hs * sgn, axis=0, keepdims=True)
    kre_ref[...] = _split_dot(c_ref[...], hs)
    kim = -_split_dot(s_ref[...], hd)
    kim_ref[...] = jnp.where(row == 0, k_nyq, kim)


def _hyena_filter(lx, z, win, ctab, stab, w1p, b1, w2, b2, w3, fr):
    tc = HY_COLS
    nj = HY_CH // tc
    full = lambda shape: pl.BlockSpec(shape, lambda j: (0,) * len(shape))
    return pl.pallas_call(
        _hyfilt_body,
        grid=(nj,),
        in_specs=[full((lx, 128)), full((128, HY_FILTER_WIDTH)), full((1, HY_FILTER_WIDTH)),
                  full((HY_FILTER_WIDTH, HY_FILTER_WIDTH)), full((1, HY_FILTER_WIDTH)),
                  pl.BlockSpec((HY_FILTER_WIDTH, tc), lambda j: (0, j)),
                  pl.BlockSpec((HY_FILTER_WIDTH, tc), lambda j: (0, nj + j)),
                  full((1, HY_FILTER_WIDTH)),
                  pl.BlockSpec((lx, tc), lambda j: (0, j)),
                  full((lx, lx)), full((lx, lx))],
        out_specs=[pl.BlockSpec((lx, tc), lambda j: (0, j)), pl.BlockSpec((lx, tc), lambda j: (0, j))],
        out_shape=[jax.ShapeDtypeStruct((lx, HY_CH), F32), jax.ShapeDtypeStruct((lx, HY_CH), F32)],
        compiler_params=_cparams(("arbitrary",)),
        name=f"hyena_filter_{lx}",
    )(z, w1p, b1, w2, b2, w3, w3, fr, win, ctab, stab)


def _hyena_body(x0_ref, x1_ref, v_ref, w0_ref, w1_ref, w2_ref, b0_ref, b1_ref, b2_ref, kre_ref, kim_ref,
                bias_ref, c_ref, s_ref, o_ref, x0_scr, wv_scr, wb_scr, yre_scr, yim_scr, *, lx):
    tc = HY_COLS
    ch = min(DFT_CHUNK, lx)
    row = lax.broadcasted_iota(jnp.int32, (lx, tc), 0)
    first = row == 0
    last = row == lx - 1

    def short_conv(x_ref, w_ref, b_ref):
        x = x_ref[...].astype(F32)
        xm = jnp.where(first, 0.0, pltpu.roll(x, 1, 0))
        xp = jnp.where(last, 0.0, pltpu.roll(x, lx - 1, 0))
        return xm * w_ref[0:1, :] + x * w_ref[1:2, :] + xp * w_ref[2:3, :] + b_ref[...]

    x0_scr[...] = short_conv(x0_ref, w0_ref, b0_ref)
    wv = short_conv(x1_ref, w1_ref, b1_ref) * short_conv(v_ref, w2_ref, b2_ref)
    wv_scr[...] = wv
    wb_scr[...] = wv.astype(BF16)
    sgn = jnp.where((row & 1) == 0, 1.0, -1.0)
    u_nyq = jnp.sum(wv * sgn, axis=0, keepdims=True)
    y_nyq = u_nyq * kim_ref[0:1, :]

    for kc in range(lx // ch):
        rows = slice(kc * ch, (kc + 1) * ch)
        wb = wb_scr[...]
        ure = jnp.dot(c_ref[rows, :], wb, preferred_element_type=F32)
        uim = -jnp.dot(s_ref[rows, :], wb, preferred_element_type=F32)
        kre = kre_ref[rows, :]
        kim = kim_ref[rows, :]
        yre = ure * kre - uim * kim
        yim = ure * kim + uim * kre
        if kc == 0:
            r0 = lax.broadcasted_iota(jnp.int32, (ch, tc), 0) == 0
            yre = jnp.where(r0, 0.5 * ure * kre, yre)
            yim = jnp.where(r0, 0.0, yim)
        yre_scr[rows, :] = yre.astype(BF16)
        yim_scr[rows, :] = yim.astype(BF16)

    inv_n = 1.0 / (2 * lx)
    for nc in range(lx // ch):
        rows = slice(nc * ch, (nc + 1) * ch)
        y = (jnp.dot(c_ref[rows, :], yre_scr[...], preferred_element_type=F32)
             - jnp.dot(s_ref[rows, :], yim_scr[...], preferred_element_type=F32))
        rr = lax.broadcasted_iota(jnp.int32, (ch, tc), 0)
        sg = jnp.where((rr & 1) == 0, inv_n, -inv_n)
        y = y * (2.0 * inv_n) + sg * y_nyq
        o_ref[rows, :] = x0_scr[rows, :] * (y + wv_scr[rows, :] * bias_ref[...])


def _hyena(lx, row_block0, p, short_w, short_b, kre, kim, bias, ctab, stab):
    tc = HY_COLS
    nj = HY_CH // tc
    seg = lambda s: pl.BlockSpec((lx, tc), lambda b, j: (row_block0 + b, s * nj + j))
    wseg = lambda s: pl.BlockSpec((SHORT_CONV, tc), lambda b, j: (0, s * nj + j))
    bseg = lambda s: pl.BlockSpec((1, tc), lambda b, j: (0, s * nj + j))
    col = lambda rows: pl.BlockSpec((rows, tc), lambda b, j: (0, j))
    tab = pl.BlockSpec((lx, lx), lambda b, j: (0, 0))
    return pl.pallas_call(
        functools.partial(_hyena_body, lx=lx),
        grid=(BATCH, nj),
        in_specs=[seg(0), seg(1), seg(2), wseg(0), wseg(1), wseg(2), bseg(0), bseg(1), bseg(2),
                  col(lx), col(lx), col(1), tab, tab],
        out_specs=pl.BlockSpec((lx, tc), lambda b, j: (b, j)),
        out_shape=jax.ShapeDtypeStruct((BATCH * lx, HY_CH), F32),
        scratch_shapes=[pltpu.VMEM((lx, tc), F32), pltpu.VMEM((lx, tc), F32), pltpu.VMEM((lx, tc), BF16),
                        pltpu.VMEM((lx, tc), BF16), pltpu.VMEM((lx, tc), BF16)],
        compiler_params=_cparams(("parallel", "parallel")),
        name=f"hyena_{lx}",
    )(p, p, p, short_w, short_w, short_w, short_b, short_b, short_b, kre, kim, bias, ctab, stab)


def _nabias_body(rpb_ref, o_ref, bm_scr):
    rb = pl.program_id(0)
    w = GRID_W

    @pl.when(rb == 0)
    def _():
        qc = lax.broadcasted_iota(jnp.int32, (w, w), 0)
        kc = lax.broadcasted_iota(jnp.int32, (w, w), 1)
        d = jnp.clip(kc - qc + (NA_WIN_COLS - 1), 0, 2 * NA_WIN_COLS - 2)
        c0 = jnp.clip(qc - NA_WIN_COLS // 2, 0, w - NA_WIN_COLS)
        ok = (kc >= c0) & (kc < c0 + NA_WIN_COLS)
        for h in range(NA_HEADS):
            for dr in range(2 * NA_WIN_ROWS - 1):
                acc = jnp.zeros((w, w), F32)
                for j in range(2 * NA_WIN_COLS - 1):
                    acc = jnp.where(d == j, rpb_ref[h, dr, j], acc)
                bm_scr[h, dr] = jnp.where(ok, acc, NEG_INF)

    start = jnp.clip(NA_QROWS * rb - NA_WIN_ROWS // 2, 0, GRID_ROWS - NA_KROWS)
    for h in range(NA_HEADS):
        for i in range(NA_QROWS):
            r = NA_QROWS * rb + i
            r0 = jnp.clip(r - NA_WIN_ROWS // 2, 0, GRID_ROWS - NA_WIN_ROWS)
            for j in range(NA_KROWS):
                kr = start + j
                valid = (kr >= r0) & (kr < r0 + NA_WIN_ROWS)
                didx = jnp.clip(kr - r + (NA_WIN_ROWS - 1), 0, 2 * NA_WIN_ROWS - 2)
                blk = jnp.where(valid, bm_scr[h, didx], NEG_INF)
                o_ref[0, h, i * w:(i + 1) * w, j * w:(j + 1) * w] = blk


def _na_bias(rpb):
    nrb = GRID_ROWS // NA_QROWS
    return pl.pallas_call(
        _nabias_body,
        grid=(nrb,),
        in_specs=[pl.BlockSpec(memory_space=pltpu.SMEM)],
        out_specs=pl.BlockSpec((1, NA_HEADS, NA_QROWS * GRID_W, NA_KROWS * GRID_W), lambda rb: (rb, 0, 0, 0)),
        out_shape=jax.ShapeDtypeStruct((nrb, NA_HEADS, NA_QROWS * GRID_W, NA_KROWS * GRID_W), F32),
        scratch_shapes=[pltpu.VMEM((NA_HEADS, 2 * NA_WIN_ROWS - 1, GRID_W, GRID_W), F32)],
        compiler_params=_cparams(("arbitrary",)),
        name="na_bias",
    )(rpb)


_NT = (((1,), (1,)), ((), ()))


def _na_body(q_ref, k_ref, v_ref, kc_ref, vc_ref, bias_ref, o_ref):
    rb = pl.program_id(1)
    start = jnp.clip(NA_QROWS * rb - NA_WIN_ROWS // 2, 0, GRID_ROWS - NA_KROWS)
    row0 = pl.multiple_of(start * GRID_W, GRID_W)
    nk = NA_KROWS * GRID_W
    scale = HEAD_DIM ** -0.5
    for h in range(NA_HEADS):
        cs = slice(h * HEAD_DIM, (h + 1) * HEAD_DIM)
        q = (q_ref[:, cs].astype(F32) * scale).astype(BF16)
        kw = k_ref[pl.ds(row0, nk), cs].astype(BF16)
        vw = v_ref[pl.ds(row0, nk), cs].astype(BF16)
        s = lax.dot_general(q, kw, _NT, preferred_element_type=F32)
        bm = bias_ref[0, h]
        s = jnp.where(bm > 0.5 * NEG_INF, s + bm, NEG_INF)
        sc = lax.dot_general(q, kc_ref[:, cs].astype(BF16), _NT, preferred_element_type=F32)
        m = jnp.maximum(jnp.max(s, axis=-1, keepdims=True), jnp.max(sc, axis=-1, keepdims=True))
        p = jnp.exp(s - m)
        pc = jnp.exp(sc - m)
        denom = jnp.sum(p, axis=-1, keepdims=True) + jnp.sum(pc, axis=-1, keepdims=True)
        o = (jnp.dot(p.astype(BF16), vw, preferred_element_type=F32)
             + jnp.dot(pc.astype(BF16), vc_ref[:, cs].astype(BF16), preferred_element_type=F32))
        o_ref[:, cs] = o / denom


def _na_latent(p, bias):
    nrb = GRID_ROWS // NA_QROWS
    tq = NA_QROWS * GRID_W
    cq, ck, cv = COL_NQ // NA_DIM, COL_NK // NA_DIM, COL_NV // NA_DIM
    cblk = T_LAT // CTX_LEN
    return pl.pallas_call(
        _na_body,
        grid=(BATCH, nrb),
        in_specs=[pl.BlockSpec((tq, NA_DIM), lambda b, r: (b * nrb + r, cq)),
                  pl.BlockSpec((SEQ, NA_DIM), lambda b, r: (b, ck)),
                  pl.BlockSpec((SEQ, NA_DIM), lambda b, r: (b, cv)),
                  pl.BlockSpec((CTX_LEN, NA_DIM), lambda b, r: (cblk + b, ck)),
                  pl.BlockSpec((CTX_LEN, NA_DIM), lambda b, r: (cblk + b, cv)),
                  pl.BlockSpec((1, NA_HEADS, tq, NA_KROWS * GRID_W), lambda b, r: (r, 0, 0, 0))],
        out_specs=pl.BlockSpec((tq, NA_DIM), lambda b, r: (b * nrb + r, 0)),
        out_shape=jax.ShapeDtypeStruct((T_LAT, NA_DIM), F32),
        compiler_params=_cparams(("parallel", "arbitrary")),
        name="na_latent",
    )(p, p, p, p, p, bias)


def _nactx_body(q_ref, k_ref, v_ref, o_ref):
    scale = HEAD_DIM ** -0.5
    for h in range(NA_HEADS):
        cs = slice(h * HEAD_DIM, (h + 1) * HEAD_DIM)
        q = (q_ref[:, cs].astype(F32) * scale).astype(BF16)
        s = lax.dot_general(q, k_ref[:, cs].astype(BF16), _NT, preferred_element_type=F32)
        m = jnp.max(s, axis=-1, keepdims=True)
        p = jnp.exp(s - m)
        denom = jnp.sum(p, axis=-1, keepdims=True)
        o = jnp.dot(p.astype(BF16), v_ref[:, cs].astype(BF16), preferred_element_type=F32)
        o_ref[:, cs] = o / denom


def _na_context(p):
    cq, ck, cv = COL_NQ // NA_DIM, COL_NK // NA_DIM, COL_NV // NA_DIM
    cblk = T_LAT // CTX_LEN
    spec = lambda c: pl.BlockSpec((CTX_LEN, NA_DIM), lambda b: (cblk + b, c))
    return pl.pallas_call(
        _nactx_body,
        grid=(BATCH,),
        in_specs=[spec(cq), spec(ck), spec(cv)],
        out_specs=pl.BlockSpec((CTX_LEN, NA_DIM), lambda b: (b, 0)),
        out_shape=jax.ShapeDtypeStruct((T_CTX, NA_DIM), F32),
        compiler_params=_cparams(("parallel",)),
        name="na_context",
    )(p, p, p)


_BNT = (((2,), (2,)), ((0,), (0,)))
_BNN = (((2,), (1,)), ((0,), (0,)))


def _rope_tables(lx):
    t = jnp.arange(lx)
    rows = (t // GRID_W).astype(F32)
    cols = (t % GRID_W).astype(F32)
    quarter = GLA_DK // 4
    inv = ROPE_BASE ** (-jnp.arange(quarter, dtype=F32) / quarter)
    ang_r = rows[:, None] * inv[None, :]
    ang_c = cols[:, None] * inv[None, :]
    cos = jnp.concatenate([jnp.cos(ang_r)] * 2 + [jnp.cos(ang_c)] * 2, axis=-1)
    sin = jnp.concatenate([-jnp.sin(ang_r), jnp.sin(ang_r), -jnp.sin(ang_c), jnp.sin(ang_c)], axis=-1)
    return jnp.tile(cos, (1, 2)), jnp.tile(sin, (1, 2))


def _log_sigmoid(z):
    return jnp.minimum(z, 0.0) - jnp.log(1.0 + jnp.exp(-jnp.abs(z)))


def _gla_body(*refs, lx, rope, with_out, with_init):
    it = iter(refs)
    q_ref = next(it) if with_out else None
    k_ref, v_ref = next(it), next(it)
    g_ref = next(it) if with_out else None
    ga_ref, w2_ref, b2_ref = next(it), next(it), next(it)
    cos_ref = sin_ref = None
    if rope:
        cos_ref, sin_ref = next(it), next(it)
    nw_ref = next(it) if with_out else None
    sf0_ref = sb0_ref = None
    if with_init:
        sf0_ref, sb0_ref = next(it), next(it)
    y_ref = next(it) if with_out else None
    sf_ref, sb_ref = next(it), next(it)
    ut_scr, stp_scr, dl_scr = next(it), next(it), next(it)

    c = GLA_CHUNK
    n = lx // c
    lane = lax.broadcasted_iota(jnp.int32, (lx, 2 * GLA_DK), 1)

    def rot(x):
        if not rope:
            return x
        even = ((lane // (GLA_DK // 4)) % 2) == 0
        partner = jnp.where(even, pltpu.roll(x, 2 * GLA_DK - GLA_DK // 4, 1), pltpu.roll(x, GLA_DK // 4, 1))
        return x * cos_ref[...] + partner * sin_ref[...]

    k = rot(k_ref[...].astype(F32))
    q = rot(q_ref[...].astype(F32) * (GLA_DK ** -0.5)) if with_out else None
    v3 = v_ref[...].reshape(n, c, 2 * GLA_DV)

    ga = ga_ref[...]
    log_f = _log_sigmoid(jnp.dot(ga, w2_ref[0], preferred_element_type=F32, precision=HIGHEST)
                         + b2_ref[0]) / GLA_GATE_NORM
    log_b = _log_sigmoid(jnp.dot(ga, w2_ref[1], preferred_element_type=F32, precision=HIGHEST)
                         + b2_ref[1]) / GLA_GATE_NORM
    pos = lax.broadcasted_iota(jnp.int32, (lx, 2 * GLA_DK), 0) % c
    cum_f, cum_b = log_f, log_b
    step = 1
    while step < c:
        cum_f = cum_f + jnp.where(pos >= step, pltpu.roll(cum_f, step, 0), 0.0)
        cum_b = cum_b + jnp.where(pos < c - step, pltpu.roll(cum_b, lx - step, 0), 0.0)
        step *= 2

    ti = lax.broadcasted_iota(jnp.int32, (n, c, c), 1)
    si = lax.broadcasted_iota(jnp.int32, (n, c, c), 2)
    o_acc = [None, None]
    for forward in (True, False):
        cum = cum_f if forward else cum_b
        cum3 = cum.reshape(n, c, 2 * GLA_DK)
        tot = cum3[:, c - 1:c, :] if forward else cum3[:, 0:1, :]
        kinv = (k * jnp.exp(-cum)).reshape(n, c, 2 * GLA_DK)
        kd = (kinv * jnp.exp(tot)).astype(BF16)
        kinv = kinv.astype(BF16)
        dl_scr[...] = jnp.exp(tot)
        qd3 = (q * jnp.exp(cum)).reshape(n, c, 2 * GLA_DK) if with_out else None
        keep = (ti >= si) if forward else (si >= ti)
        for hh in range(2):
            vh = v3[:, :, hh * GLA_DV:(hh + 1) * GLA_DV].astype(BF16)
            vt = jnp.swapaxes(vh, 1, 2)
            ut_scr[...] = lax.dot_general(vt, kd, _BNN, preferred_element_type=F32)
            st0 = (sf0_ref if forward else sb0_ref)[0, hh] if with_init else jnp.zeros((GLA_DV, 2 * GLA_DK), F32)

            def scan(i, st):
                ci = i if forward else n - 1 - i
                stp_scr[ci] = st
                return st * dl_scr[ci] + ut_scr[ci]

            st_fin = lax.fori_loop(0, n, scan, st0)
            (sf_ref if forward else sb_ref)[0, hh] = st_fin
            if with_out:
                in_head = (lax.broadcasted_iota(jnp.int32, (n, c, 2 * GLA_DK), 2) // GLA_DK) == hh
                qm = jnp.where(in_head, qd3, 0.0).astype(BF16)
                att = lax.dot_general(qm, kinv, _BNT, preferred_element_type=F32)
                att = jnp.where(keep, att, 0.0).astype(BF16)
                o = (lax.dot_general(att, vh, _BNN, preferred_element_type=F32)
                     + lax.dot_general(qm, stp_scr[...].astype(BF16), _BNT, preferred_element_type=F32))
                o_acc[hh] = o if o_acc[hh] is None else o_acc[hh] + o

    if with_out:
        for hh in range(2):
            vs = slice(hh * GLA_DV, (hh + 1) * GLA_DV)
            o = _rms_rows(o_acc[hh].reshape(lx, GLA_DV)) * nw_ref[...]
            y_ref[:, vs] = o * _silu(g_ref[:, vs].astype(F32))


def _gla(lx, row_block0, p, p_ga, w2cat, b2, norm_w, rope_tabs, init, with_out):
    npair = GLA_HEADS // 2
    rope = rope_tabs is not None
    with_init = init is not None
    rb = lambda b: row_block0 + b
    qk = lambda col: pl.BlockSpec((lx, 2 * GLA_DK), lambda b, h: (rb(b), col // (2 * GLA_DK) + h))
    vg = lambda col: pl.BlockSpec((lx, 2 * GLA_DV), lambda b, h: (rb(b), col // (2 * GLA_DV) + h))
    st_spec = pl.BlockSpec((1, 2, GLA_DV, 2 * GLA_DK), lambda b, h: (b, h, 0, 0))
    in_specs, args = [], []
    if with_out:
        in_specs.append(qk(COL_GQ)); args.append(p)
    in_specs += [qk(COL_GK), vg(COL_GV)]; args += [p, p]
    if with_out:
        in_specs.append(vg(COL_GG)); args.append(p)
    in_specs += [pl.BlockSpec((lx, 2 * GLA_RANK), lambda b, h: (rb(b), 0)),
                 pl.BlockSpec((2, 2 * GLA_RANK, 2 * GLA_DK), lambda b, h: (0, 0, h)),
                 pl.BlockSpec((2, 1, 2 * GLA_DK), lambda b, h: (0, 0, h))]
    args += [p_ga, w2cat, b2]
    if rope:
        in_specs += [pl.BlockSpec((lx, 2 * GLA_DK), lambda b, h: (0, 0))] * 2
        args += list(rope_tabs)
    if with_out:
        in_specs.append(pl.BlockSpec((1, GLA_DV), lambda b, h: (0, 0))); args.append(norm_w)
    if with_init:
        in_specs += [st_spec, st_spec]; args += list(init)
    st_shape = jax.ShapeDtypeStruct((BATCH, GLA_HEADS, GLA_DV, 2 * GLA_DK), F32)
    out_specs, out_shape = [], []
    if with_out:
        out_specs.append(pl.BlockSpec((lx, 2 * GLA_DV), lambda b, h: (b, h)))
        out_shape.append(jax.ShapeDtypeStruct((BATCH * lx, GLA_V), F32))
    out_specs += [st_spec, st_spec]
    out_shape += [st_shape, st_shape]
    n = lx // GLA_CHUNK
    outs = pl.pallas_call(
        functools.partial(_gla_body, lx=lx, rope=rope, with_out=with_out, with_init=with_init),
        grid=(BATCH, npair),
        in_specs=in_specs, out_specs=out_specs, out_shape=out_shape,
        scratch_shapes=[pltpu.VMEM((n, GLA_DV, 2 * GLA_DK), F32), pltpu.VMEM((n, GLA_DV, 2 * GLA_DK), F32),
                        pltpu.VMEM((n, 1, 2 * GLA_DK), F32)],
        compiler_params=_cparams(("parallel", "parallel")),
        name=f"gla_{lx}",
    )(*args)
    if with_out:
        return outs[0], outs[1], outs[2]
    return None, outs[0], outs[1]


def _outproj_body(*refs, with_ctx):
    it = iter(refs)
    lat = [next(it) for _ in range(3)]
    ctx = [next(it) for _ in range(3)] if with_ctx else None
    x_ref, mod_ref, hyw_ref, naw_ref, w_ref, g_ref, b_ref, o_ref = (next(it) for _ in range(8))

    def pick(j):
        if not with_ctx:
            return lat[j][...]
        return jnp.where(pl.program_id(0) >= T_LAT // ROW_TILE, ctx[j][...], lat[j][...])

    ya = (_rms_rows(pick(0)) * hyw_ref[...]).astype(BF16)
    yb = (_rms_rows(pick(1)) * naw_ref[...]).astype(BF16)
    yc = pick(2).astype(BF16)
    y = (jnp.dot(ya, w_ref[0:HY_CH, :], preferred_element_type=F32)
         + jnp.dot(yb, w_ref[HY_CH:HY_CH + NA_DIM, :], preferred_element_type=F32)
         + jnp.dot(yc, w_ref[HY_CH + NA_DIM:, :], preferred_element_type=F32))
    r = ALPHA * x_ref[...] + mod_ref[0, 2:3, :] * y
    o_ref[...] = _layer_norm_rows(r) * g_ref[...] + b_ref[...]


def _outproj(rows, lat, ctx, xa, mod, hy_norm_w, na_norm_w, w_out_bf, ln_g, ln_b):
    tm = ROW_TILE
    with_ctx = ctx is not None
    nlat = T_LAT // tm
    mrow = _mod_row(tm)
    widths = (HY_CH, NA_DIM, GLA_V)
    in_specs = [pl.BlockSpec((tm, w), lambda i: (jnp.minimum(i, nlat - 1), 0)) for w in widths]
    args = list(lat)
    if with_ctx:
        in_specs += [pl.BlockSpec((tm, w), lambda i: (jnp.maximum(i - nlat, 0), 0)) for w in widths]
        args += list(ctx)
    vec = lambda w: pl.BlockSpec((1, w), lambda i: (0, 0))
    in_specs += [pl.BlockSpec((tm, D_MODEL), lambda i: (i, 0)),
                 pl.BlockSpec((1, 6, D_MODEL), lambda i: (mrow(i), 0, 0)),
                 vec(HY_CH), vec(NA_DIM),
                 pl.BlockSpec((D_MODEL, D_MODEL), lambda i: (0, 0)),
                 vec(D_MODEL), vec(D_MODEL)]
    args += [xa, mod, hy_norm_w, na_norm_w, w_out_bf, ln_g, ln_b]
    return pl.pallas_call(
        functools.partial(_outproj_body, with_ctx=with_ctx),
        grid=(rows // tm,),
        in_specs=in_specs,
        out_specs=pl.BlockSpec((tm, D_MODEL), lambda i: (i, 0)),
        out_shape=jax.ShapeDtypeStruct((rows, D_MODEL), F32),
        compiler_params=_cparams(("parallel",)),
        name=f"outproj_{rows}",
    )(*args)


ROUTER_ROWS = 8 + N_EXPERTS


def _router_body(x_ref, mod_ref, w_ref, b_ref, h_ref, eid_ref, gw_ref):
    tm = ROW_TILE
    h2 = _layer_norm_rows(x_ref[...]) * (1.0 + mod_ref[0, 4:5, :]) + mod_ref[0, 3:4, :]
    _to_slabs(h_ref, h2)
    logits = lax.dot_general(w_ref[...], h2, _NT, preferred_element_type=F32, precision=HIGHEST) + b_ref[...]
    lg = logits[0:N_GROUPS, :]
    eg = jnp.exp(lg - jnp.max(lg, axis=0, keepdims=True))
    pg = eg / jnp.sum(eg, axis=0, keepdims=True)
    top_pg = jnp.max(pg, axis=0, keepdims=True)
    gio = lax.broadcasted_iota(jnp.int32, (N_GROUPS, tm), 0)
    gi = jnp.min(jnp.where(pg == top_pg, gio, N_GROUPS), axis=0, keepdims=True)
    le = logits[8:8 + N_EXPERTS, :].reshape(N_GROUPS, EXPERTS_PER_GROUP, tm)
    gsel = lax.broadcasted_iota(jnp.int32, (N_GROUPS, EXPERTS_PER_GROUP, tm), 0) == gi[None]
    les = jnp.sum(jnp.where(gsel, le, 0.0), axis=0)
    eio = lax.broadcasted_iota(jnp.int32, (EXPERTS_PER_GROUP, tm), 0)
    v1 = jnp.max(les, axis=0, keepdims=True)
    i1 = jnp.min(jnp.where(les == v1, eio, EXPERTS_PER_GROUP), axis=0, keepdims=True)
    les2 = jnp.where(eio == i1, -jnp.inf, les)
    v2 = jnp.max(les2, axis=0, keepdims=True)
    i2 = jnp.min(jnp.where(les2 == v2, eio, EXPERTS_PER_GROUP), axis=0, keepdims=True)
    t = jnp.exp(v2 - v1)
    w1 = top_pg / (1.0 + t)
    eid_ref[...] = jnp.concatenate([gi * EXPERTS_PER_GROUP + i1, gi * EXPERTS_PER_GROUP + i2], axis=0)
    gw_ref[...] = jnp.concatenate([w1, w1 * t], axis=0)


def _router(rows, x1, mod, w_rt, b_rt):
    tm = ROW_TILE
    mrow = _mod_row(tm)
    return pl.pallas_call(
        _router_body,
        grid=(rows // tm,),
        in_specs=[pl.BlockSpec((tm, D_MODEL), lambda i: (i, 0)),
                  pl.BlockSpec((1, 6, D_MODEL), lambda i: (mrow(i), 0, 0)),
                  pl.BlockSpec((ROUTER_ROWS, D_MODEL), lambda i: (0, 0)),
                  pl.BlockSpec((ROUTER_ROWS, 1), lambda i: (0, 0))],
        out_specs=[pl.BlockSpec((tm * SLAB, LANES), lambda i: (i, 0)),
                   pl.BlockSpec((2, tm), lambda i: (0, i)),
                   pl.BlockSpec((2, tm), lambda i: (0, i))],
        out_shape=[jax.ShapeDtypeStruct((rows * SLAB, LANES), F32),
                   jax.ShapeDtypeStruct((2, rows), jnp.int32),
                   jax.ShapeDtypeStruct((2, rows), F32)],
        compiler_params=_cparams(("parallel",)),
        name=f"router_{rows}",
    )(x1, mod, w_rt, b_rt)


def _moe_tiles(rows):
    return -(-(2 * rows + N_EXPERTS * (MOE_TILE - 1)) // MOE_TILE)


def _dispatch_plan(rows, eid):
    nt = _moe_tiles(rows)
    e = eid.reshape(-1)
    experts = jnp.arange(N_EXPERTS, dtype=jnp.int32)
    onehot = (e[:, None] == experts[None, :]).astype(jnp.int32)
    csum = jnp.cumsum(onehot, axis=0)
    rank = jnp.sum(csum * onehot, axis=1) - 1
    counts = csum[-1]
    padded = ((counts + MOE_TILE - 1) // MOE_TILE) * MOE_TILE
    pend = jnp.cumsum(padded)
    pstart = pend - padded
    dest = jnp.sum(onehot * pstart[None, :], axis=1) + rank
    tile_start = jnp.arange(nt, dtype=jnp.int32) * MOE_TILE
    tile_valid = (tile_start < pend[-1]).astype(jnp.int32)
    tile_expert = jnp.sum((pend[None, :] <= tile_start[:, None]).astype(jnp.int32), axis=1)
    last_expert = jnp.max(jnp.where(counts > 0, experts, 0))
    tile_expert = jnp.minimum(tile_expert, last_expert)
    first = jnp.sum(((pstart[None, :] == tile_start[:, None]) & (padded[None, :] > 0)).astype(jnp.int32), axis=1)
    tile_first = jnp.minimum(first, 1) * tile_valid
    return dict(dest=dest, pend=pend, counts=counts, tile_expert=tile_expert, tile_valid=tile_valid,
                tile_first=tile_first, n_valid=(pend[-1] // MOE_TILE).reshape(1))


def _slab_copy(src, src_tok, dst, dst_tok, sem):
    s0 = pl.multiple_of(src_tok * SLAB, SLAB)
    d0 = pl.multiple_of(dst_tok * SLAB, SLAB)
    return pltpu.make_async_copy(src.at[pl.ds(s0, SLAB), :], dst.at[pl.ds(d0, SLAB), :], sem)


def _dispatch_body(dest_ref, pend_ref, cnt_ref, h_ref, xs_hbm, zero_scr, sem_z, sem, *, rows):
    i = pl.program_id(0)
    tm = ROW_TILE
    tile_rows = MOE_TILE * SLAB
    nt = xs_hbm.shape[0] // tile_rows

    def zero_tile(tile):
        row0 = pl.multiple_of(tile * tile_rows, tile_rows)
        return pltpu.make_async_copy(zero_scr, xs_hbm.at[pl.ds(row0, tile_rows), :], sem_z)

    @pl.when(i == 0)
    def _():
        zero_scr[...] = jnp.zeros_like(zero_scr)
        n_used = pend_ref[N_EXPERTS - 1] // MOE_TILE
        for e in range(N_EXPERTS):
            @pl.when(cnt_ref[e] > 0)
            def _():
                zero_tile(pend_ref[e] // MOE_TILE - 1).start()

        def start_unused(j, carry):
            zero_tile(j).start()
            return carry

        lax.fori_loop(n_used, nt, start_unused, 0)
        for e in range(N_EXPERTS):
            @pl.when(cnt_ref[e] > 0)
            def _():
                zero_tile(0).wait()

        def wait_unused(j, carry):
            zero_tile(0).wait()
            return carry

        lax.fori_loop(n_used, nt, wait_unused, 0)

    def start(r, carry):
        for s in range(2):
            _slab_copy(h_ref, r, xs_hbm, dest_ref[s * rows + i * tm + r], sem).start()
        return carry

    lax.fori_loop(0, tm, start, 0, unroll=8)
    for s in range(2):
        pltpu.make_async_copy(h_ref, xs_hbm.at[pl.ds(0, tm * SLAB), :], sem).wait()


def _dispatch(rows, plan, h2):
    grid_spec = pltpu.PrefetchScalarGridSpec(
        num_scalar_prefetch=3,
        grid=(rows // ROW_TILE,),
        in_specs=[pl.BlockSpec((ROW_TILE * SLAB, LANES), lambda i, dest, pend, cnt: (i, 0))],
        out_specs=pl.BlockSpec(memory_space=pl.ANY),
        scratch_shapes=[pltpu.VMEM((MOE_TILE * SLAB, LANES), F32), pltpu.SemaphoreType.DMA, pltpu.SemaphoreType.DMA],
    )
    return pl.pallas_call(
        functools.partial(_dispatch_body, rows=rows),
        grid_spec=grid_spec,
        out_shape=jax.ShapeDtypeStruct((_moe_tiles(rows) * MOE_TILE * SLAB, LANES), F32),
        compiler_params=_cparams(("arbitrary",)),
        name=f"moe_dispatch_{rows}",
    )(plan["dest"], plan["pend"], plan["counts"], h2)


def _moe_body(te_ref, tv_ref, tf_ref, nv_ref, x_ref, wup_ref, wdn_ref, o_ref, wup_bf, wdn_bf):
    i = pl.program_id(0)
    tm = MOE_TILE

    @pl.when(tv_ref[i] == 1)
    def _():
        @pl.when(tf_ref[i] == 1)
        def _():
            ch = 256
            for r in range(D_MODEL // ch):
                wup_bf[r * ch:(r + 1) * ch, :] = wup_ref[r * ch:(r + 1) * ch, :].astype(BF16)
            for r in range(D_EXPERT // ch):
                wdn_bf[r * ch:(r + 1) * ch, :] = wdn_ref[r * ch:(r + 1) * ch, :].astype(BF16)

        x = _from_slabs(x_ref, tm).astype(BF16)
        gu = jnp.dot(x, wup_bf[...], preferred_element_type=F32)
        act = _silu(gu[:, :D_EXPERT]) * gu[:, D_EXPERT:]
        _to_slabs(o_ref, jnp.dot(act.astype(BF16), wdn_bf[...], preferred_element_type=F32))

    @pl.when(tv_ref[i] == 0)
    def _():
        o_ref[...] = jnp.zeros_like(o_ref)


def _moe_ffn(rows, layer, plan, xs, w_up, w_down):
    nt = _moe_tiles(rows)
    tm = MOE_TILE

    def wmap(i, te, tv, tf, nv):
        return (layer, te[i] // EXPERTS_PER_GROUP, te[i] % EXPERTS_PER_GROUP, 0, 0)

    grid_spec = pltpu.PrefetchScalarGridSpec(
        num_scalar_prefetch=4,
        grid=(nt,),
        in_specs=[pl.BlockSpec((tm * SLAB, LANES), lambda i, te, tv, tf, nv: (jnp.minimum(i, nv[0] - 1), 0)),
                  pl.BlockSpec((None, None, None, D_MODEL, 2 * D_EXPERT), wmap),
                  pl.BlockSpec((None, None, None, D_EXPERT, D_MODEL), wmap)],
        out_specs=pl.BlockSpec((tm * SLAB, LANES), lambda i, te, tv, tf, nv: (i, 0)),
        scratch_shapes=[pltpu.VMEM((D_MODEL, 2 * D_EXPERT), BF16), pltpu.VMEM((D_EXPERT, D_MODEL), BF16)],
    )
    return pl.pallas_call(
        _moe_body,
        grid_spec=grid_spec,
        out_shape=jax.ShapeDtypeStruct((nt * tm * SLAB, LANES), F32),
        compiler_params=_cparams(("arbitrary",)),
        name=f"moe_ffn_{rows}",
    )(plan["tile_expert"], plan["tile_valid"], plan["tile_first"], plan["n_valid"], xs, w_up, w_down)


def _combine_body(pos_ref, y_hbm, gw_ref, x_ref, mod_ref, g_ref, b_ref, o_ref, y_scr, sem, *, rows):
    i = pl.program_id(0)
    tm = ROW_TILE

    def gather(tile, buf):
        def start(r, carry):
            for s in range(2):
                _slab_copy(y_hbm, pos_ref[s * rows + tile * tm + r], y_scr.at[buf, s], r, sem.at[buf]).start()
            return carry

        lax.fori_loop(0, tm, start, 0, unroll=8)

    @pl.when(i == 0)
    def _():
        gather(0, 0)

    for buf in range(2):
        @pl.when(i % 2 == buf)
        def _():
            @pl.when(i + 1 < pl.num_programs(0))
            def _():
                gather(i + 1, 1 - buf)

            for s in range(2):
                pltpu.make_async_copy(y_hbm.at[pl.ds(0, tm * SLAB), :], y_scr.at[buf, s], sem.at[buf]).wait()
            f = (_from_slabs(y_scr.at[buf, 0], tm) * gw_ref[:, 0:1]
                 + _from_slabs(y_scr.at[buf, 1], tm) * gw_ref[:, 1:2])
            r = ALPHA * x_ref[...] + mod_ref[0, 5:6, :] * f
            o_ref[...] = _layer_norm_rows(r) * g_ref[...] + b_ref[...]


def _combine(rows, pos, ysort, gw_t, x1, mod, ln_g, ln_b):
    tm = ROW_TILE
    mrow = _mod_row(tm)
    vec = pl.BlockSpec((1, D_MODEL), lambda i, pos: (0, 0))
    grid_spec = pltpu.PrefetchScalarGridSpec(
        num_scalar_prefetch=1,
        grid=(rows // tm,),
        in_specs=[pl.BlockSpec(memory_space=pl.ANY),
                  pl.BlockSpec((tm, 2), lambda i, pos: (i, 0)),
                  pl.BlockSpec((tm, D_MODEL), lambda i, pos: (i, 0)),
                  pl.BlockSpec((1, 6, D_MODEL), lambda i, pos: (mrow(i), 0, 0)),
                  vec, vec],
        out_specs=pl.BlockSpec((tm, D_MODEL), lambda i, pos: (i, 0)),
        scratch_shapes=[pltpu.VMEM((2, 2, tm * SLAB, LANES), F32), pltpu.SemaphoreType.DMA((2,))],
    )
    return pl.pallas_call(
        functools.partial(_combine_body, rows=rows),
        grid_spec=grid_spec,
        out_shape=jax.ShapeDtypeStruct((rows, D_MODEL), F32),
        compiler_params=_cparams(("arbitrary",)),
        name=f"combine_{rows}",
    )(pos, ysort, gw_t, x1, mod, ln_g, ln_b)


def _layer(xa, c8, tabs, w, big, layer, last):
    row = lambda v: v.reshape(1, -1)
    mod = _ada(c8, layer, big["w_ada"], w["b_ada"]).reshape(8, 6, D_MODEL)
    p, p_ga = _inproj(xa, mod, w["w_in"][:, :P_COLS].astype(BF16), w["w_in"][:, P_COLS:].astype(BF16))

    w1p = jnp.pad(w["hy_f_w1"], ((0, 128 - w["hy_f_w1"].shape[0]), (0, 0)))
    filt_args = (w1p, row(w["hy_f_b1"]), w["hy_f_w2"], row(w["hy_f_b2"]), w["hy_f_w3"], row(w["hy_sin_freq"]))
    short_b, hy_bias = row(w["hy_short_b"]), row(w["hy_bias"])
    kre, kim = _hyena_filter(SEQ, tabs["z_lat"], tabs["win_lat"], tabs["c_lat"], tabs["s_lat"], *filt_args)
    ya = _hyena(SEQ, 0, p, w["hy_short_w"], short_b, kre, kim, hy_bias, tabs["c_lat"], tabs["s_lat"])

    yb = _na_latent(p, _na_bias(w["na_rpb"]))

    zpad = jnp.zeros((GLA_RANK, GLA_K), F32)
    w2cat = jnp.stack([jnp.concatenate([w["gla_a_w2"][0], zpad], axis=0),
                       jnp.concatenate([zpad, w["gla_a_w2"][1]], axis=0)])
    b2 = w["gla_a_b"].reshape(2, 1, GLA_K)
    nw = row(w["gla_norm_w"])
    ctx_blk = T_LAT // CTX_LEN
    yc_ctx, sf, sb = _gla(CTX_LEN, ctx_blk, p, p_ga, w2cat, b2, nw, None, None, with_out=True)
    yc, _, _ = _gla(SEQ, 0, p, p_ga, w2cat, b2, nw, tabs["rope"], (sf, sb), with_out=True)

    lat = (ya, yb, yc)
    ctx = None
    if not last:
        kre_c, kim_c = _hyena_filter(CTX_LEN, tabs["z_ctx"], tabs["win_ctx"], tabs["c_ctx"], tabs["s_ctx"], *filt_args)
        ya_c = _hyena(CTX_LEN, ctx_blk, p, w["hy_short_w"], short_b, kre_c, kim_c, hy_bias,
                      tabs["c_ctx"], tabs["s_ctx"])
        ctx = (ya_c, _na_context(p), yc_ctx)

    rows = T_LAT if last else T_ALL
    x1 = _outproj(rows, lat, ctx, xa, mod, row(w["hy_norm_w"]), row(w["na_norm_w"]),
                  w["w_out"].astype(BF16), row(w["ln1_g"]), row(w["ln1_b"]))

    pad4 = jnp.zeros((8 - N_GROUPS, D_MODEL), F32)
    w_rt = jnp.concatenate([w["w_rg"].T, pad4, w["w_re"].T], axis=0)
    b_rt = jnp.concatenate([w["b_rg"], jnp.zeros((8 - N_GROUPS,), F32), w["b_re"]]).reshape(ROUTER_ROWS, 1)
    h2, eid, gw = _router(rows, x1, mod, w_rt, b_rt)
    plan = _dispatch_plan(rows, eid)
    ysort = _moe_ffn(rows, layer, plan, _dispatch(rows, plan, h2), big["w_up"], big["w_down"])
    return _combine(rows, plan["dest"], ysort, gw.T, x1, mod, row(w["ln2_g"]), row(w["ln2_b"]))


_BIG_WEIGHTS = ("w_ada", "w_up", "w_down")
_LAYER_WEIGHTS = ("w_ada", "b_ada", "w_in", "hy_short_w", "hy_short_b", "hy_f_w1", "hy_f_b1", "hy_f_w2", "hy_f_b2",
                  "hy_f_w3", "hy_sin_freq", "hy_bias", "hy_norm_w", "na_rpb", "na_norm_w", "gla_a_w2", "gla_a_b",
                  "gla_norm_w", "w_out", "ln1_g", "ln1_b", "w_rg", "b_rg", "w_re", "b_re", "w_up", "w_down",
                  "ln2_g", "ln2_b")


def kernel(x, c, ctx, c_ctx, w_ada, b_ada, w_in, hy_short_w, hy_short_b, hy_f_w1, hy_f_b1, hy_f_w2, hy_f_b2, hy_f_w3, hy_sin_freq, hy_bias, hy_norm_w, na_rpb, na_norm_w, gla_a_w2, gla_a_b, gla_norm_w, w_out, ln1_g, ln1_b, w_rg, b_rg, w_re, b_re, w_up, w_down, ln2_g, ln2_b):
    stacked = dict(zip(_LAYER_WEIGHTS, (w_ada, b_ada, w_in, hy_short_w, hy_short_b, hy_f_w1, hy_f_b1, hy_f_w2, hy_f_b2,
                                        hy_f_w3, hy_sin_freq, hy_bias, hy_norm_w, na_rpb, na_norm_w, gla_a_w2,
                                        gla_a_b, gla_norm_w, w_out, ln1_g, ln1_b, w_rg, b_rg, w_re, b_re, w_up,
                                        w_down, ln2_g, ln2_b)))
    tabs = {}
    tabs["c_lat"], tabs["s_lat"] = _dft_tables(SEQ)
    tabs["c_ctx"], tabs["s_ctx"] = _dft_tables(CTX_LEN)
    tabs["z_lat"], tabs["win_lat"] = _hyena_features(SEQ)
    tabs["z_ctx"], tabs["win_ctx"] = _hyena_features(CTX_LEN)
    tabs["rope"] = _rope_tables(SEQ)

    xa = jnp.concatenate([x.reshape(T_LAT, D_MODEL), ctx.reshape(T_CTX, D_MODEL)], axis=0)
    c8 = jnp.concatenate([c, c_ctx[None, :], jnp.zeros((8 - BATCH - 1, D_MODEL), F32)], axis=0)
    big = {k: stacked[k] for k in _BIG_WEIGHTS}
    for i in range(DEPTH):
        small = {k: v[i] for k, v in stacked.items() if k not in _BIG_WEIGHTS}
        xa = _layer(xa, c8, tabs, small, big, i, last=(i == DEPTH - 1))
    return xa.reshape(BATCH, SEQ, D_MODEL)
```
